```python
import jax, jax.numpy as jnp
from jax import lax
import numpy as np

D_MODEL = 1024
BATCH = 8
SEQ = 2048
DEPTH = 1

GRID_W = 64
CTX_LEN = 256
RET_HEADS = 4
RET_DK = 256
RET_DV = 256
RET_QK = RET_HEADS * RET_DK
RET_V = RET_HEADS * RET_DV
RET_CHUNK = 128
MLP_WIDTH = 1024
MLP_GROUPS = 8
MLP_GROUP_DIM = MLP_WIDTH // MLP_GROUPS
MLP_CHUNK = 128
ROPE_BASE = 10000.0
EPS = 1e-6
IN_SIZES = (RET_QK, RET_QK, RET_V, RET_V, MLP_WIDTH, MLP_WIDTH, MLP_WIDTH, D_MODEL, D_MODEL)
IN_WIDTH = sum(IN_SIZES)

kernel_name = "hybrid_retention_chunkmlp_prefix_block"


def rms_norm(x, g):
    xf = x.astype(jnp.float32)
    y = xf * lax.rsqrt(jnp.mean(xf * xf, axis=-1, keepdims=True) + EPS)
    return (y * g.astype(jnp.float32)).astype(x.dtype)


def layer_norm(x, g, b):
    xf = x.astype(jnp.float32)
    mu = jnp.mean(xf, axis=-1, keepdims=True)
    xc = xf - mu
    y = xc * lax.rsqrt(jnp.mean(xc * xc, axis=-1, keepdims=True) + EPS)
    return (y * g.astype(jnp.float32) + b.astype(jnp.float32)).astype(x.dtype)


def ada_modulation(cond, w_mod, b_mod):
    m = jax.nn.silu(cond) @ w_mod + b_mod
    return jnp.split(m, 3, axis=-1)


def split_columns(p):
    out, start = [], 0
    for size in IN_SIZES:
        out.append(p[..., start:start + size])
        start += size
    return out


def to_heads(t, d):
    b, n, _ = t.shape
    return t.reshape(b, n, RET_HEADS, d).transpose(0, 2, 1, 3)


def axial_rope(t):
    n = t.shape[2]
    rows = n // GRID_W
    r = jnp.broadcast_to(jnp.arange(rows, dtype=jnp.float32)[:, None], (rows, GRID_W)).reshape(n)
    col = jnp.broadcast_to(jnp.arange(GRID_W, dtype=jnp.float32)[None, :], (rows, GRID_W)).reshape(n)
    half = t.shape[-1] // 2
    quarter = half // 2
    inv = jnp.power(ROPE_BASE, -jnp.arange(quarter, dtype=jnp.float32) / quarter)
    ang = jnp.concatenate([r[:, None] * inv, col[:, None] * inv], axis=-1)
    cos, sin = jnp.cos(ang), jnp.sin(ang)
    t1, t2 = t[..., :half], t[..., half:]
    return jnp.concatenate([t1 * cos - t2 * sin, t1 * sin + t2 * cos], axis=-1)


def retention_scan(q, k, v, log_gamma, init_state, include_diag):
    b, h, n, dk = q.shape
    dv = v.shape[-1]
    C = RET_CHUNK
    nc = n // C
    qc = q.reshape(b, h, nc, C, dk)
    kc = k.reshape(b, h, nc, C, dk)
    vc = v.reshape(b, h, nc, C, dv)
    idx = jnp.arange(C, dtype=jnp.float32)
    diff = idx[:, None] - idx[None, :]
    mask = diff >= 0 if include_diag else diff > 0
    lg = log_gamma[:, None, None]
    dmat = jnp.where(mask[None], jnp.exp(lg * jnp.where(mask, diff, 0.0)[None]), 0.0)
    scores = jnp.einsum('bhcid,bhcjd->bhcij', qc, kc) * dmat[None, :, None]
    intra = jnp.einsum('bhcij,bhcjv->bhciv', scores, vc)
    k_decay = jnp.exp(log_gamma[:, None] * (C - 1 - idx)[None])
    chunk_kv = jnp.einsum('bhcjd,bhcjv->cbhdv', kc * k_decay[None, :, None, :, None], vc)
    chunk_decay = jnp.exp(log_gamma * C)[None, :, None, None]

    def step(state, kv):
        return chunk_decay * state + kv, state

    _, starts = lax.scan(step, init_state, chunk_kv)
    q_decay = jnp.exp(log_gamma[:, None] * (idx + 1.0)[None])
    cross = jnp.einsum('bhcid,cbhdv->bhciv', qc, starts) * q_decay[None, :, None, :, None]
    return (intra + cross).reshape(b, h, n, dv)


def context_states(k, v, lg_f, lg_b):
    n = k.shape[2]
    pos = jnp.arange(n, dtype=jnp.float32)
    w_f = jnp.exp(lg_f[:, None] * (n - 1 - pos)[None])
    w_b = jnp.exp(lg_b[:, None] * pos[None])
    s_f = jnp.einsum('bhnd,hn,bhnv->bhdv', k, w_f, v)
    s_b = jnp.einsum('bhnd,hn,bhnv->bhdv', k, w_b, v)
    return s_f, s_b


def head_group_norm(y, g):
    mu = jnp.mean(y, axis=-1, keepdims=True)
    yc = y - mu
    yn = yc * lax.rsqrt(jnp.mean(yc * yc, axis=-1, keepdims=True) + EPS)
    b, h, n, d = y.shape
    return yn.transpose(0, 2, 1, 3).reshape(b, n, h * d) * g.astype(jnp.float32)


def chunk_spatial_gating(u, v_g, ln_g, ln_b, ws, bs):
    u = jax.nn.gelu(u)
    v_g = layer_norm(jax.nn.gelu(v_g), ln_g, ln_b)
    b, n, w = v_g.shape
    nc = n // MLP_CHUNK
    vc = v_g.reshape(b, nc, MLP_CHUNK, MLP_GROUPS, MLP_GROUP_DIM)
    mixed = jnp.einsum('gpq,bcqgd->bcpgd', ws, vc) + bs.T[None, None, :, :, None]
    return u * mixed.reshape(b, n, w)


def mixer_branches(parts, on_grid, state_f, state_b, lg_f, lg_b, ret_norm_g,
                   mlp_ln_g, mlp_ln_b, mlp_ws, mlp_bs, w_proj_a, w_proj_b, w_out):
    q, k, v, z_a, u, v_g, z_b, g_a, g_b = parts
    dtype = u.dtype
    qh = to_heads(q, RET_DK).astype(jnp.float32) * (RET_DK ** -0.5)
    kh = to_heads(k, RET_DK).astype(jnp.float32)
    vh = to_heads(v, RET_DV).astype(jnp.float32)
    if on_grid:
        qh, kh = axial_rope(qh), axial_rope(kh)
    ret_f = retention_scan(qh, kh, vh, lg_f, state_f, True)
    ret_b = retention_scan(qh[:, :, ::-1], kh[:, :, ::-1], vh[:, :, ::-1], lg_b, state_b, False)[:, :, ::-1]
    y_a = head_group_norm(ret_f + ret_b, ret_norm_g).astype(dtype) * jax.nn.silu(z_a)
    y_b = chunk_spatial_gating(u, v_g, mlp_ln_g, mlp_ln_b, mlp_ws, mlp_bs) * jax.nn.silu(z_b)
    merged = jax.nn.sigmoid(g_a) * (y_a @ w_proj_a) + jax.nn.sigmoid(g_b) * (y_b @ w_proj_b)
    return merged @ w_out


def setup_inputs(seed: int = 0) -> dict:
    key = jax.random.key(seed)
    ks = jax.random.split(key, 20)
    f32 = jnp.float32
    nrm = lambda k, s, sc: jax.random.normal(k, s, f32) * sc
    base_gamma = 1.0 - jnp.power(2.0, -5.0 - jnp.arange(RET_HEADS, dtype=f32))
    base_logit = jnp.log(base_gamma) - jnp.log1p(-base_gamma)
    return {
        'x': nrm(ks[0], (BATCH, SEQ, D_MODEL), 1.0),
        'c': nrm(ks[1], (BATCH, D_MODEL), 1.0),
        'ctx': nrm(ks[2], (BATCH, CTX_LEN, D_MODEL), 1.0),
        'c_ctx': nrm(ks[3], (D_MODEL,), 1.0),
        'w_mod': nrm(ks[4], (DEPTH, D_MODEL, 3 * D_MODEL), 0.5 * D_MODEL ** -0.5),
        'b_mod': nrm(ks[5], (DEPTH, 3 * D_MODEL), 0.01),
        'norm_g': 1.0 + nrm(ks[6], (DEPTH, D_MODEL), 0.02),
        'w_in': nrm(ks[7], (DEPTH, D_MODEL, IN_WIDTH), D_MODEL ** -0.5),
        'ret_decay_fwd': base_logit[None] + nrm(ks[8], (DEPTH, RET_HEADS), 0.1),
        'ret_decay_bwd': base_logit[None] + nrm(ks[9], (DEPTH, RET_HEADS), 0.1),
        'ret_norm_g': 1.0 + nrm(ks[10], (DEPTH, RET_V), 0.02),
        'mlp_ln_g': 1.0 + nrm(ks[11], (DEPTH, MLP_WIDTH), 0.02),
        'mlp_ln_b': nrm(ks[12], (DEPTH, MLP_WIDTH), 0.02),
        'mlp_ws': nrm(ks[13], (DEPTH, MLP_GROUPS, MLP_CHUNK, MLP_CHUNK), MLP_CHUNK ** -0.5),
        'mlp_bs': 1.0 + nrm(ks[14], (DEPTH, MLP_GROUPS, MLP_CHUNK), 0.1),
        'w_proj_a': nrm(ks[15], (DEPTH, RET_V, D_MODEL), RET_V ** -0.5),
        'w_proj_b': nrm(ks[16], (DEPTH, MLP_WIDTH, D_MODEL), MLP_WIDTH ** -0.5),
        'w_out': nrm(ks[17], (DEPTH, D_MODEL, D_MODEL), D_MODEL ** -0.5),
        'final_norm_g': 1.0 + nrm(ks[18], (D_MODEL,), 0.02),
    }


def reference(x, c, ctx, c_ctx, w_mod, b_mod, norm_g, w_in, ret_decay_fwd, ret_decay_bwd,
              ret_norm_g, mlp_ln_g, mlp_ln_b, mlp_ws, mlp_bs, w_proj_a, w_proj_b, w_out,
              final_norm_g):
    for l in range(DEPTH):
        last = l == DEPTH - 1
        sh_x, sc_x, gt_x = ada_modulation(c[:, None, :], w_mod[l], b_mod[l])
        sh_c, sc_c, gt_c = ada_modulation(c_ctx[None, None, :], w_mod[l], b_mod[l])
        hx = rms_norm(x, norm_g[l]) * (1.0 + sc_x) + sh_x
        hc = rms_norm(ctx, norm_g[l]) * (1.0 + sc_c) + sh_c
        lg_f = jax.nn.log_sigmoid(ret_decay_fwd[l].astype(jnp.float32))
        lg_b = jax.nn.log_sigmoid(ret_decay_bwd[l].astype(jnp.float32))
        if last:
            kv_c = hc @ w_in[l][:, RET_QK:2 * RET_QK + RET_V]
            kc_h = to_heads(kv_c[..., :RET_QK], RET_DK).astype(jnp.float32)
            vc_h = to_heads(kv_c[..., RET_QK:], RET_DV).astype(jnp.float32)
        else:
            pc = split_columns(hc @ w_in[l])
            kc_h = to_heads(pc[1], RET_DK).astype(jnp.float32)
            vc_h = to_heads(pc[2], RET_DV).astype(jnp.float32)
        s_f, s_b = context_states(kc_h, vc_h, lg_f, lg_b)
        px = split_columns(hx @ w_in[l])
        out_x = mixer_branches(px, True, s_f, s_b, lg_f, lg_b, ret_norm_g[l], mlp_ln_g[l], mlp_ln_b[l],
                               mlp_ws[l], mlp_bs[l], w_proj_a[l], w_proj_b[l], w_out[l])
        if not last:
            zeros = jnp.zeros_like(s_f)
            out_c = mixer_branches(pc, False, zeros, zeros, lg_f, lg_b, ret_norm_g[l], mlp_ln_g[l], mlp_ln_b[l],
                                   mlp_ws[l], mlp_bs[l], w_proj_a[l], w_proj_b[l], w_out[l])
            ctx = ctx + gt_c * out_c
        x = x + gt_x * out_x
    return rms_norm(x, final_norm_g)
```

```python
import functools

import numpy as np
import jax
import jax.numpy as jnp
from jax import lax
from jax.experimental import pallas as pl
from jax.experimental.pallas import tpu as pltpu

F32 = jnp.float32
BF16 = jnp.bfloat16

D_MODEL = 1024
HEADS = 4
HEAD_DIM = 256
HALF = HEAD_DIM // 2
GRID_W = 64
MLP_GROUPS = 8
MLP_CHUNK = 128
ROPE_BASE = 10000.0
EPS = 1e-6
N_COL_GROUPS = 9

RET_CHUNK = 256
PROJ_ROWS = 256
OUT_ROWS = 512
VMEM_LIMIT = 52 * 1024 * 1024


def _rope_tables(n):
    pos = np.arange(n)
    row = (pos // GRID_W).astype(np.float64)
    col = (pos % GRID_W).astype(np.float64)
    quarter = HALF // 2
    inv = np.power(ROPE_BASE, -np.arange(quarter, dtype=np.float64) / quarter)
    ang = np.concatenate([row[:, None] * inv, col[:, None] * inv], axis=-1)
    return np.cos(ang).astype(np.float32), np.sin(ang).astype(np.float32)


def _silu(x):
    return x * jax.nn.sigmoid(x)


def _mod_kernel(c_ref, w_ref, b_ref, o_ref):
    s = _silu(c_ref[...])
    o_ref[...] = jnp.dot(s, w_ref[...], preferred_element_type=F32,
                         precision=lax.Precision.HIGHEST) + b_ref[...]


def _modulation(cond, w_mod, b_mod):
    rows, d = cond.shape
    width = w_mod.shape[1]
    bn = 512
    return pl.pallas_call(
        _mod_kernel,
        grid=(width // bn,),
        in_specs=[
            pl.BlockSpec((rows, d), lambda j: (0, 0)),
            pl.BlockSpec((d, bn), lambda j: (0, j)),
            pl.BlockSpec((1, bn), lambda j: (0, j)),
        ],
        out_specs=pl.BlockSpec((rows, bn), lambda j: (0, j)),
        out_shape=jax.ShapeDtypeStruct((rows, width), F32),
        name="mod",
    )(cond, w_mod, b_mod)


def _modulated_norm(x, ng, sh, sc):
    ms = jnp.mean(x * x, axis=-1, keepdims=True)
    h = x * lax.rsqrt(ms + EPS) * ng
    return h * (1.0 + sc) + sh


def _ctx_proj_kernel(x_ref, sh_ref, sc_ref, ng_ref, wk_ref, wv_ref, k_ref, v_ref):
    hb = _modulated_norm(x_ref[0], ng_ref[...], sh_ref[...], sc_ref[...]).astype(BF16)
    pk = jnp.dot(hb, wk_ref[...], preferred_element_type=F32)
    pv = jnp.dot(hb, wv_ref[...], preferred_element_type=F32)
    for h in range(HEADS):
        k_ref[0, h] = pk[:, h * HEAD_DIM:(h + 1) * HEAD_DIM].astype(BF16)
        v_ref[0, h] = pv[:, h * HEAD_DIM:(h + 1) * HEAD_DIM].astype(BF16)


def _ctx_proj(ctx, mod_c, ng, w_in_b):
    b, n, d = ctx.shape
    head_out = jax.ShapeDtypeStruct((b, HEADS, n, HEAD_DIM), BF16)
    head_spec = pl.BlockSpec((1, HEADS, n, HEAD_DIM), lambda i: (i, 0, 0, 0))
    return pl.pallas_call(
        _ctx_proj_kernel,
        grid=(b,),
        in_specs=[
            pl.BlockSpec((1, n, d), lambda i: (i, 0, 0)),
            pl.BlockSpec((1, d), lambda i: (0, 0)),
            pl.BlockSpec((1, d), lambda i: (0, 1)),
            pl.BlockSpec((1, d), lambda i: (0, 0)),
            pl.BlockSpec((d, d), lambda i: (0, 1)),
            pl.BlockSpec((d, d), lambda i: (0, 2)),
        ],
        out_specs=[head_spec, head_spec],
        out_shape=[head_out, head_out],
        name="ctx_proj",
    )(ctx, mod_c, mod_c, ng, w_in_b, w_in_b)


def _proj_kernel(x_ref, sh_ref, sc_ref, ng_ref, w_ref, cos_ref, sin_ref, lng_ref, lnb_ref,
                 q_ref, k_ref, v_ref, za_ref, u_ref, vg_ref, zb_ref, ga_ref, gb_ref):
    hb = _modulated_norm(x_ref[0], ng_ref[...], sh_ref[0], sc_ref[0]).astype(BF16)
    cos = cos_ref[...]
    sin = sin_ref[...]

    def group(g):
        return jnp.dot(hb, w_ref[:, g * D_MODEL:(g + 1) * D_MODEL], preferred_element_type=F32)

    def store_rope(p, o_ref):
        for h in range(HEADS):
            t1 = p[:, h * HEAD_DIM:h * HEAD_DIM + HALF]
            t2 = p[:, h * HEAD_DIM + HALF:(h + 1) * HEAD_DIM]
            o_ref[0, h, :, :HALF] = (t1 * cos - t2 * sin).astype(BF16)
            o_ref[0, h, :, HALF:] = (t1 * sin + t2 * cos).astype(BF16)

    def store_heads(p, o_ref):
        for h in range(HEADS):
            o_ref[0, h] = p[:, h * HEAD_DIM:(h + 1) * HEAD_DIM].astype(BF16)

    store_rope(group(0), q_ref)
    store_rope(group(1), k_ref)
    store_heads(group(2), v_ref)
    store_heads(_silu(group(3)), za_ref)
    u_ref[0] = jax.nn.gelu(group(4)).astype(BF16)
    t = jax.nn.gelu(group(5))
    mu = jnp.mean(t, axis=-1, keepdims=True)
    tc = t - mu
    tn = tc * lax.rsqrt(jnp.mean(tc * tc, axis=-1, keepdims=True) + EPS)
    vg_ref[0] = (tn * lng_ref[...] + lnb_ref[...]).astype(BF16)
    zb_ref[0] = _silu(group(6)).astype(BF16)
    ga_ref[0] = jax.nn.sigmoid(group(7)).astype(BF16)
    gb_ref[0] = jax.nn.sigmoid(group(8)).astype(BF16)


def _proj(x, mod_x, ng, w_in_b, cos, sin, ln_g, ln_b):
    b, n, d = x.shape
    tm = PROJ_ROWS
    head_out = jax.ShapeDtypeStruct((b, HEADS, n, HEAD_DIM), BF16)
    flat_out = jax.ShapeDtypeStruct((b, n, d), BF16)
    head_spec = pl.BlockSpec((1, HEADS, tm, HEAD_DIM), lambda i, j: (i, 0, j, 0))
    flat_spec = pl.BlockSpec((1, tm, d), lambda i, j: (i, j, 0))
    vec = pl.BlockSpec((1, d), lambda i, j: (0, 0))
    return pl.pallas_call(
        _proj_kernel,
        grid=(b, n // tm),
        in_specs=[
            pl.BlockSpec((1, tm, d), lambda i, j: (i, j, 0)),
            pl.BlockSpec((1, 1, d), lambda i, j: (i, 0, 0)),
            pl.BlockSpec((1, 1, d), lambda i, j: (i, 0, 1)),
            vec,
            pl.BlockSpec(w_in_b.shape, lambda i, j: (0, 0), pipeline_mode=pl.Buffered(1)),
            pl.BlockSpec((tm, HALF), lambda i, j: (j, 0)),
            pl.BlockSpec((tm, HALF), lambda i, j: (j, 0)),
            vec, vec,
        ],
        out_specs=[head_spec] * 4 + [flat_spec] * 5,
        out_shape=[head_out] * 4 + [flat_out] * 5,
        compiler_params=pltpu.CompilerParams(
            dimension_semantics=("arbitrary", "arbitrary"), vmem_limit_bytes=VMEM_LIMIT),
        name="proj",
    )(x, mod_x, mod_x, ng, w_in_b, cos, sin, ln_g, ln_b)


def _dot_tn(a, b):
    return lax.dot_general(a, b, (((0,), (0,)), ((), ())), preferred_element_type=F32)


def _dot_nt(a, b):
    return lax.dot_general(a, b, (((1,), (1,)), ((), ())), preferred_element_type=F32)


def _ret_kernel(lg_ref, q_ref, k_ref, v_ref, za_ref, kc_ref, vc_ref, rg_ref, o_ref,
                sf_ref, tb_ref, *, n_chunks, n_ctx):
    C = RET_CHUNK
    h = pl.program_id(1)
    lgf = lg_ref[0, h]
    lgb = lg_ref[1, h]

    row = lax.broadcasted_iota(jnp.int32, (C, 1), 0).astype(F32)
    ii = lax.broadcasted_iota(jnp.int32, (C, C), 0)
    jj = lax.broadcasted_iota(jnp.int32, (C, C), 1)
    diff = (ii - jj).astype(F32)
    dmat = jnp.where(diff >= 0.0,
                     jnp.exp(lgf * jnp.maximum(diff, 0.0)),
                     jnp.exp(lgb * jnp.maximum(-diff, 0.0)))
    kdec_f = jnp.exp(lgf * (C - 1.0 - row))
    kdec_b = jnp.exp(lgb * row)
    qdec_f = jnp.exp(lgf * (row + 1.0))
    qdec_b = jnp.exp(lgb * (C - row))
    one = jnp.ones((1, 1), F32)
    cdec_f = jnp.exp(one * (lgf * C))
    cdec_b = jnp.exp(one * (lgb * C))

    crow = lax.broadcasted_iota(jnp.int32, (n_ctx, 1), 0).astype(F32)
    kc = kc_ref[0, 0].astype(F32)
    vc = vc_ref[0, 0]
    s_f = _dot_tn((kc * jnp.exp(lgf * (n_ctx - 1.0 - crow))).astype(BF16), vc)
    s_b = _dot_tn((kc * jnp.exp(lgb * crow)).astype(BF16), vc)

    t = s_b
    for c in range(n_chunks - 1, -1, -1):
        tb_ref[c] = t.astype(BF16)
        if c > 0:
            kk = k_ref[0, 0, c * C:(c + 1) * C, :].astype(F32)
            vv = v_ref[0, 0, c * C:(c + 1) * C, :]
            t = cdec_b * t + _dot_tn((kk * kdec_b).astype(BF16), vv)

    sf_ref[...] = s_f
    rg = rg_ref[...]
    for c in range(n_chunks):
        sl = slice(c * C, (c + 1) * C)
        qq = q_ref[0, 0, sl, :]
        kk = k_ref[0, 0, sl, :]
        vv = v_ref[0, 0, sl, :]
        scores = (_dot_nt(qq, kk) * dmat).astype(BF16)
        o = jnp.dot(scores, vv, preferred_element_type=F32)
        o = o + qdec_f * jnp.dot(qq, sf_ref[...].astype(BF16), preferred_element_type=F32)
        o = o + qdec_b * jnp.dot(qq, tb_ref[c], preferred_element_type=F32)
        if c + 1 < n_chunks:
            sf_ref[...] = cdec_f * sf_ref[...] + _dot_tn((kk.astype(F32) * kdec_f).astype(BF16), vv)
        mu = jnp.mean(o, axis=-1, keepdims=True)
        oc = o - mu
        yn = oc * lax.rsqrt(jnp.mean(oc * oc, axis=-1, keepdims=True) + EPS)
        o_ref[0, 0, sl, :] = (yn * rg * za_ref[0, 0, sl, :].astype(F32)).astype(BF16)


def _retention(lg, q, k, v, za, kc, vc, ret_g):
    b, hh, n, dh = q.shape
    n_ctx = kc.shape[2]
    n_chunks = n // RET_CHUNK
    seq = pl.BlockSpec((1, 1, n, dh), lambda i, j: (i, j, 0, 0))
    cseq = pl.BlockSpec((1, 1, n_ctx, dh), lambda i, j: (i, j, 0, 0))
    return pl.pallas_call(
        functools.partial(_ret_kernel, n_chunks=n_chunks, n_ctx=n_ctx),
        grid=(b, hh),
        in_specs=[
            pl.BlockSpec(memory_space=pltpu.SMEM),
            seq, seq, seq, seq, cseq, cseq,
            pl.BlockSpec((1, dh), lambda i, j: (0, j)),
        ],
        out_specs=seq,
        out_shape=jax.ShapeDtypeStruct((b, hh, n, dh), BF16),
        scratch_shapes=[
            pltpu.VMEM((dh, dh), F32),
            pltpu.VMEM((n_chunks, dh, dh), BF16),
        ],
        compiler_params=pltpu.CompilerParams(
            dimension_semantics=("arbitrary", "arbitrary"), vmem_limit_bytes=VMEM_LIMIT),
        name="ret",
    )(lg, q, k, v, za, kc, vc, ret_g)


def _out_kernel(x_ref, ya_ref, u_ref, vg_ref, zb_ref, ga_ref, gb_ref, gt_ref, ws_ref, bias_ref,
                wpa_ref, wpb_ref, wo_ref, fg_ref, o_ref, mix_ref):
    tm = x_ref.shape[1]
    n_sub = tm // MLP_CHUNK
    gw = D_MODEL // MLP_GROUPS
    for g in range(MLP_GROUPS):
        cols = slice(g * gw, (g + 1) * gw)
        rhs = jnp.concatenate(
            [vg_ref[0, c * MLP_CHUNK:(c + 1) * MLP_CHUNK, cols] for c in range(n_sub)], axis=1)
        m = jnp.dot(ws_ref[g], rhs, preferred_element_type=F32)
        for c in range(n_sub):
            mix_ref[c * MLP_CHUNK:(c + 1) * MLP_CHUNK, cols] = (
                m[:, c * gw:(c + 1) * gw] + bias_ref[:, cols])
    yb = (u_ref[0].astype(F32) * mix_ref[...] * zb_ref[0].astype(F32)).astype(BF16)
    ya = jnp.concatenate([ya_ref[0, h] for h in range(HEADS)], axis=1)
    pa = jnp.dot(ya, wpa_ref[...], preferred_element_type=F32)
    pb = jnp.dot(yb, wpb_ref[...], preferred_element_type=F32)
    merged = (ga_ref[0].astype(F32) * pa + gb_ref[0].astype(F32) * pb).astype(BF16)
    out = jnp.dot(merged, wo_ref[...], preferred_element_type=F32)
    xo = x_ref[0] + gt_ref[0] * out
    ms = jnp.mean(xo * xo, axis=-1, keepdims=True)
    o_ref[0] = xo * lax.rsqrt(ms + EPS) * fg_ref[...]


def _output(x, ya, u, vg, zb, ga, gb, mod_x, ws_b, bias_full, wpa_b, wpb_b, wo_b, fg):
    b, n, d = x.shape
    tm = OUT_ROWS
    flat = pl.BlockSpec((1, tm, d), lambda i, j: (i, j, 0))
    const2 = lambda shape: pl.BlockSpec(shape, lambda i, j: (0,) * len(shape))
    return pl.pallas_call(
        _out_kernel,
        grid=(b, n // tm),
        in_specs=[
            flat,
            pl.BlockSpec((1, HEADS, tm, HEAD_DIM), lambda i, j: (i, 0, j, 0)),
            flat, flat, flat, flat, flat,
            pl.BlockSpec((1, 1, d), lambda i, j: (i, 0, 2)),
            const2(ws_b.shape), const2(bias_full.shape),
            const2(wpa_b.shape), const2(wpb_b.shape), const2(wo_b.shape),
            const2((1, d)),
        ],
        out_specs=flat,
        out_shape=jax.ShapeDtypeStruct((b, n, d), F32),
        scratch_shapes=[pltpu.VMEM((tm, d), F32)],
        compiler_params=pltpu.CompilerParams(
            dimension_semantics=("arbitrary", "arbitrary"), vmem_limit_bytes=VMEM_LIMIT),
        name="out",
    )(x, ya, u, vg, zb, ga, gb, mod_x, ws_b, bias_full, wpa_b, wpb_b, wo_b, fg)


def kernel(x, c, ctx, c_ctx, w_mod, b_mod, norm_g, w_in, ret_decay_fwd, ret_decay_bwd, ret_norm_g,
           mlp_ln_g, mlp_ln_b, mlp_ws, mlp_bs, w_proj_a, w_proj_b, w_out, final_norm_g):
    b, n, d = x.shape
    assert d == D_MODEL and n % RET_CHUNK == 0 and n % OUT_ROWS == 0 and n % PROJ_ROWS == 0
    assert w_mod.shape[0] == 1, "single-layer block"

    cond = jnp.concatenate([c, c_ctx[None, :], jnp.zeros((16 - b - 1, d), F32)], axis=0)
    mod = _modulation(cond, w_mod[0], b_mod[0][None, :])
    mod_x = mod[:b].reshape(b, 1, 3 * d)
    mod_c = mod[b:b + 1]

    q_scale = jnp.concatenate([jnp.full((d,), HEAD_DIM ** -0.5, F32),
                               jnp.ones((w_in.shape[2] - d,), F32)])
    w_in_b = (w_in[0] * q_scale[None, :]).astype(BF16)
    ng = norm_g[0][None, :]
    lg = jnp.stack([jax.nn.log_sigmoid(ret_decay_fwd[0].astype(F32)),
                    jax.nn.log_sigmoid(ret_decay_bwd[0].astype(F32))])
    cos, sin = _rope_tables(n)

    kc, vc = _ctx_proj(ctx, mod_c, ng, w_in_b)
    q, k, v, za, u, vg, zb, ga, gb = _proj(
        x, mod_x, ng, w_in_b, jnp.asarray(cos), jnp.asarray(sin),
        mlp_ln_g[0][None, :], mlp_ln_b[0][None, :])
    ya = _retention(lg, q, k, v, za, kc, vc, ret_norm_g[0][None, :])

    bias_full = jnp.repeat(mlp_bs[0].T, d // MLP_GROUPS, axis=1)
    return _output(x, ya, u, vg, zb, ga, gb, mod_x, mlp_ws[0].astype(BF16), bias_full,
                   w_proj_a[0].astype(BF16), w_proj_b[0].astype(BF16), w_out[0].astype(BF16),
                   final_norm_g[None, :])
```

```python
import functools

import numpy as np
import jax
import jax.numpy as jnp
from jax import lax
from jax.experimental import pallas as pl
from jax.experimental.pallas import tpu as pltpu

F32 = jnp.float32
BF16 = jnp.bfloat16

D_MODEL = 1024
HEADS = 4
HEAD_DIM = 256
HALF = HEAD_DIM // 2
GRID_W = 64
MLP_GROUPS = 8
MLP_CHUNK = 128
ROPE_BASE = 10000.0
EPS = 1e-6
N_COL_GROUPS = 9

RET_CHUNK = 256
PROJ_ROWS = 256
OUT_ROWS = 512
VMEM_LIMIT = 52 * 1024 * 1024


def _rope_tables(n):
    pos = np.arange(n)
    row = (pos // GRID_W).astype(np.float64)
    col = (pos % GRID_W).astype(np.float64)
    quarter = HALF // 2
    inv = np.power(ROPE_BASE, -np.arange(quarter, dtype=np.float64) / quarter)
    ang = np.concatenate([row[:, None] * inv, col[:, None] * inv], axis=-1)
    return np.cos(ang).astype(np.float32), np.sin(ang).astype(np.float32)


def _silu(x):
    return x * jax.nn.sigmoid(x)


def _mod_kernel(c_ref, w_ref, b_ref, o_ref):
    s = _silu(c_ref[...])
    o_ref[...] = jnp.dot(s, w_ref[...], preferred_element_type=F32,
                         precision=lax.Precision.HIGHEST) + b_ref[...]


def _modulation(cond, w_mod, b_mod):
    rows, d = cond.shape
    width = w_mod.shape[1]
    bn = 512
    return pl.pallas_call(
        _mod_kernel,
        grid=(width // bn,),
        in_specs=[
            pl.BlockSpec((rows, d), lambda j: (0, 0)),
            pl.BlockSpec((d, bn), lambda j: (0, j)),
            pl.BlockSpec((1, bn), lambda j: (0, j)),
        ],
        out_specs=pl.BlockSpec((rows, bn), lambda j: (0, j)),
        out_shape=jax.ShapeDtypeStruct((rows, width), F32),
        name="mod",
    )(cond, w_mod, b_mod)


def _modulated_norm(x, ng, sh, sc):
    ms = jnp.mean(x * x, axis=-1, keepdims=True)
    h = x * lax.rsqrt(ms + EPS) * ng
    return h * (1.0 + sc) + sh


def _ctx_proj_kernel(x_ref, sh_ref, sc_ref, ng_ref, wk_ref, wv_ref, k_ref, v_ref):
    hb = _modulated_norm(x_ref[0], ng_ref[...], sh_ref[...], sc_ref[...]).astype(BF16)
    pk = jnp.dot(hb, wk_ref[...], preferred_element_type=F32)
    pv = jnp.dot(hb, wv_ref[...], preferred_element_type=F32)
    for h in range(HEADS):
        k_ref[0, h] = pk[:, h * HEAD_DIM:(h + 1) * HEAD_DIM].astype(BF16)
        v_ref[0, h] = pv[:, h * HEAD_DIM:(h + 1) * HEAD_DIM].astype(BF16)


def _ctx_proj(ctx, mod_c, ng, w_in_b):
    b, n, d = ctx.shape
    head_out = jax.ShapeDtypeStruct((b, HEADS, n, HEAD_DIM), BF16)
    head_spec = pl.BlockSpec((1, HEADS, n, HEAD_DIM), lambda i: (i, 0, 0, 0))
    return pl.pallas_call(
        _ctx_proj_kernel,
        grid=(b,),
        in_specs=[
            pl.BlockSpec((1, n, d), lambda i: (i, 0, 0)),
            pl.BlockSpec((1, d), lambda i: (0, 0)),
            pl.BlockSpec((1, d), lambda i: (0, 1)),
            pl.BlockSpec((1, d), lambda i: (0, 0)),
            pl.BlockSpec((d, d), lambda i: (0, 1)),
            pl.BlockSpec((d, d), lambda i: (0, 2)),
        ],
        out_specs=[head_spec, head_spec],
        out_shape=[head_out, head_out],
        name="ctx_proj",
    )(ctx, mod_c, mod_c, ng, w_in_b, w_in_b)


def _proj_kernel(x0_ref, xn_ref, sh0_ref, sc0_ref, shn_ref, scn_ref, ng_ref, w_ref, cos_ref, sin_ref,
                 lng_ref, lnb_ref, rg_ref,
                 q_ref, k_ref, v_ref, za_ref, u_ref, vg_ref, zb_ref, ga_ref, gb_ref, hb_ref):
    @pl.when(pl.program_id(0) == 0)
    def _():
        hb_ref[...] = _modulated_norm(x0_ref[...], ng_ref[...], sh0_ref[0], sc0_ref[0]).astype(BF16)

    cos = cos_ref[...]
    sin = sin_ref[...]

    def group(g):
        return jnp.dot(hb_ref[...], w_ref[:, g * D_MODEL:(g + 1) * D_MODEL],
                       preferred_element_type=F32)

    def store_rope(p, o_ref):
        for h in range(HEADS):
            t1 = p[:, h * HEAD_DIM:h * HEAD_DIM + HALF]
            t2 = p[:, h * HEAD_DIM + HALF:(h + 1) * HEAD_DIM]
            o_ref[0, h, :, :HALF] = (t1 * cos - t2 * sin).astype(BF16)
            o_ref[0, h, :, HALF:] = (t1 * sin + t2 * cos).astype(BF16)

    def store_heads(p, o_ref):
        for h in range(HEADS):
            o_ref[0, h] = p[:, h * HEAD_DIM:(h + 1) * HEAD_DIM].astype(BF16)

    store_rope(group(0), q_ref)
    store_rope(group(1), k_ref)
    store_heads(_silu(group(3)) * rg_ref[...], za_ref)
    u_ref[0] = jax.nn.gelu(group(4)).astype(BF16)
    t = jax.nn.gelu(group(5))
    mu = jnp.mean(t, axis=-1, keepdims=True)
    tc = t - mu
    tn = tc * lax.rsqrt(jnp.mean(tc * tc, axis=-1, keepdims=True) + EPS)
    vg_ref[0] = (tn * lng_ref[...] + lnb_ref[...]).astype(BF16)
    zb_ref[0] = _silu(group(6)).astype(BF16)
    ga_ref[0] = jax.nn.sigmoid(group(7)).astype(BF16)
    gb_ref[0] = jax.nn.sigmoid(group(8)).astype(BF16)
    store_heads(group(2), v_ref)

    hb_ref[...] = _modulated_norm(xn_ref[...], ng_ref[...], shn_ref[0], scn_ref[0]).astype(BF16)


def _proj(x, mod_x, ng, w_in_b, cos, sin, ln_g, ln_b, ret_g):
    b, n, d = x.shape
    tm = PROJ_ROWS
    per_b = n // tm
    steps = b * per_b
    x2 = x.reshape(b * n, d)
    nxt = lambda s: jnp.minimum(s + 1, steps - 1)
    head_out = jax.ShapeDtypeStruct((b, HEADS, n, HEAD_DIM), BF16)
    flat_out = jax.ShapeDtypeStruct((b, n, d), BF16)
    head_spec = pl.BlockSpec((1, HEADS, tm, HEAD_DIM), lambda s: (s // per_b, 0, s % per_b, 0))
    flat_spec = pl.BlockSpec((1, tm, d), lambda s: (s // per_b, s % per_b, 0))
    vec = pl.BlockSpec((1, d), lambda s: (0, 0))
    return pl.pallas_call(
        _proj_kernel,
        grid=(steps,),
        in_specs=[
            pl.BlockSpec((tm, d), lambda s: (0, 0)),
            pl.BlockSpec((tm, d), lambda s: (nxt(s), 0)),
            pl.BlockSpec((1, 1, d), lambda s: (0, 0, 0)),
            pl.BlockSpec((1, 1, d), lambda s: (0, 0, 1)),
            pl.BlockSpec((1, 1, d), lambda s: (nxt(s) // per_b, 0, 0)),
            pl.BlockSpec((1, 1, d), lambda s: (nxt(s) // per_b, 0, 1)),
            vec,
            pl.BlockSpec(w_in_b.shape, lambda s: (0, 0), pipeline_mode=pl.Buffered(1)),
            pl.BlockSpec((tm, HALF), lambda s: (s % per_b, 0)),
            pl.BlockSpec((tm, HALF), lambda s: (s % per_b, 0)),
            vec, vec, vec,
        ],
        out_specs=[head_spec] * 4 + [flat_spec] * 5,
        out_shape=[head_out] * 4 + [flat_out] * 5,
        scratch_shapes=[pltpu.VMEM((tm, d), BF16)],
        compiler_params=pltpu.CompilerParams(
            dimension_semantics=("arbitrary",), vmem_limit_bytes=VMEM_LIMIT),
        name="proj",
    )(x2, x2, mod_x, mod_x, mod_x, mod_x, ng, w_in_b, cos, sin, ln_g, ln_b, ret_g)


def _dot_tn(a, b):
    return lax.dot_general(a, b, (((0,), (0,)), ((), ())), preferred_element_type=F32)


def _dot_nt(a, b):
    return lax.dot_general(a, b, (((1,), (1,)), ((), ())), preferred_element_type=F32)


def _ret_kernel(lg_ref, q_ref, k_ref, v_ref, za_ref, kc_ref, vc_ref, o_ref,
                sf_ref, tf_ref, tb_ref, dmat_ref, kdf_ref, kdb_ref, qdf_ref, qdb_ref, cwf_ref, cwb_ref,
                *, n_chunks, n_ctx):
    C = RET_CHUNK
    h = pl.program_id(0)
    lgf = lg_ref[0, h]
    lgb = lg_ref[1, h]

    @pl.when(pl.program_id(1) == 0)
    def _():
        ii = lax.broadcasted_iota(jnp.int32, (C, C), 0)
        jj = lax.broadcasted_iota(jnp.int32, (C, C), 1)
        diff = (ii - jj).astype(F32)
        dmat_ref[...] = jnp.where(diff >= 0.0,
                                  jnp.exp(lgf * jnp.maximum(diff, 0.0)),
                                  jnp.exp(lgb * jnp.maximum(-diff, 0.0))).astype(BF16)
        row = lax.broadcasted_iota(jnp.int32, (C, HEAD_DIM), 0).astype(F32)
        kdf_ref[...] = jnp.exp(lgf * (C - 1.0 - row)).astype(BF16)
        kdb_ref[...] = jnp.exp(lgb * row).astype(BF16)
        qdf_ref[...] = jnp.exp(lgf * (row + 1.0)).astype(BF16)
        qdb_ref[...] = jnp.exp(lgb * (C - row)).astype(BF16)
        crow = lax.broadcasted_iota(jnp.int32, (n_ctx, HEAD_DIM), 0).astype(F32)
        cwf_ref[...] = jnp.exp(lgf * (n_ctx - 1.0 - crow)).astype(BF16)
        cwb_ref[...] = jnp.exp(lgb * crow).astype(BF16)

    one = jnp.ones((1, 1), F32)
    cdec_f = jnp.exp(one * (lgf * C))
    cdec_b = jnp.exp(one * (lgb * C))

    kc = kc_ref[0, 0]
    vc = vc_ref[0, 0]
    sf_ref[...] = _dot_tn(kc * cwf_ref[...], vc)
    tf_ref[...] = _dot_tn(kc * cwb_ref[...], vc)

    for c in range(n_chunks - 1, -1, -1):
        tb_ref[c] = tf_ref[...].astype(BF16)
        if c > 0:
            sl = slice(c * C, (c + 1) * C)
            tf_ref[...] = cdec_b * tf_ref[...] + _dot_tn(k_ref[0, 0, sl, :] * kdb_ref[...],
                                                         v_ref[0, 0, sl, :])

    for c in range(n_chunks):
        sl = slice(c * C, (c + 1) * C)
        qq = q_ref[0, 0, sl, :]
        kk = k_ref[0, 0, sl, :]
        vv = v_ref[0, 0, sl, :]
        p = _dot_nt(qq, kk).astype(BF16) * dmat_ref[...]
        lhs = jnp.concatenate([p, qq * qdf_ref[...], qq * qdb_ref[...]], axis=1)
        rhs = jnp.concatenate([vv, sf_ref[...].astype(BF16), tb_ref[c]], axis=0)
        o = jnp.dot(lhs, rhs, preferred_element_type=F32)
        if c + 1 < n_chunks:
            sf_ref[...] = cdec_f * sf_ref[...] + _dot_tn(kk * kdf_ref[...], vv)
        mu = jnp.mean(o, axis=-1, keepdims=True)
        oc = o - mu
        rstd = lax.rsqrt(jnp.mean(oc * oc, axis=-1, keepdims=True) + EPS)
        o_ref[0, 0, sl, :] = (oc * rstd * za_ref[0, 0, sl, :].astype(F32)).astype(BF16)


def _retention(lg, q, k, v, za, kc, vc):
    b, hh, n, dh = q.shape
    n_ctx = kc.shape[2]
    n_chunks = n // RET_CHUNK
    seq = pl.BlockSpec((1, 1, n, dh), lambda j, i: (i, j, 0, 0))
    cseq = pl.BlockSpec((1, 1, n_ctx, dh), lambda j, i: (i, j, 0, 0))
    table = pltpu.VMEM((RET_CHUNK, dh), BF16)
    return pl.pallas_call(
        functools.partial(_ret_kernel, n_chunks=n_chunks, n_ctx=n_ctx),
        grid=(hh, b),
        in_specs=[
            pl.BlockSpec(memory_space=pltpu.SMEM),
            seq, seq, seq, seq, cseq, cseq,
        ],
        out_specs=seq,
        out_shape=jax.ShapeDtypeStruct((b, hh, n, dh), BF16),
        scratch_shapes=[
            pltpu.VMEM((dh, dh), F32),
            pltpu.VMEM((dh, dh), F32),
            pltpu.VMEM((n_chunks, dh, dh), BF16),
            pltpu.VMEM((RET_CHUNK, RET_CHUNK), BF16),
            table, table, table, table,
            pltpu.VMEM((n_ctx, dh), BF16), pltpu.VMEM((n_ctx, dh), BF16),
        ],
        compiler_params=pltpu.CompilerParams(
            dimension_semantics=("arbitrary", "arbitrary"), vmem_limit_bytes=VMEM_LIMIT),
        name="ret",
    )(lg, q, k, v, za, kc, vc)


def _out_kernel(x_ref, ya_ref, u_ref, vg_ref, zb_ref, ga_ref, gb_ref, gt_ref, ws_ref, bias_ref,
                wpa_ref, wpb_ref, wo_ref, fg_ref, o_ref, mix_ref):
    tm = x_ref.shape[1]
    n_sub = tm // MLP_CHUNK
    gw = D_MODEL // MLP_GROUPS
    for g in range(MLP_GROUPS):
        cols = slice(g * gw, (g + 1) * gw)
        rhs = jnp.concatenate(
            [vg_ref[0, c * MLP_CHUNK:(c + 1) * MLP_CHUNK, cols] for c in range(n_sub)], axis=1)
        m = jnp.dot(ws_ref[g], rhs, preferred_element_type=F32)
        for c in range(n_sub):
            mix_ref[c * MLP_CHUNK:(c + 1) * MLP_CHUNK, cols] = (
                m[:, c * gw:(c + 1) * gw] + bias_ref[:, cols])
    yb = (u_ref[0].astype(F32) * mix_ref[...] * zb_ref[0].astype(F32)).astype(BF16)
    ya = jnp.concatenate([ya_ref[0, h] for h in range(HEADS)], axis=1)
    pa = jnp.dot(ya, wpa_ref[...], preferred_element_type=F32)
    pb = jnp.dot(yb, wpb_ref[...], preferred_element_type=F32)
    merged = (ga_ref[0].astype(F32) * pa + gb_ref[0].astype(F32) * pb).astype(BF16)
    out = jnp.dot(merged, wo_ref[...], preferred_element_type=F32)
    xo = x_ref[0] + gt_ref[0] * out
    ms = jnp.mean(xo * xo, axis=-1, keepdims=True)
    o_ref[0] = xo * lax.rsqrt(ms + EPS) * fg_ref[...]


def _output(x, ya, u, vg, zb, ga, gb, mod_x, ws_b, bias_full, wpa_b, wpb_b, wo_b, fg):
    b, n, d = x.shape
    tm = OUT_ROWS
    flat = pl.BlockSpec((1, tm, d), lambda i, j: (i, j, 0))
    const2 = lambda shape: pl.BlockSpec(shape, lambda i, j: (0,) * len(shape))
    return pl.pallas_call(
        _out_kernel,
        grid=(b, n // tm),
        in_specs=[
            flat,
            pl.BlockSpec((1, HEADS, tm, HEAD_DIM), lambda i, j: (i, 0, j, 0)),
            flat, flat, flat, flat, flat,
            pl.BlockSpec((1, 1, d), lambda i, j: (i, 0, 2)),
            const2(ws_b.shape), const2(bias_full.shape),
            const2(wpa_b.shape), const2(wpb_b.shape), const2(wo_b.shape),
            const2((1, d)),
        ],
        out_specs=flat,
        out_shape=jax.ShapeDtypeStruct((b, n, d), F32),
        scratch_shapes=[pltpu.VMEM((tm, d), F32)],
        compiler_params=pltpu.CompilerParams(
            dimension_semantics=("arbitrary", "arbitrary"), vmem_limit_bytes=VMEM_LIMIT),
        name="out",
    )(x, ya, u, vg, zb, ga, gb, mod_x, ws_b, bias_full, wpa_b, wpb_b, wo_b, fg)


def kernel(x, c, ctx, c_ctx, w_mod, b_mod, norm_g, w_in, ret_decay_fwd, ret_decay_bwd, ret_norm_g,
           mlp_ln_g, mlp_ln_b, mlp_ws, mlp_bs, w_proj_a, w_proj_b, w_out, final_norm_g):
    b, n, d = x.shape
    assert d == D_MODEL and n % RET_CHUNK == 0 and n % OUT_ROWS == 0 and n % PROJ_ROWS == 0
    assert w_mod.shape[0] == 1, "single-layer block"

    cond = jnp.concatenate([c, c_ctx[None, :], jnp.zeros((16 - b - 1, d), F32)], axis=0)
    mod = _modulation(cond, w_mod[0], b_mod[0][None, :])
    mod_x = mod[:b].reshape(b, 1, 3 * d)
    mod_c = mod[b:b + 1]

    q_scale = jnp.concatenate([jnp.full((d,), HEAD_DIM ** -0.5, F32),
                               jnp.ones((w_in.shape[2] - d,), F32)])
    w_in_b = (w_in[0] * q_scale[None, :]).astype(BF16)
    ng = norm_g[0][None, :]
    lg = jnp.stack([jax.nn.log_sigmoid(ret_decay_fwd[0].astype(F32)),
                    jax.nn.log_sigmoid(ret_decay_bwd[0].astype(F32))])
    cos, sin = _rope_tables(n)

    kc, vc = _ctx_proj(ctx, mod_c, ng, w_in_b)
    q, k, v, za, u, vg, zb, ga, gb = _proj(
        x, mod_x, ng, w_in_b, jnp.asarray(cos), jnp.asarray(sin),
        mlp_ln_g[0][None, :], mlp_ln_b[0][None, :], ret_norm_g[0][None, :])
    ya = _retention(lg, q, k, v, za, kc, vc)

    bias_full = jnp.repeat(mlp_bs[0].T, d // MLP_GROUPS, axis=1)
    return _output(x, ya, u, vg, zb, ga, gb, mod_x, mlp_ws[0].astype(BF16), bias_full,
                   w_proj_a[0].astype(BF16), w_proj_b[0].astype(BF16), w_out[0].astype(BF16),
                   final_norm_g[None, :])
```

```python
import functools

import numpy as np
import jax
import jax.numpy as jnp
from jax import lax
from jax.experimental import pallas as pl
from jax.experimental.pallas import tpu as pltpu

F32 = jnp.float32
BF16 = jnp.bfloat16

D_MODEL = 1024
HEADS = 4
HEAD_DIM = 256
HALF = HEAD_DIM // 2
GRID_W = 64
MLP_GROUPS = 8
MLP_CHUNK = 128
ROPE_BASE = 10000.0
EPS = 1e-6
N_COL_GROUPS = 9

RET_CHUNK = 256
PROJ_ROWS = 256
OUT_ROWS = 512
VMEM_LIMIT = 52 * 1024 * 1024


def _rope_tables(n):
    pos = np.arange(n)
    row = (pos // GRID_W).astype(np.float64)
    col = (pos % GRID_W).astype(np.float64)
    quarter = HALF // 2
    inv = np.power(ROPE_BASE, -np.arange(quarter, dtype=np.float64) / quarter)
    ang = np.concatenate([row[:, None] * inv, col[:, None] * inv], axis=-1)
    return np.cos(ang).astype(np.float32), np.sin(ang).astype(np.float32)


def _silu(x):
    return x * jax.nn.sigmoid(x)


def _pack_rows(w):
    k, n = w.shape
    wb = w.astype(BF16).reshape(k // 2, 2, n)
    return lax.bitcast_convert_type(jnp.swapaxes(wb, 1, 2), jnp.uint32)


def _unpack_rows(w32):
    return pltpu.bitcast(w32, BF16)


def _mod_kernel(c_ref, w_ref, b_ref, o_ref):
    s = _silu(c_ref[...])
    o_ref[...] = jnp.dot(s, w_ref[...], preferred_element_type=F32,
                         precision=lax.Precision.HIGHEST) + b_ref[...]


def _modulation(cond, w_mod, b_mod):
    rows, d = cond.shape
    width = w_mod.shape[1]
    bn = 512
    return pl.pallas_call(
        _mod_kernel,
        grid=(width // bn,),
        in_specs=[
            pl.BlockSpec((rows, d), lambda j: (0, 0)),
            pl.BlockSpec((d, bn), lambda j: (0, j)),
            pl.BlockSpec((1, bn), lambda j: (0, j)),
        ],
        out_specs=pl.BlockSpec((rows, bn), lambda j: (0, j)),
        out_shape=jax.ShapeDtypeStruct((rows, width), F32),
        name="mod",
    )(cond, w_mod, b_mod)


def _modulated_norm(x, ng, sh, sc):
    ms = jnp.mean(x * x, axis=-1, keepdims=True)
    return x * lax.rsqrt(ms + EPS) * (ng * (1.0 + sc)) + sh


def _ctx_proj_kernel(x_ref, sh_ref, sc_ref, ng_ref, wk_ref, wv_ref, k_ref, v_ref):
    hb = _modulated_norm(x_ref[0], ng_ref[...], sh_ref[...], sc_ref[...]).astype(BF16)
    pk = jnp.dot(hb, _unpack_rows(wk_ref[...]), preferred_element_type=F32)
    pv = jnp.dot(hb, _unpack_rows(wv_ref[...]), preferred_element_type=F32)
    for h in range(HEADS):
        k_ref[0, h] = pk[:, h * HEAD_DIM:(h + 1) * HEAD_DIM].astype(BF16)
        v_ref[0, h] = pv[:, h * HEAD_DIM:(h + 1) * HEAD_DIM].astype(BF16)


def _ctx_proj(ctx, mod_c, ng, w_in_b):
    b, n, d = ctx.shape
    head_out = jax.ShapeDtypeStruct((b, HEADS, n, HEAD_DIM), BF16)
    head_spec = pl.BlockSpec((1, HEADS, n, HEAD_DIM), lambda i: (i, 0, 0, 0))
    return pl.pallas_call(
        _ctx_proj_kernel,
        grid=(b,),
        in_specs=[
            pl.BlockSpec((1, n, d), lambda i: (i, 0, 0)),
            pl.BlockSpec((1, d), lambda i: (0, 0)),
            pl.BlockSpec((1, d), lambda i: (0, 1)),
            pl.BlockSpec((1, d), lambda i: (0, 0)),
            pl.BlockSpec((d // 2, d), lambda i: (0, 1)),
            pl.BlockSpec((d // 2, d), lambda i: (0, 2)),
        ],
        out_specs=[head_spec, head_spec],
        out_shape=[head_out, head_out],
        name="ctx_proj",
    )(ctx, mod_c, mod_c, ng, w_in_b, w_in_b)


def _proj_kernel(x0_ref, xn_ref, sh0_ref, sc0_ref, shn_ref, scn_ref, ng_ref, w_ref, cos_ref, sin_ref,
                 lng_ref, lnb_ref, rg_ref,
                 q_ref, k_ref, v_ref, za_ref, u_ref, vg_ref, zb_ref, ga_ref, gb_ref, hb_ref):
    @pl.when(pl.program_id(0) == 0)
    def _():
        hb_ref[...] = _modulated_norm(x0_ref[...], ng_ref[...], sh0_ref[0], sc0_ref[0]).astype(BF16)

    cos = cos_ref[...]
    sin = sin_ref[...]

    def group(g):
        return jnp.dot(hb_ref[...], _unpack_rows(w_ref[:, g * D_MODEL:(g + 1) * D_MODEL]),
                       preferred_element_type=F32)

    def store_rope(p, o_ref):
        pb = p.astype(BF16)
        for h in range(HEADS):
            t1 = pb[:, h * HEAD_DIM:h * HEAD_DIM + HALF]
            t2 = pb[:, h * HEAD_DIM + HALF:(h + 1) * HEAD_DIM]
            o_ref[0, h, :, :HALF] = t1 * cos - t2 * sin
            o_ref[0, h, :, HALF:] = t1 * sin + t2 * cos

    def store_heads(pb, o_ref):
        for h in range(HEADS):
            o_ref[0, h] = pb[:, h * HEAD_DIM:(h + 1) * HEAD_DIM]

    store_rope(group(0), q_ref)
    hb_next = _modulated_norm(xn_ref[...], ng_ref[...], shn_ref[0], scn_ref[0]).astype(BF16)
    store_heads(_silu(group(3)).astype(BF16) * rg_ref[...].astype(BF16), za_ref)
    u_ref[0] = jax.nn.gelu(group(4)).astype(BF16)
    ga_ref[0] = jax.nn.sigmoid(group(7)).astype(BF16)
    t = jax.nn.gelu(group(5))
    mu = jnp.mean(t, axis=-1, keepdims=True)
    tc = t - mu
    tn = tc * lax.rsqrt(jnp.mean(tc * tc, axis=-1, keepdims=True) + EPS)
    vg_ref[0] = tn.astype(BF16) * lng_ref[...].astype(BF16) + lnb_ref[...].astype(BF16)
    gb_ref[0] = jax.nn.sigmoid(group(8)).astype(BF16)
    store_rope(group(1), k_ref)
    zb_ref[0] = _silu(group(6)).astype(BF16)
    store_heads(group(2).astype(BF16), v_ref)

    hb_ref[...] = hb_next


def _proj(x, mod_x, ng, w_in_b, cos, sin, ln_g, ln_b, ret_g):
    b, n, d = x.shape
    tm = PROJ_ROWS
    per_b = n // tm
    steps = b * per_b
    x2 = x.reshape(b * n, d)
    nxt = lambda s: jnp.minimum(s + 1, steps - 1)
    head_out = jax.ShapeDtypeStruct((b, HEADS, n, HEAD_DIM), BF16)
    flat_out = jax.ShapeDtypeStruct((b, n, d), BF16)
    head_spec = pl.BlockSpec((1, HEADS, tm, HEAD_DIM), lambda s: (s // per_b, 0, s % per_b, 0))
    flat_spec = pl.BlockSpec((1, tm, d), lambda s: (s // per_b, s % per_b, 0))
    vec = pl.BlockSpec((1, d), lambda s: (0, 0))
    return pl.pallas_call(
        _proj_kernel,
        grid=(steps,),
        in_specs=[
            pl.BlockSpec((tm, d), lambda s: (0, 0)),
            pl.BlockSpec((tm, d), lambda s: (nxt(s), 0)),
            pl.BlockSpec((1, 1, d), lambda s: (0, 0, 0)),
            pl.BlockSpec((1, 1, d), lambda s: (0, 0, 1)),
            pl.BlockSpec((1, 1, d), lambda s: (nxt(s) // per_b, 0, 0)),
            pl.BlockSpec((1, 1, d), lambda s: (nxt(s) // per_b, 0, 1)),
            vec,
            pl.BlockSpec(w_in_b.shape, lambda s: (0, 0), pipeline_mode=pl.Buffered(1)),
            pl.BlockSpec((tm, HALF), lambda s: (s % per_b, 0)),
            pl.BlockSpec((tm, HALF), lambda s: (s % per_b, 0)),
            vec, vec, vec,
        ],
        out_specs=[head_spec] * 4 + [flat_spec] * 5,
        out_shape=[head_out] * 4 + [flat_out] * 5,
        scratch_shapes=[pltpu.VMEM((tm, d), BF16)],
        compiler_params=pltpu.CompilerParams(
            dimension_semantics=("arbitrary",), vmem_limit_bytes=VMEM_LIMIT),
        name="proj",
    )(x2, x2, mod_x, mod_x, mod_x, mod_x, ng, w_in_b, cos, sin, ln_g, ln_b, ret_g)


def _dot_tn(a, b):
    return lax.dot_general(a, b, (((0,), (0,)), ((), ())), preferred_element_type=F32)


def _dot_nt(a, b):
    return lax.dot_general(a, b, (((1,), (1,)), ((), ())), preferred_element_type=F32)


def _ret_kernel(lg_ref, q_ref, k_ref, v_ref, za_ref, kc_ref, vc_ref, o_ref,
                sf_ref, tf_ref, tb_ref, dmat_ref, kdf_ref, kdb_ref, qdf_ref, qdb_ref, cwf_ref, cwb_ref,
                *, n_chunks, n_ctx):
    C = RET_CHUNK
    h = pl.program_id(0)
    lgf = lg_ref[0, h]
    lgb = lg_ref[1, h]

    @pl.when(pl.program_id(1) == 0)
    def _():
        ii = lax.broadcasted_iota(jnp.int32, (C, C), 0)
        jj = lax.broadcasted_iota(jnp.int32, (C, C), 1)
        diff = (ii - jj).astype(F32)
        dmat_ref[...] = jnp.where(diff >= 0.0,
                                  jnp.exp(lgf * jnp.maximum(diff, 0.0)),
                                  jnp.exp(lgb * jnp.maximum(-diff, 0.0))).astype(BF16)
        row = lax.broadcasted_iota(jnp.int32, (C, HEAD_DIM), 0).astype(F32)
        kdf_ref[...] = jnp.exp(lgf * (C - 1.0 - row)).astype(BF16)
        kdb_ref[...] = jnp.exp(lgb * row).astype(BF16)
        qdf_ref[...] = jnp.exp(lgf * (row + 1.0)).astype(BF16)
        qdb_ref[...] = jnp.exp(lgb * (C - row)).astype(BF16)
        crow = lax.broadcasted_iota(jnp.int32, (n_ctx, HEAD_DIM), 0).astype(F32)
        cwf_ref[...] = jnp.exp(lgf * (n_ctx - 1.0 - crow)).astype(BF16)
        cwb_ref[...] = jnp.exp(lgb * crow).astype(BF16)

    one = jnp.ones((1, 1), F32)
    cdec_f = jnp.exp(one * (lgf * C))
    cdec_b = jnp.exp(one * (lgb * C))

    kc = kc_ref[0, 0]
    vc = vc_ref[0, 0]
    sf_ref[...] = _dot_tn(kc * cwf_ref[...], vc)
    tf_ref[...] = _dot_tn(kc * cwb_ref[...], vc)

    for c in range(n_chunks - 1, -1, -1):
        tb_ref[c] = tf_ref[...].astype(BF16)
        if c > 0:
            sl = slice(c * C, (c + 1) * C)
            tf_ref[...] = cdec_b * tf_ref[...] + _dot_tn(k_ref[0, 0, sl, :] * kdb_ref[...],
                                                         v_ref[0, 0, sl, :])

    for c in range(n_chunks):
        sl = slice(c * C, (c + 1) * C)
        qq = q_ref[0, 0, sl, :]
        kk = k_ref[0, 0, sl, :]
        vv = v_ref[0, 0, sl, :]
        p = _dot_nt(qq, kk).astype(BF16) * dmat_ref[...]
        lhs = jnp.concatenate([p, qq * qdf_ref[...], qq * qdb_ref[...]], axis=1)
        rhs = jnp.concatenate([vv, sf_ref[...].astype(BF16), tb_ref[c]], axis=0)
        o = jnp.dot(lhs, rhs, preferred_element_type=F32)
        if c + 1 < n_chunks:
            sf_ref[...] = cdec_f * sf_ref[...] + _dot_tn(kk * kdf_ref[...], vv)
        mu = jnp.mean(o, axis=-1, keepdims=True)
        oc = o - mu
        rstd = lax.rsqrt(jnp.mean(oc * oc, axis=-1, keepdims=True) + EPS)
        o_ref[0, 0, sl, :] = (oc * rstd * za_ref[0, 0, sl, :].astype(F32)).astype(BF16)


def _retention(lg, q, k, v, za, kc, vc):
    b, hh, n, dh = q.shape
    n_ctx = kc.shape[2]
    n_chunks = n // RET_CHUNK
    seq = pl.BlockSpec((1, 1, n, dh), lambda j, i: (i, j, 0, 0))
    cseq = pl.BlockSpec((1, 1, n_ctx, dh), lambda j, i: (i, j, 0, 0))
    table = pltpu.VMEM((RET_CHUNK, dh), BF16)
    return pl.pallas_call(
        functools.partial(_ret_kernel, n_chunks=n_chunks, n_ctx=n_ctx),
        grid=(hh, b),
        in_specs=[
            pl.BlockSpec(memory_space=pltpu.SMEM),
            seq, seq, seq, seq, cseq, cseq,
        ],
        out_specs=seq,
        out_shape=jax.ShapeDtypeStruct((b, hh, n, dh), BF16),
        scratch_shapes=[
            pltpu.VMEM((dh, dh), F32),
            pltpu.VMEM((dh, dh), F32),
            pltpu.VMEM((n_chunks, dh, dh), BF16),
            pltpu.VMEM((RET_CHUNK, RET_CHUNK), BF16),
            table, table, table, table,
            pltpu.VMEM((n_ctx, dh), BF16), pltpu.VMEM((n_ctx, dh), BF16),
        ],
        compiler_params=pltpu.CompilerParams(
            dimension_semantics=("arbitrary", "arbitrary"), vmem_limit_bytes=VMEM_LIMIT),
        name="ret",
    )(lg, q, k, v, za, kc, vc)


def _out_kernel(x_ref, ya_ref, u_ref, vg_ref, zb_ref, ga_ref, gb_ref, gt_ref, ws_ref, bias_ref,
                wpa_ref, wpb_ref, wo_ref, fg_ref, o_ref, mix_ref):
    tm = x_ref.shape[1]
    n_sub = tm // MLP_CHUNK
    gw = D_MODEL // MLP_GROUPS
    for g in range(MLP_GROUPS):
        cols = slice(g * gw, (g + 1) * gw)
        rhs = jnp.concatenate(
            [vg_ref[0, c * MLP_CHUNK:(c + 1) * MLP_CHUNK, cols] for c in range(n_sub)], axis=1)
        m = jnp.dot(ws_ref[g], rhs, preferred_element_type=F32)
        for c in range(n_sub):
            mix_ref[c * MLP_CHUNK:(c + 1) * MLP_CHUNK, cols] = (
                m[:, c * gw:(c + 1) * gw] + bias_ref[:, cols])
    yb = (u_ref[0].astype(F32) * mix_ref[...] * zb_ref[0].astype(F32)).astype(BF16)
    ya = jnp.concatenate([ya_ref[0, h] for h in range(HEADS)], axis=1)
    pa = jnp.dot(ya, _unpack_rows(wpa_ref[...]), preferred_element_type=F32)
    pb = jnp.dot(yb, _unpack_rows(wpb_ref[...]), preferred_element_type=F32)
    merged = (ga_ref[0].astype(F32) * pa + gb_ref[0].astype(F32) * pb).astype(BF16)
    out = jnp.dot(merged, _unpack_rows(wo_ref[...]), preferred_element_type=F32)
    xo = x_ref[0] + gt_ref[0] * out
    ms = jnp.mean(xo * xo, axis=-1, keepdims=True)
    o_ref[0] = xo * lax.rsqrt(ms + EPS) * fg_ref[...]


def _output(x, ya, u, vg, zb, ga, gb, mod_x, ws_b, bias_full, wpa_b, wpb_b, wo_b, fg):
    b, n, d = x.shape
    tm = OUT_ROWS
    flat = pl.BlockSpec((1, tm, d), lambda i, j: (i, j, 0))
    const2 = lambda shape: pl.BlockSpec(shape, lambda i, j: (0,) * len(shape))
    return pl.pallas_call(
        _out_kernel,
        grid=(b, n // tm),
        in_specs=[
            flat,
            pl.BlockSpec((1, HEADS, tm, HEAD_DIM), lambda i, j: (i, 0, j, 0)),
            flat, flat, flat, flat, flat,
            pl.BlockSpec((1, 1, d), lambda i, j: (i, 0, 2)),
            const2(ws_b.shape), const2(bias_full.shape),
            const2(wpa_b.shape), const2(wpb_b.shape), const2(wo_b.shape),
            const2((1, d)),
        ],
        out_specs=flat,
        out_shape=jax.ShapeDtypeStruct((b, n, d), F32),
        scratch_shapes=[pltpu.VMEM((tm, d), F32)],
        compiler_params=pltpu.CompilerParams(
            dimension_semantics=("arbitrary", "arbitrary"), vmem_limit_bytes=VMEM_LIMIT),
        name="out",
    )(x, ya, u, vg, zb, ga, gb, mod_x, ws_b, bias_full, wpa_b, wpb_b, wo_b, fg)


def kernel(x, c, ctx, c_ctx, w_mod, b_mod, norm_g, w_in, ret_decay_fwd, ret_decay_bwd, ret_norm_g,
           mlp_ln_g, mlp_ln_b, mlp_ws, mlp_bs, w_proj_a, w_proj_b, w_out, final_norm_g):
    b, n, d = x.shape
    assert d == D_MODEL and n % RET_CHUNK == 0 and n % OUT_ROWS == 0 and n % PROJ_ROWS == 0
    assert w_mod.shape[0] == 1, "single-layer block"

    cond = jnp.concatenate([c, c_ctx[None, :], jnp.zeros((16 - b - 1, d), F32)], axis=0)
    mod = _modulation(cond, w_mod[0], b_mod[0][None, :])
    mod_x = mod[:b].reshape(b, 1, 3 * d)
    mod_c = mod[b:b + 1]

    q_scale = jnp.concatenate([jnp.full((d,), HEAD_DIM ** -0.5, F32),
                               jnp.ones((w_in.shape[2] - d,), F32)])
    w_in_b = _pack_rows(w_in[0] * q_scale[None, :])
    ng = norm_g[0][None, :]
    lg = jnp.stack([jax.nn.log_sigmoid(ret_decay_fwd[0].astype(F32)),
                    jax.nn.log_sigmoid(ret_decay_bwd[0].astype(F32))])
    cos, sin = _rope_tables(n)

    kc, vc = _ctx_proj(ctx, mod_c, ng, w_in_b)
    q, k, v, za, u, vg, zb, ga, gb = _proj(
        x, mod_x, ng, w_in_b, jnp.asarray(cos).astype(BF16), jnp.asarray(sin).astype(BF16),
        mlp_ln_g[0][None, :], mlp_ln_b[0][None, :], ret_norm_g[0][None, :])
    ya = _retention(lg, q, k, v, za, kc, vc)

    bias_full = jnp.repeat(mlp_bs[0].T, d // MLP_GROUPS, axis=1)
    return _output(x, ya, u, vg, zb, ga, gb, mod_x, mlp_ws[0].astype(BF16), bias_full,
                   _pack_rows(w_proj_a[0]), _pack_rows(w_proj_b[0]), _pack_rows(w_out[0]),
                   final_norm_g[None, :])
```

```python
import functools

import numpy as np
import jax
import jax.numpy as jnp
from jax import lax
from jax.experimental import pallas as pl
from jax.experimental.pallas import tpu as pltpu

F32 = jnp.float32
BF16 = jnp.bfloat16

D_MODEL = 1024
HEADS = 4
HEAD_DIM = 256
HALF = HEAD_DIM // 2
GRID_W = 64
MLP_GROUPS = 8
MLP_CHUNK = 128
ROPE_BASE = 10000.0
EPS = 1e-6
N_COL_GROUPS = 9

RET_CHUNK = 256
PROJ_ROWS = 256
PACK_COLS = 1024
OUT_ROWS = 512
VMEM_LIMIT = 52 * 1024 * 1024


def _rope_tables(n):
    pos = np.arange(n)
    row = (pos // GRID_W).astype(np.float64)
    col = (pos % GRID_W).astype(np.float64)
    quarter = HALF // 2
    inv = np.power(ROPE_BASE, -np.arange(quarter, dtype=np.float64) / quarter)
    ang = np.concatenate([row[:, None] * inv, col[:, None] * inv], axis=-1)
    return np.cos(ang).astype(np.float32), np.sin(ang).astype(np.float32)


def _silu(x):
    return x * jax.nn.sigmoid(x)


def _pack_rows_kernel(*refs):
    n_w = (len(refs) - 1) // 2
    s_ref = refs[n_w]
    for w_ref, o_ref in zip(refs[:n_w], refs[n_w + 1:]):
        o_ref[...] = pltpu.bitcast((w_ref[...] * s_ref[...]).astype(BF16), jnp.uint32)


def _pack_rows(ws, col_scale, bn):
    k, n = ws[0].shape
    w_spec = pl.BlockSpec((k, bn), lambda j: (0, j))
    o_spec = pl.BlockSpec((k // 2, bn), lambda j: (0, j))
    return pl.pallas_call(
        _pack_rows_kernel,
        grid=(n // bn,),
        in_specs=[w_spec] * len(ws) + [pl.BlockSpec((1, bn), lambda j: (0, j))],
        out_specs=[o_spec] * len(ws),
        out_shape=[jax.ShapeDtypeStruct((k // 2, n), jnp.uint32)] * len(ws),
        name="pack_rows",
    )(*ws, col_scale)


def _unpack_rows(w32):
    return pltpu.bitcast(w32, BF16)


def _mod_kernel(c_ref, w_ref, b_ref, o_ref):
    s = _silu(c_ref[...])
    o_ref[...] = jnp.dot(s, w_ref[...], preferred_element_type=F32,
                         precision=lax.Precision.HIGHEST) + b_ref[...]


def _modulation(cond, w_mod, b_mod):
    rows, d = cond.shape
    width = w_mod.shape[1]
    bn = 512
    return pl.pallas_call(
        _mod_kernel,
        grid=(width // bn,),
        in_specs=[
            pl.BlockSpec((rows, d), lambda j: (0, 0)),
            pl.BlockSpec((d, bn), lambda j: (0, j)),
            pl.BlockSpec((1, bn), lambda j: (0, j)),
        ],
        out_specs=pl.BlockSpec((rows, bn), lambda j: (0, j)),
        out_shape=jax.ShapeDtypeStruct((rows, width), F32),
        name="mod",
    )(cond, w_mod, b_mod)


def _modulated_norm(x, ng, sh, sc):
    ms = jnp.mean(x * x, axis=-1, keepdims=True)
    return x * lax.rsqrt(ms + EPS) * (ng * (1.0 + sc)) + sh


def _ctx_proj_kernel(x_ref, sh_ref, sc_ref, ng_ref, wk_ref, wv_ref, k_ref, v_ref):
    hb = _modulated_norm(x_ref[0], ng_ref[...], sh_ref[...], sc_ref[...]).astype(BF16)
    pk = jnp.dot(hb, _unpack_rows(wk_ref[...]), preferred_element_type=F32)
    pv = jnp.dot(hb, _unpack_rows(wv_ref[...]), preferred_element_type=F32)
    for h in range(HEADS):
        k_ref[0, h] = pk[:, h * HEAD_DIM:(h + 1) * HEAD_DIM].astype(BF16)
        v_ref[0, h] = pv[:, h * HEAD_DIM:(h + 1) * HEAD_DIM].astype(BF16)


def _ctx_proj(ctx, mod_c, ng, w_in_b):
    b, n, d = ctx.shape
    head_out = jax.ShapeDtypeStruct((b, HEADS, n, HEAD_DIM), BF16)
    head_spec = pl.BlockSpec((1, HEADS, n, HEAD_DIM), lambda i: (i, 0, 0, 0))
    return pl.pallas_call(
        _ctx_proj_kernel,
        grid=(b,),
        in_specs=[
            pl.BlockSpec((1, n, d), lambda i: (i, 0, 0)),
            pl.BlockSpec((1, d), lambda i: (0, 0)),
            pl.BlockSpec((1, d), lambda i: (0, 1)),
            pl.BlockSpec((1, d), lambda i: (0, 0)),
            pl.BlockSpec((d // 2, d), lambda i: (0, 1)),
            pl.BlockSpec((d // 2, d), lambda i: (0, 2)),
        ],
        out_specs=[head_spec, head_spec],
        out_shape=[head_out, head_out],
        name="ctx_proj",
    )(ctx, mod_c, mod_c, ng, w_in_b, w_in_b)


def _proj_kernel(x0_ref, xn_ref, sh0_ref, sc0_ref, shn_ref, scn_ref, ng_ref, w_ref, cos_ref, sin_ref,
                 lng_ref, lnb_ref, rg_ref,
                 q_ref, k_ref, v_ref, za_ref, u_ref, vg_ref, zb_ref, ga_ref, gb_ref, hb_ref):
    @pl.when(pl.program_id(0) == 0)
    def _():
        hb_ref[...] = _modulated_norm(x0_ref[...], ng_ref[...], sh0_ref[0], sc0_ref[0]).astype(BF16)

    cos = cos_ref[...]
    sin = sin_ref[...]

    def group(g):
        return jnp.dot(hb_ref[...], _unpack_rows(w_ref[:, g * D_MODEL:(g + 1) * D_MODEL]),
                       preferred_element_type=F32)

    def store_rope(p, o_ref):
        pb = p.astype(BF16)
        for h in range(HEADS):
            t1 = pb[:, h * HEAD_DIM:h * HEAD_DIM + HALF]
            t2 = pb[:, h * HEAD_DIM + HALF:(h + 1) * HEAD_DIM]
            o_ref[0, h, :, :HALF] = t1 * cos - t2 * sin
            o_ref[0, h, :, HALF:] = t1 * sin + t2 * cos

    def store_heads(pb, o_ref):
        for h in range(HEADS):
            o_ref[0, h] = pb[:, h * HEAD_DIM:(h + 1) * HEAD_DIM]

    store_rope(group(0), q_ref)
    hb_next = _modulated_norm(xn_ref[...], ng_ref[...], shn_ref[0], scn_ref[0]).astype(BF16)
    store_heads(_silu(group(3)).astype(BF16) * rg_ref[...].astype(BF16), za_ref)
    u_ref[0] = jax.nn.gelu(group(4)).astype(BF16)
    ga_ref[0] = jax.nn.sigmoid(group(7)).astype(BF16)
    t = jax.nn.gelu(group(5))
    mu = jnp.mean(t, axis=-1, keepdims=True)
    tc = t - mu
    tn = tc * lax.rsqrt(jnp.mean(tc * tc, axis=-1, keepdims=True) + EPS)
    vg_ref[0] = tn.astype(BF16) * lng_ref[...].astype(BF16) + lnb_ref[...].astype(BF16)
    gb_ref[0] = jax.nn.sigmoid(group(8)).astype(BF16)
    store_rope(group(1), k_ref)
    zb_ref[0] = _silu(group(6)).astype(BF16)
    store_heads(group(2).astype(BF16), v_ref)

    hb_ref[...] = hb_next


def _proj(x, mod_x, ng, w_in_b, cos, sin, ln_g, ln_b, ret_g):
    b, n, d = x.shape
    tm = PROJ_ROWS
    per_b = n // tm
    steps = b * per_b
    x2 = x.reshape(b * n, d)
    nxt = lambda s: jnp.minimum(s + 1, steps - 1)
    head_out = jax.ShapeDtypeStruct((b, HEADS, n, HEAD_DIM), BF16)
    flat_out = jax.ShapeDtypeStruct((b, n, d), BF16)
    head_spec = pl.BlockSpec((1, HEADS, tm, HEAD_DIM), lambda s: (s // per_b, 0, s % per_b, 0))
    flat_spec = pl.BlockSpec((1, tm, d), lambda s: (s // per_b, s % per_b, 0))
    vec = pl.BlockSpec((1, d), lambda s: (0, 0))
    return pl.pallas_call(
        _proj_kernel,
        grid=(steps,),
        in_specs=[
            pl.BlockSpec((tm, d), lambda s: (0, 0)),
            pl.BlockSpec((tm, d), lambda s: (nxt(s), 0)),
            pl.BlockSpec((1, 1, d), lambda s: (0, 0, 0)),
            pl.BlockSpec((1, 1, d), lambda s: (0, 0, 1)),
            pl.BlockSpec((1, 1, d), lambda s: (nxt(s) // per_b, 0, 0)),
            pl.BlockSpec((1, 1, d), lambda s: (nxt(s) // per_b, 0, 1)),
            vec,
            pl.BlockSpec(w_in_b.shape, lambda s: (0, 0), pipeline_mode=pl.Buffered(1)),
            pl.BlockSpec((tm, HALF), lambda s: (s % per_b, 0)),
            pl.BlockSpec((tm, HALF), lambda s: (s % per_b, 0)),
            vec, vec, vec,
        ],
        out_specs=[head_spec] * 4 + [flat_spec] * 5,
        out_shape=[head_out] * 4 + [flat_out] * 5,
        scratch_shapes=[pltpu.VMEM((tm, d), BF16)],
        compiler_params=pltpu.CompilerParams(
            dimension_semantics=("arbitrary",), vmem_limit_bytes=VMEM_LIMIT),
        name="proj",
    )(x2, x2, mod_x, mod_x, mod_x, mod_x, ng, w_in_b, cos, sin, ln_g, ln_b, ret_g)


def _dot_tn(a, b):
    return lax.dot_general(a, b, (((0,), (0,)), ((), ())), preferred_element_type=F32)


def _dot_nt(a, b):
    return lax.dot_general(a, b, (((1,), (1,)), ((), ())), preferred_element_type=F32)


def _ret_kernel(lg_ref, q_ref, k_ref, v_ref, za_ref, kc_ref, vc_ref, o_ref,
                sf_ref, tf_ref, tb_ref, dmat_ref, kdf_ref, kdb_ref, qdf_ref, qdb_ref, cwf_ref, cwb_ref,
                *, n_chunks, n_ctx):
    C = RET_CHUNK
    h = pl.program_id(0)
    lgf = lg_ref[0, h]
    lgb = lg_ref[1, h]

    @pl.when(pl.program_id(1) == 0)
    def _():
        ii = lax.broadcasted_iota(jnp.int32, (C, C), 0)
        jj = lax.broadcasted_iota(jnp.int32, (C, C), 1)
        diff = (ii - jj).astype(F32)
        dmat_ref[...] = jnp.where(diff >= 0.0,
                                  jnp.exp(lgf * jnp.maximum(diff, 0.0)),
                                  jnp.exp(lgb * jnp.maximum(-diff, 0.0))).astype(BF16)
        row = lax.broadcasted_iota(jnp.int32, (C, HEAD_DIM), 0).astype(F32)
        kdf_ref[...] = jnp.exp(lgf * (C - 1.0 - row)).astype(BF16)
        kdb_ref[...] = jnp.exp(lgb * row).astype(BF16)
        qdf_ref[...] = jnp.exp(lgf * (row + 1.0)).astype(BF16)
        qdb_ref[...] = jnp.exp(lgb * (C - row)).astype(BF16)
        crow = lax.broadcasted_iota(jnp.int32, (n_ctx, HEAD_DIM), 0).astype(F32)
        cwf_ref[...] = jnp.exp(lgf * (n_ctx - 1.0 - crow)).astype(BF16)
        cwb_ref[...] = jnp.exp(lgb * crow).astype(BF16)

    one = jnp.ones((1, 1), F32)
    cdec_f = jnp.exp(one * (lgf * C))
    cdec_b = jnp.exp(one * (lgb * C))

    kc = kc_ref[0, 0]
    vc = vc_ref[0, 0]
    sf_ref[...] = _dot_tn(kc * cwf_ref[...], vc)
    tf_ref[...] = _dot_tn(kc * cwb_ref[...], vc)

    for c in range(n_chunks - 1, -1, -1):
        tb_ref[c] = tf_ref[...].astype(BF16)
        if c > 0:
            sl = slice(c * C, (c + 1) * C)
            tf_ref[...] = cdec_b * tf_ref[...] + _dot_tn(k_ref[0, 0, sl, :] * kdb_ref[...],
                                                         v_ref[0, 0, sl, :])

    for c in range(n_chunks):
        sl = slice(c * C, (c + 1) * C)
        qq = q_ref[0, 0, sl, :]
        kk = k_ref[0, 0, sl, :]
        vv = v_ref[0, 0, sl, :]
        p = _dot_nt(qq, kk).astype(BF16) * dmat_ref[...]
        lhs = jnp.concatenate([p, qq * qdf_ref[...], qq * qdb_ref[...]], axis=1)
        rhs = jnp.concatenate([vv, sf_ref[...].astype(BF16), tb_ref[c]], axis=0)
        o = jnp.dot(lhs, rhs, preferred_element_type=F32)
        if c + 1 < n_chunks:
            sf_ref[...] = cdec_f * sf_ref[...] + _dot_tn(kk * kdf_ref[...], vv)
        mu = jnp.mean(o, axis=-1, keepdims=True)
        oc = o - mu
        rstd = lax.rsqrt(jnp.mean(oc * oc, axis=-1, keepdims=True) + EPS)
        o_ref[0, 0, sl, :] = (oc * rstd * za_ref[0, 0, sl, :].astype(F32)).astype(BF16)


def _retention(lg, q, k, v, za, kc, vc):
    b, hh, n, dh = q.shape
    n_ctx = kc.shape[2]
    n_chunks = n // RET_CHUNK
    seq = pl.BlockSpec((1, 1, n, dh), lambda j, i: (i, j, 0, 0))
    cseq = pl.BlockSpec((1, 1, n_ctx, dh), lambda j, i: (i, j, 0, 0))
    table = pltpu.VMEM((RET_CHUNK, dh), BF16)
    return pl.pallas_call(
        functools.partial(_ret_kernel, n_chunks=n_chunks, n_ctx=n_ctx),
        grid=(hh, b),
        in_specs=[
            pl.BlockSpec(memory_space=pltpu.SMEM),
            seq, seq, seq, seq, cseq, cseq,
        ],
        out_specs=seq,
        out_shape=jax.ShapeDtypeStruct((b, hh, n, dh), BF16),
        scratch_shapes=[
            pltpu.VMEM((dh, dh), F32),
            pltpu.VMEM((dh, dh), F32),
            pltpu.VMEM((n_chunks, dh, dh), BF16),
            pltpu.VMEM((RET_CHUNK, RET_CHUNK), BF16),
            table, table, table, table,
            pltpu.VMEM((n_ctx, dh), BF16), pltpu.VMEM((n_ctx, dh), BF16),
        ],
        compiler_params=pltpu.CompilerParams(
            dimension_semantics=("arbitrary", "arbitrary"), vmem_limit_bytes=VMEM_LIMIT),
        name="ret",
    )(lg, q, k, v, za, kc, vc)


def _out_kernel(x_ref, ya_ref, u_ref, vg_ref, zb_ref, ga_ref, gb_ref, gt_ref, ws_ref, bias_ref,
                wpa_ref, wpb_ref, wo_ref, fg_ref, o_ref, mix_ref):
    tm = x_ref.shape[1]
    n_sub = tm // MLP_CHUNK
    gw = D_MODEL // MLP_GROUPS
    for g in range(MLP_GROUPS):
        cols = slice(g * gw, (g + 1) * gw)
        rhs = jnp.concatenate(
            [vg_ref[0, c * MLP_CHUNK:(c + 1) * MLP_CHUNK, cols] for c in range(n_sub)], axis=1)
        m = jnp.dot(ws_ref[g], rhs, preferred_element_type=F32)
        for c in range(n_sub):
            mix_ref[c * MLP_CHUNK:(c + 1) * MLP_CHUNK, cols] = (
                m[:, c * gw:(c + 1) * gw] + bias_ref[:, cols])
    yb = (u_ref[0].astype(F32) * mix_ref[...] * zb_ref[0].astype(F32)).astype(BF16)
    ya = jnp.concatenate([ya_ref[0, h] for h in range(HEADS)], axis=1)
    pa = jnp.dot(ya, _unpack_rows(wpa_ref[...]), preferred_element_type=F32)
    pb = jnp.dot(yb, _unpack_rows(wpb_ref[...]), preferred_element_type=F32)
    merged = (ga_ref[0].astype(F32) * pa + gb_ref[0].astype(F32) * pb).astype(BF16)
    out = jnp.dot(merged, _unpack_rows(wo_ref[...]), preferred_element_type=F32)
    xo = x_ref[0] + gt_ref[0] * out
    ms = jnp.mean(xo * xo, axis=-1, keepdims=True)
    o_ref[0] = xo * lax.rsqrt(ms + EPS) * fg_ref[...]


def _output(x, ya, u, vg, zb, ga, gb, mod_x, ws_b, bias_full, wpa_b, wpb_b, wo_b, fg):
    b, n, d = x.shape
    tm = OUT_ROWS
    flat = pl.BlockSpec((1, tm, d), lambda i, j: (i, j, 0))
    const2 = lambda shape: pl.BlockSpec(shape, lambda i, j: (0,) * len(shape))
    return pl.pallas_call(
        _out_kernel,
        grid=(b, n // tm),
        in_specs=[
            flat,
            pl.BlockSpec((1, HEADS, tm, HEAD_DIM), lambda i, j: (i, 0, j, 0)),
            flat, flat, flat, flat, flat,
            pl.BlockSpec((1, 1, d), lambda i, j: (i, 0, 2)),
            const2(ws_b.shape), const2(bias_full.shape),
            const2(wpa_b.shape), const2(wpb_b.shape), const2(wo_b.shape),
            const2((1, d)),
        ],
        out_specs=flat,
        out_shape=jax.ShapeDtypeStruct((b, n, d), F32),
        scratch_shapes=[pltpu.VMEM((tm, d), F32)],
        compiler_params=pltpu.CompilerParams(
            dimension_semantics=("arbitrary", "arbitrary"), vmem_limit_bytes=VMEM_LIMIT),
        name="out",
    )(x, ya, u, vg, zb, ga, gb, mod_x, ws_b, bias_full, wpa_b, wpb_b, wo_b, fg)


def kernel(x, c, ctx, c_ctx, w_mod, b_mod, norm_g, w_in, ret_decay_fwd, ret_decay_bwd, ret_norm_g,
           mlp_ln_g, mlp_ln_b, mlp_ws, mlp_bs, w_proj_a, w_proj_b, w_out, final_norm_g):
    b, n, d = x.shape
    assert d == D_MODEL and n % RET_CHUNK == 0 and n % OUT_ROWS == 0 and n % PROJ_ROWS == 0
    assert w_mod.shape[0] == 1, "single-layer block"

    cond = jnp.concatenate([c, c_ctx[None, :], jnp.zeros((16 - b - 1, d), F32)], axis=0)
    mod = _modulation(cond, w_mod[0], b_mod[0][None, :])
    mod_x = mod[:b].reshape(b, 1, 3 * d)
    mod_c = mod[b:b + 1]

    q_scale = jnp.concatenate([jnp.full((d,), HEAD_DIM ** -0.5, F32),
                               jnp.ones((w_in.shape[2] - d,), F32)])
    (w_in_b,) = _pack_rows([w_in[0]], q_scale[None, :], PACK_COLS)
    wpa_b, wpb_b, wo_b = _pack_rows([w_proj_a[0], w_proj_b[0], w_out[0]], jnp.ones((1, d), F32),
                                    PACK_COLS // 2)
    ng = norm_g[0][None, :]
    lg = jnp.stack([jax.nn.log_sigmoid(ret_decay_fwd[0].astype(F32)),
                    jax.nn.log_sigmoid(ret_decay_bwd[0].astype(F32))])
    cos, sin = _rope_tables(n)

    kc, vc = _ctx_proj(ctx, mod_c, ng, w_in_b)
    q, k, v, za, u, vg, zb, ga, gb = _proj(
        x, mod_x, ng, w_in_b, jnp.asarray(cos).astype(BF16), jnp.asarray(sin).astype(BF16),
        mlp_ln_g[0][None, :], mlp_ln_b[0][None, :], ret_norm_g[0][None, :])
    ya = _retention(lg, q, k, v, za, kc, vc)

    bias_full = jnp.repeat(mlp_bs[0].T, d // MLP_GROUPS, axis=1)
    return _output(x, ya, u, vg, zb, ga, gb, mod_x, mlp_ws[0].astype(BF16), bias_full,
                   wpa_b, wpb_b, wo_b,
                   final_norm_g[None, :])
```

```python
import functools

import numpy as np
import jax
import jax.numpy as jnp
from jax import lax
from jax.experimental import pallas as pl
from jax.experimental.pallas import tpu as pltpu

F32 = jnp.float32
BF16 = jnp.bfloat16

D_MODEL = 1024
HEADS = 4
HEAD_DIM = 256
HALF = HEAD_DIM // 2
GRID_W = 64
MLP_GROUPS = 8
MLP_CHUNK = 128
ROPE_BASE = 10000.0
EPS = 1e-6
N_COL_GROUPS = 9

RET_CHUNK = 256
RET_HEADS_PER_STEP = 2
PROJ_ROWS = 512
PROJ_SUB_ROWS = 256
PACK_COLS = 1024
OUT_ROWS = 512
VMEM_LIMIT = 52 * 1024 * 1024


def _rope_tables(n):
    pos = np.arange(n)
    row = (pos // GRID_W).astype(np.float64)
    col = (pos % GRID_W).astype(np.float64)
    quarter = HALF // 2
    inv = np.power(ROPE_BASE, -np.arange(quarter, dtype=np.float64) / quarter)
    ang = np.concatenate([row[:, None] * inv, col[:, None] * inv], axis=-1)
    return np.cos(ang).astype(np.float32), np.sin(ang).astype(np.float32)


def _silu(x):
    return x * jax.nn.sigmoid(x)


def _pack_rows_kernel(*refs):
    n_w = (len(refs) - 1) // 2
    s_ref = refs[n_w]
    for w_ref, o_ref in zip(refs[:n_w], refs[n_w + 1:]):
        o_ref[...] = pltpu.bitcast((w_ref[...] * s_ref[...]).astype(BF16), jnp.uint32)


def _pack_rows(ws, col_scale, bn):
    k, n = ws[0].shape
    w_spec = pl.BlockSpec((k, bn), lambda j: (0, j))
    o_spec = pl.BlockSpec((k // 2, bn), lambda j: (0, j))
    return pl.pallas_call(
        _pack_rows_kernel,
        grid=(n // bn,),
        in_specs=[w_spec] * len(ws) + [pl.BlockSpec((1, bn), lambda j: (0, j))],
        out_specs=[o_spec] * len(ws),
        out_shape=[jax.ShapeDtypeStruct((k // 2, n), jnp.uint32)] * len(ws),
        name="pack_rows",
    )(*ws, col_scale)


def _unpack_rows(w32):
    return pltpu.bitcast(w32, BF16)


def _mod_kernel(c_ref, w_ref, b_ref, o_ref):
    s = _silu(c_ref[...])
    o_ref[...] = jnp.dot(s, w_ref[...], preferred_element_type=F32,
                         precision=lax.Precision.HIGHEST) + b_ref[...]


def _modulation(cond, w_mod, b_mod):
    rows, d = cond.shape
    width = w_mod.shape[1]
    bn = 512
    return pl.pallas_call(
        _mod_kernel,
        grid=(width // bn,),
        in_specs=[
            pl.BlockSpec((rows, d), lambda j: (0, 0)),
            pl.BlockSpec((d, bn), lambda j: (0, j)),
            pl.BlockSpec((1, bn), lambda j: (0, j)),
        ],
        out_specs=pl.BlockSpec((rows, bn), lambda j: (0, j)),
        out_shape=jax.ShapeDtypeStruct((rows, width), F32),
        name="mod",
    )(cond, w_mod, b_mod)


def _modulated_norm(x, ng, sh, sc):
    ms = jnp.mean(x * x, axis=-1, keepdims=True)
    return x * lax.rsqrt(ms + EPS) * (ng * (1.0 + sc)) + sh


def _ctx_proj_kernel(x_ref, sh_ref, sc_ref, ng_ref, wk_ref, wv_ref, k_ref, v_ref):
    hb = _modulated_norm(x_ref[0], ng_ref[...], sh_ref[...], sc_ref[...]).astype(BF16)
    pk = jnp.dot(hb, _unpack_rows(wk_ref[...]), preferred_element_type=F32)
    pv = jnp.dot(hb, _unpack_rows(wv_ref[...]), preferred_element_type=F32)
    for h in range(HEADS):
        k_ref[0, h] = pk[:, h * HEAD_DIM:(h + 1) * HEAD_DIM].astype(BF16)
        v_ref[0, h] = pv[:, h * HEAD_DIM:(h + 1) * HEAD_DIM].astype(BF16)


def _ctx_proj(ctx, mod_c, ng, w_in_b):
    b, n, d = ctx.shape
    head_out = jax.ShapeDtypeStruct((b, HEADS, n, HEAD_DIM), BF16)
    head_spec = pl.BlockSpec((1, HEADS, n, HEAD_DIM), lambda i: (i, 0, 0, 0))
    return pl.pallas_call(
        _ctx_proj_kernel,
        grid=(b,),
        in_specs=[
            pl.BlockSpec((1, n, d), lambda i: (i, 0, 0)),
            pl.BlockSpec((1, d), lambda i: (0, 0)),
            pl.BlockSpec((1, d), lambda i: (0, 1)),
            pl.BlockSpec((1, d), lambda i: (0, 0)),
            pl.BlockSpec((d // 2, d), lambda i: (0, 1)),
            pl.BlockSpec((d // 2, d), lambda i: (0, 2)),
        ],
        out_specs=[head_spec, head_spec],
        out_shape=[head_out, head_out],
        name="ctx_proj",
    )(ctx, mod_c, mod_c, ng, w_in_b, w_in_b)


def _proj_kernel(x0_ref, xn_ref, sh0_ref, sc0_ref, shn_ref, scn_ref, ng_ref, w_ref, cos_ref, sin_ref,
                 lng_ref, lnb_ref, rg_ref,
                 q_ref, k_ref, v_ref, za_ref, u_ref, vg_ref, zb_ref, ga_ref, gb_ref, hb_ref):
    @pl.when(pl.program_id(0) == 0)
    def _():
        hb_ref[...] = _modulated_norm(x0_ref[...], ng_ref[...], sh0_ref[0], sc0_ref[0]).astype(BF16)

    hb_next = _modulated_norm(xn_ref[...], ng_ref[...], shn_ref[0], scn_ref[0]).astype(BF16)

    for r in range(PROJ_ROWS // PROJ_SUB_ROWS):
        rows = slice(r * PROJ_SUB_ROWS, (r + 1) * PROJ_SUB_ROWS)
        cos = cos_ref[rows, :]
        sin = sin_ref[rows, :]

        def group(g):
            return jnp.dot(hb_ref[rows, :], _unpack_rows(w_ref[:, g * D_MODEL:(g + 1) * D_MODEL]),
                           preferred_element_type=F32)

        def store_rope(p, o_ref):
            pb = p.astype(BF16)
            for h in range(HEADS):
                t1 = pb[:, h * HEAD_DIM:h * HEAD_DIM + HALF]
                t2 = pb[:, h * HEAD_DIM + HALF:(h + 1) * HEAD_DIM]
                o_ref[0, h, rows, :HALF] = t1 * cos - t2 * sin
                o_ref[0, h, rows, HALF:] = t1 * sin + t2 * cos

        def store_heads(pb, o_ref):
            for h in range(HEADS):
                o_ref[0, h, rows, :] = pb[:, h * HEAD_DIM:(h + 1) * HEAD_DIM]

        store_rope(group(0), q_ref)
        store_heads(_silu(group(3)).astype(BF16) * rg_ref[...].astype(BF16), za_ref)
        u_ref[0, rows, :] = jax.nn.gelu(group(4)).astype(BF16)
        ga_ref[0, rows, :] = jax.nn.sigmoid(group(7)).astype(BF16)
        t = jax.nn.gelu(group(5))
        mu = jnp.mean(t, axis=-1, keepdims=True)
        tc = t - mu
        tn = tc * lax.rsqrt(jnp.mean(tc * tc, axis=-1, keepdims=True) + EPS)
        vg_ref[0, rows, :] = (tn.astype(BF16) * lng_ref[...].astype(BF16)
                              + lnb_ref[...].astype(BF16))
        gb_ref[0, rows, :] = jax.nn.sigmoid(group(8)).astype(BF16)
        store_rope(group(1), k_ref)
        zb_ref[0, rows, :] = _silu(group(6)).astype(BF16)
        store_heads(group(2).astype(BF16), v_ref)

    hb_ref[...] = hb_next


def _proj(x, mod_x, ng, w_in_b, cos, sin, ln_g, ln_b, ret_g):
    b, n, d = x.shape
    tm = PROJ_ROWS
    per_b = n // tm
    steps = b * per_b
    x2 = x.reshape(b * n, d)
    nxt = lambda s: jnp.minimum(s + 1, steps - 1)
    head_out = jax.ShapeDtypeStruct((b, HEADS, n, HEAD_DIM), BF16)
    flat_out = jax.ShapeDtypeStruct((b, n, d), BF16)
    head_spec = pl.BlockSpec((1, HEADS, tm, HEAD_DIM), lambda s: (s // per_b, 0, s % per_b, 0))
    flat_spec = pl.BlockSpec((1, tm, d), lambda s: (s // per_b, s % per_b, 0))
    vec = pl.BlockSpec((1, d), lambda s: (0, 0))
    return pl.pallas_call(
        _proj_kernel,
        grid=(steps,),
        in_specs=[
            pl.BlockSpec((tm, d), lambda s: (0, 0)),
            pl.BlockSpec((tm, d), lambda s: (nxt(s), 0)),
            pl.BlockSpec((1, 1, d), lambda s: (0, 0, 0)),
            pl.BlockSpec((1, 1, d), lambda s: (0, 0, 1)),
            pl.BlockSpec((1, 1, d), lambda s: (nxt(s) // per_b, 0, 0)),
            pl.BlockSpec((1, 1, d), lambda s: (nxt(s) // per_b, 0, 1)),
            vec,
            pl.BlockSpec(w_in_b.shape, lambda s: (0, 0), pipeline_mode=pl.Buffered(1)),
            pl.BlockSpec((tm, HALF), lambda s: (s % per_b, 0)),
            pl.BlockSpec((tm, HALF), lambda s: (s % per_b, 0)),
            vec, vec, vec,
        ],
        out_specs=[head_spec] * 4 + [flat_spec] * 5,
        out_shape=[head_out] * 4 + [flat_out] * 5,
        scratch_shapes=[pltpu.VMEM((tm, d), BF16)],
        compiler_params=pltpu.CompilerParams(
            dimension_semantics=("arbitrary",), vmem_limit_bytes=VMEM_LIMIT),
        name="proj",
    )(x2, x2, mod_x, mod_x, mod_x, mod_x, ng, w_in_b, cos, sin, ln_g, ln_b, ret_g)


def _dot_tn(a, b):
    return lax.dot_general(a, b, (((0,), (0,)), ((), ())), preferred_element_type=F32)


def _dot_nt(a, b):
    return lax.dot_general(a, b, (((1,), (1,)), ((), ())), preferred_element_type=F32)


def _ret_kernel(lg_ref, q_ref, k_ref, v_ref, za_ref, kc_ref, vc_ref, o_ref,
                sf_ref, tf_ref, tb_ref, dmat_ref, kdf_ref, kdb_ref, qdf_ref, qdb_ref, cwf_ref, cwb_ref,
                *, n_chunks, n_ctx):
    C = RET_CHUNK
    heads = range(RET_HEADS_PER_STEP)
    lgf = [lg_ref[0, pl.program_id(0) * RET_HEADS_PER_STEP + hh] for hh in heads]
    lgb = [lg_ref[1, pl.program_id(0) * RET_HEADS_PER_STEP + hh] for hh in heads]

    @pl.when(pl.program_id(1) == 0)
    def _():
        ii = lax.broadcasted_iota(jnp.int32, (C, C), 0)
        jj = lax.broadcasted_iota(jnp.int32, (C, C), 1)
        diff = (ii - jj).astype(F32)
        row = lax.broadcasted_iota(jnp.int32, (C, HEAD_DIM), 0).astype(F32)
        crow = lax.broadcasted_iota(jnp.int32, (n_ctx, HEAD_DIM), 0).astype(F32)
        for hh in heads:
            dmat_ref[hh] = jnp.where(diff >= 0.0,
                                     jnp.exp(lgf[hh] * jnp.maximum(diff, 0.0)),
                                     jnp.exp(lgb[hh] * jnp.maximum(-diff, 0.0))).astype(BF16)
            kdf_ref[hh] = jnp.exp(lgf[hh] * (C - 1.0 - row)).astype(BF16)
            kdb_ref[hh] = jnp.exp(lgb[hh] * row).astype(BF16)
            qdf_ref[hh] = jnp.exp(lgf[hh] * (row + 1.0)).astype(BF16)
            qdb_ref[hh] = jnp.exp(lgb[hh] * (C - row)).astype(BF16)
            cwf_ref[hh] = jnp.exp(lgf[hh] * (n_ctx - 1.0 - crow)).astype(BF16)
            cwb_ref[hh] = jnp.exp(lgb[hh] * crow).astype(BF16)

    one = jnp.ones((1, 1), F32)
    cdec_f = [jnp.exp(one * (lgf[hh] * C)) for hh in heads]
    cdec_b = [jnp.exp(one * (lgb[hh] * C)) for hh in heads]

    for hh in heads:
        kc = kc_ref[0, hh]
        vc = vc_ref[0, hh]
        sf_ref[hh] = _dot_tn(kc * cwf_ref[hh], vc)
        tf_ref[hh] = _dot_tn(kc * cwb_ref[hh], vc)

    for c in range(n_chunks - 1, -1, -1):
        sl = slice(c * C, (c + 1) * C)
        for hh in heads:
            tb_ref[hh, c] = tf_ref[hh].astype(BF16)
            if c > 0:
                tf_ref[hh] = cdec_b[hh] * tf_ref[hh] + _dot_tn(k_ref[0, hh, sl, :] * kdb_ref[hh],
                                                               v_ref[0, hh, sl, :])

    for c in range(n_chunks):
        sl = slice(c * C, (c + 1) * C)
        for hh in heads:
            qq = q_ref[0, hh, sl, :]
            kk = k_ref[0, hh, sl, :]
            vv = v_ref[0, hh, sl, :]
            p = _dot_nt(qq, kk).astype(BF16) * dmat_ref[hh]
            lhs = jnp.concatenate([p, qq * qdf_ref[hh], qq * qdb_ref[hh]], axis=1)
            rhs = jnp.concatenate([vv, sf_ref[hh].astype(BF16), tb_ref[hh, c]], axis=0)
            o = jnp.dot(lhs, rhs, preferred_element_type=F32)
            if c + 1 < n_chunks:
                sf_ref[hh] = cdec_f[hh] * sf_ref[hh] + _dot_tn(kk * kdf_ref[hh], vv)
            mu = jnp.mean(o, axis=-1, keepdims=True)
            oc = o - mu
            rstd = lax.rsqrt(jnp.mean(oc * oc, axis=-1, keepdims=True) + EPS)
            o_ref[0, hh, sl, :] = (oc * rstd * za_ref[0, hh, sl, :].astype(F32)).astype(BF16)


def _retention(lg, q, k, v, za, kc, vc):
    b, hh, n, dh = q.shape
    n_ctx = kc.shape[2]
    n_chunks = n // RET_CHUNK
    hp = RET_HEADS_PER_STEP
    seq = pl.BlockSpec((1, hp, n, dh), lambda j, i: (i, j, 0, 0))
    cseq = pl.BlockSpec((1, hp, n_ctx, dh), lambda j, i: (i, j, 0, 0))
    table = pltpu.VMEM((hp, RET_CHUNK, dh), BF16)
    ctable = pltpu.VMEM((hp, n_ctx, dh), BF16)
    return pl.pallas_call(
        functools.partial(_ret_kernel, n_chunks=n_chunks, n_ctx=n_ctx),
        grid=(hh // hp, b),
        in_specs=[
            pl.BlockSpec(memory_space=pltpu.SMEM),
            seq, seq, seq, seq, cseq, cseq,
        ],
        out_specs=seq,
        out_shape=jax.ShapeDtypeStruct((b, hh, n, dh), BF16),
        scratch_shapes=[
            pltpu.VMEM((hp, dh, dh), F32),
            pltpu.VMEM((hp, dh, dh), F32),
            pltpu.VMEM((hp, n_chunks, dh, dh), BF16),
            pltpu.VMEM((hp, RET_CHUNK, RET_CHUNK), BF16),
            table, table, table, table, ctable, ctable,
        ],
        compiler_params=pltpu.CompilerParams(
            dimension_semantics=("arbitrary", "arbitrary"), vmem_limit_bytes=VMEM_LIMIT),
        name="ret",
    )(lg, q, k, v, za, kc, vc)


def _out_kernel(x_ref, ya_ref, u_ref, vg_ref, zb_ref, ga_ref, gb_ref, gt_ref, ws_ref, bias_ref,
                wpa_ref, wpb_ref, wo_ref, fg_ref, o_ref, mix_ref):
    tm = x_ref.shape[1]
    n_sub = tm // MLP_CHUNK
    gw = D_MODEL // MLP_GROUPS
    for g in range(MLP_GROUPS):
        cols = slice(g * gw, (g + 1) * gw)
        rhs = jnp.concatenate(
            [vg_ref[0, c * MLP_CHUNK:(c + 1) * MLP_CHUNK, cols] for c in range(n_sub)], axis=1)
        m = jnp.dot(ws_ref[g], rhs, preferred_element_type=F32)
        for c in range(n_sub):
            mix_ref[c * MLP_CHUNK:(c + 1) * MLP_CHUNK, cols] = (
                m[:, c * gw:(c + 1) * gw] + bias_ref[:, cols])
    yb = (u_ref[0].astype(F32) * mix_ref[...] * zb_ref[0].astype(F32)).astype(BF16)
    ya = jnp.concatenate([ya_ref[0, h] for h in range(HEADS)], axis=1)
    pa = jnp.dot(ya, _unpack_rows(wpa_ref[...]), preferred_element_type=F32)
    pb = jnp.dot(yb, _unpack_rows(wpb_ref[...]), preferred_element_type=F32)
    merged = (ga_ref[0].astype(F32) * pa + gb_ref[0].astype(F32) * pb).astype(BF16)
    out = jnp.dot(merged, _unpack_rows(wo_ref[...]), preferred_element_type=F32)
    xo = x_ref[0] + gt_ref[0] * out
    ms = jnp.mean(xo * xo, axis=-1, keepdims=True)
    o_ref[0] = xo * lax.rsqrt(ms + EPS) * fg_ref[...]


def _output(x, ya, u, vg, zb, ga, gb, mod_x, ws_b, bias_full, wpa_b, wpb_b, wo_b, fg):
    b, n, d = x.shape
    tm = OUT_ROWS
    flat = pl.BlockSpec((1, tm, d), lambda i, j: (i, j, 0))
    const2 = lambda shape: pl.BlockSpec(shape, lambda i, j: (0,) * len(shape))
    return pl.pallas_call(
        _out_kernel,
        grid=(b, n // tm),
        in_specs=[
            flat,
            pl.BlockSpec((1, HEADS, tm, HEAD_DIM), lambda i, j: (i, 0, j, 0)),
            flat, flat, flat, flat, flat,
            pl.BlockSpec((1, 1, d), lambda i, j: (i, 0, 2)),
            const2(ws_b.shape), const2(bias_full.shape),
            const2(wpa_b.shape), const2(wpb_b.shape), const2(wo_b.shape),
            const2((1, d)),
        ],
        out_specs=flat,
        out_shape=jax.ShapeDtypeStruct((b, n, d), F32),
        scratch_shapes=[pltpu.VMEM((tm, d), F32)],
        compiler_params=pltpu.CompilerParams(
            dimension_semantics=("arbitrary", "arbitrary"), vmem_limit_bytes=VMEM_LIMIT),
        name="out",
    )(x, ya, u, vg, zb, ga, gb, mod_x, ws_b, bias_full, wpa_b, wpb_b, wo_b, fg)


def kernel(x, c, ctx, c_ctx, w_mod, b_mod, norm_g, w_in, ret_decay_fwd, ret_decay_bwd, ret_norm_g,
           mlp_ln_g, mlp_ln_b, mlp_ws, mlp_bs, w_proj_a, w_proj_b, w_out, final_norm_g):
    b, n, d = x.shape
    assert d == D_MODEL and n % RET_CHUNK == 0 and n % OUT_ROWS == 0 and n % PROJ_ROWS == 0
    assert w_mod.shape[0] == 1, "single-layer block"

    cond = jnp.concatenate([c, c_ctx[None, :], jnp.zeros((16 - b - 1, d), F32)], axis=0)
    mod = _modulation(cond, w_mod[0], b_mod[0][None, :])
    mod_x = mod[:b].reshape(b, 1, 3 * d)
    mod_c = mod[b:b + 1]

    q_scale = jnp.concatenate([jnp.full((d,), HEAD_DIM ** -0.5, F32),
                               jnp.ones((w_in.shape[2] - d,), F32)])
    (w_in_b,) = _pack_rows([w_in[0]], q_scale[None, :], PACK_COLS)
    wpa_b, wpb_b, wo_b = _pack_rows([w_proj_a[0], w_proj_b[0], w_out[0]], jnp.ones((1, d), F32),
                                    PACK_COLS // 2)
    ng = norm_g[0][None, :]
    lg = jnp.stack([jax.nn.log_sigmoid(ret_decay_fwd[0].astype(F32)),
                    jax.nn.log_sigmoid(ret_decay_bwd[0].astype(F32))])
    cos, sin = _rope_tables(n)

    kc, vc = _ctx_proj(ctx, mod_c, ng, w_in_b)
    q, k, v, za, u, vg, zb, ga, gb = _proj(
        x, mod_x, ng, w_in_b, jnp.asarray(cos).astype(BF16), jnp.asarray(sin).astype(BF16),
        mlp_ln_g[0][None, :], mlp_ln_b[0][None, :], ret_norm_g[0][None, :])
    ya = _retention(lg, q, k, v, za, kc, vc)

    bias_full = jnp.repeat(mlp_bs[0].T, d // MLP_GROUPS, axis=1)
    return _output(x, ya, u, vg, zb, ga, gb, mod_x, mlp_ws[0].astype(BF16), bias_full,
                   wpa_b, wpb_b, wo_b,
                   final_norm_g[None, :])
```

```python
import functools

import numpy as np
import jax
import jax.numpy as jnp
from jax import lax
from jax.experimental import pallas as pl
from jax.experimental.pallas import tpu as pltpu

F32 = jnp.float32
BF16 = jnp.bfloat16

D_MODEL = 1024
HEADS = 4
HEAD_DIM = 256
HALF = HEAD_DIM // 2
GRID_W = 64
MLP_GROUPS = 8
MLP_CHUNK = 128
ROPE_BASE = 10000.0
EPS = 1e-6
N_COL_GROUPS = 9

RET_CHUNK = 256
RET_HEADS_PER_STEP = 2
PROJ_ROWS = 512
PROJ_SUB_ROWS = 256
PACK_COLS = 1024
MOD_COLS = 1024
OUT_ROWS = 512
OUT_SUB_ROWS = 256
VMEM_LIMIT = 52 * 1024 * 1024


def _rope_tables(n):
    pos = np.arange(n)
    row = (pos // GRID_W).astype(np.float64)
    col = (pos % GRID_W).astype(np.float64)
    quarter = HALF // 2
    inv = np.power(ROPE_BASE, -np.arange(quarter, dtype=np.float64) / quarter)
    ang = np.concatenate([row[:, None] * inv, col[:, None] * inv], axis=-1)
    return np.cos(ang).astype(np.float32), np.sin(ang).astype(np.float32)


def _silu(x):
    return x * jax.nn.sigmoid(x)


def _pack_rows_kernel(*refs):
    n_w = (len(refs) - 1) // 2
    s_ref = refs[n_w]
    for w_ref, o_ref in zip(refs[:n_w], refs[n_w + 1:]):
        o_ref[...] = pltpu.bitcast((w_ref[...] * s_ref[...]).astype(BF16), jnp.uint32)


def _pack_rows(ws, col_scale, bn):
    k, n = ws[0].shape
    w_spec = pl.BlockSpec((k, bn), lambda j: (0, j))
    o_spec = pl.BlockSpec((k // 2, bn), lambda j: (0, j))
    return pl.pallas_call(
        _pack_rows_kernel,
        grid=(n // bn,),
        in_specs=[w_spec] * len(ws) + [pl.BlockSpec((1, bn), lambda j: (0, j))],
        out_specs=[o_spec] * len(ws),
        out_shape=[jax.ShapeDtypeStruct((k // 2, n), jnp.uint32)] * len(ws),
        name="pack_rows",
    )(*ws, col_scale)


def _unpack_rows(w32):
    return pltpu.bitcast(w32, BF16)


def _mod_kernel(c_ref, cc_ref, w_ref, b_ref, ox_ref, oc_ref):
    w = w_ref[...].astype(BF16)
    mx = jnp.dot(_silu(c_ref[...]).astype(BF16), w, preferred_element_type=F32) + b_ref[...]
    for i in range(ox_ref.shape[0]):
        ox_ref[i] = mx[i:i + 1, :]
    oc_ref[...] = jnp.dot(_silu(cc_ref[...]).astype(BF16), w, preferred_element_type=F32) + b_ref[...]


def _modulation(c, c_ctx, w_mod, b_mod):
    rows, d = c.shape
    width = w_mod.shape[1]
    bn = MOD_COLS
    return pl.pallas_call(
        _mod_kernel,
        grid=(width // bn,),
        in_specs=[
            pl.BlockSpec((rows, d), lambda j: (0, 0)),
            pl.BlockSpec((1, d), lambda j: (0, 0)),
            pl.BlockSpec((d, bn), lambda j: (0, j)),
            pl.BlockSpec((1, bn), lambda j: (0, j)),
        ],
        out_specs=[pl.BlockSpec((rows, 1, bn), lambda j: (0, 0, j)),
                   pl.BlockSpec((1, bn), lambda j: (0, j))],
        out_shape=[jax.ShapeDtypeStruct((rows, 1, width), F32),
                   jax.ShapeDtypeStruct((1, width), F32)],
        name="mod",
    )(c, c_ctx, w_mod, b_mod)


def _modulated_norm(x, ng, sh, sc):
    ms = jnp.mean(x * x, axis=-1, keepdims=True)
    return x * lax.rsqrt(ms + EPS) * (ng * (1.0 + sc)) + sh


def _ctx_proj_kernel(x_ref, sh_ref, sc_ref, ng_ref, wk_ref, wv_ref, k_ref, v_ref):
    hb = _modulated_norm(x_ref[0], ng_ref[...], sh_ref[...], sc_ref[...]).astype(BF16)
    pk = jnp.dot(hb, _unpack_rows(wk_ref[...]), preferred_element_type=F32)
    pv = jnp.dot(hb, _unpack_rows(wv_ref[...]), preferred_element_type=F32)
    for h in range(HEADS):
        k_ref[0, h] = pk[:, h * HEAD_DIM:(h + 1) * HEAD_DIM].astype(BF16)
        v_ref[0, h] = pv[:, h * HEAD_DIM:(h + 1) * HEAD_DIM].astype(BF16)


def _ctx_proj(ctx, mod_c, ng, w_in_b):
    b, n, d = ctx.shape
    head_out = jax.ShapeDtypeStruct((b, HEADS, n, HEAD_DIM), BF16)
    head_spec = pl.BlockSpec((1, HEADS, n, HEAD_DIM), lambda i: (i, 0, 0, 0))
    return pl.pallas_call(
        _ctx_proj_kernel,
        grid=(b,),
        in_specs=[
            pl.BlockSpec((1, n, d), lambda i: (i, 0, 0)),
            pl.BlockSpec((1, d), lambda i: (0, 0)),
            pl.BlockSpec((1, d), lambda i: (0, 1)),
            pl.BlockSpec((1, d), lambda i: (0, 0)),
            pl.BlockSpec((d // 2, d), lambda i: (0, 1)),
            pl.BlockSpec((d // 2, d), lambda i: (0, 2)),
        ],
        out_specs=[head_spec, head_spec],
        out_shape=[head_out, head_out],
        name="ctx_proj",
    )(ctx, mod_c, mod_c, ng, w_in_b, w_in_b)


def _proj_kernel(x0_ref, xn_ref, sh0_ref, sc0_ref, shn_ref, scn_ref, ng_ref, w_ref, cos_ref, sin_ref,
                 lng_ref, lnb_ref, rg_ref,
                 q_ref, k_ref, v_ref, za_ref, u_ref, vg_ref, zb_ref, ga_ref, gb_ref, hb_ref):
    @pl.when(pl.program_id(0) == 0)
    def _():
        hb_ref[...] = _modulated_norm(x0_ref[...], ng_ref[...], sh0_ref[0], sc0_ref[0]).astype(BF16)

    hb_next = _modulated_norm(xn_ref[...], ng_ref[...], shn_ref[0], scn_ref[0]).astype(BF16)

    for r in range(PROJ_ROWS // PROJ_SUB_ROWS):
        rows = slice(r * PROJ_SUB_ROWS, (r + 1) * PROJ_SUB_ROWS)
        cos = cos_ref[rows, :]
        sin = sin_ref[rows, :]

        def group(g):
            return jnp.dot(hb_ref[rows, :], _unpack_rows(w_ref[:, g * D_MODEL:(g + 1) * D_MODEL]),
                           preferred_element_type=F32)

        def store_rope(p, o_ref):
            pb = p.astype(BF16)
            for h in range(HEADS):
                t1 = pb[:, h * HEAD_DIM:h * HEAD_DIM + HALF]
                t2 = pb[:, h * HEAD_DIM + HALF:(h + 1) * HEAD_DIM]
                o_ref[0, h, rows, :HALF] = t1 * cos - t2 * sin
                o_ref[0, h, rows, HALF:] = t1 * sin + t2 * cos

        def store_heads(pb, o_ref):
            for h in range(HEADS):
                o_ref[0, h, rows, :] = pb[:, h * HEAD_DIM:(h + 1) * HEAD_DIM]

        store_rope(group(0), q_ref)
        store_heads(_silu(group(3)).astype(BF16) * rg_ref[...].astype(BF16), za_ref)
        u_ref[0, rows, :] = jax.nn.gelu(group(4)).astype(BF16)
        ga_ref[0, rows, :] = jax.nn.sigmoid(group(7)).astype(BF16)
        t = jax.nn.gelu(group(5))
        mu = jnp.mean(t, axis=-1, keepdims=True)
        tc = t - mu
        tn = tc * lax.rsqrt(jnp.mean(tc * tc, axis=-1, keepdims=True) + EPS)
        vg_ref[0, rows, :] = (tn.astype(BF16) * lng_ref[...].astype(BF16)
                              + lnb_ref[...].astype(BF16))
        gb_ref[0, rows, :] = jax.nn.sigmoid(group(8)).astype(BF16)
        store_rope(group(1), k_ref)
        zb_ref[0, rows, :] = _silu(group(6)).astype(BF16)
        store_heads(group(2).astype(BF16), v_ref)

    hb_ref[...] = hb_next


def _proj(x, mod_x, ng, w_in_b, cos, sin, ln_g, ln_b, ret_g):
    b, n, d = x.shape
    tm = PROJ_ROWS
    per_b = n // tm
    steps = b * per_b
    x2 = x.reshape(b * n, d)
    nxt = lambda s: jnp.minimum(s + 1, steps - 1)
    head_out = jax.ShapeDtypeStruct((b, HEADS, n, HEAD_DIM), BF16)
    flat_out = jax.ShapeDtypeStruct((b, n, d), BF16)
    head_spec = pl.BlockSpec((1, HEADS, tm, HEAD_DIM), lambda s: (s // per_b, 0, s % per_b, 0))
    flat_spec = pl.BlockSpec((1, tm, d), lambda s: (s // per_b, s % per_b, 0))
    vec = pl.BlockSpec((1, d), lambda s: (0, 0))
    return pl.pallas_call(
        _proj_kernel,
        grid=(steps,),
        in_specs=[
            pl.BlockSpec((tm, d), lambda s: (0, 0)),
            pl.BlockSpec((tm, d), lambda s: (nxt(s), 0)),
            pl.BlockSpec((1, 1, d), lambda s: (0, 0, 0)),
            pl.BlockSpec((1, 1, d), lambda s: (0, 0, 1)),
            pl.BlockSpec((1, 1, d), lambda s: (nxt(s) // per_b, 0, 0)),
            pl.BlockSpec((1, 1, d), lambda s: (nxt(s) // per_b, 0, 1)),
            vec,
            pl.BlockSpec(w_in_b.shape, lambda s: (0, 0), pipeline_mode=pl.Buffered(1)),
            pl.BlockSpec((tm, HALF), lambda s: (s % per_b, 0)),
            pl.BlockSpec((tm, HALF), lambda s: (s % per_b, 0)),
            vec, vec, vec,
        ],
        out_specs=[head_spec] * 4 + [flat_spec] * 5,
        out_shape=[head_out] * 4 + [flat_out] * 5,
        scratch_shapes=[pltpu.VMEM((tm, d), BF16)],
        compiler_params=pltpu.CompilerParams(
            dimension_semantics=("arbitrary",), vmem_limit_bytes=VMEM_LIMIT),
        name="proj",
    )(x2, x2, mod_x, mod_x, mod_x, mod_x, ng, w_in_b, cos, sin, ln_g, ln_b, ret_g)


def _dot_tn(a, b):
    return lax.dot_general(a, b, (((0,), (0,)), ((), ())), preferred_element_type=F32)


def _dot_nt(a, b):
    return lax.dot_general(a, b, (((1,), (1,)), ((), ())), preferred_element_type=F32)


def _ret_kernel(decf_ref, decb_ref, q_ref, k_ref, v_ref, za_ref, kc_ref, vc_ref, o_ref,
                sf_ref, tf_ref, tb_ref, dmat_ref, kdf_ref, kdb_ref, qdf_ref, qdb_ref, cwf_ref, cwb_ref,
                *, n_chunks, n_ctx):
    C = RET_CHUNK
    heads = range(RET_HEADS_PER_STEP)

    def log_decay(dec_ref, hh):
        x = dec_ref[...]
        ls = jnp.minimum(x, 0.0) - jnp.log1p(jnp.exp(-jnp.abs(x)))
        lane = lax.broadcasted_iota(jnp.int32, x.shape, 1)
        pick = lane == pl.program_id(0) * RET_HEADS_PER_STEP + hh
        return jnp.sum(jnp.where(pick, ls, 0.0), axis=1, keepdims=True)

    lgf = [log_decay(decf_ref, hh) for hh in heads]
    lgb = [log_decay(decb_ref, hh) for hh in heads]

    @pl.when(pl.program_id(1) == 0)
    def _():
        ii = lax.broadcasted_iota(jnp.int32, (C, C), 0)
        jj = lax.broadcasted_iota(jnp.int32, (C, C), 1)
        diff = (ii - jj).astype(F32)
        row = lax.broadcasted_iota(jnp.int32, (C, HEAD_DIM), 0).astype(F32)
        crow = lax.broadcasted_iota(jnp.int32, (n_ctx, HEAD_DIM), 0).astype(F32)
        for hh in heads:
            dmat_ref[hh] = jnp.where(diff >= 0.0,
                                     jnp.exp(lgf[hh] * jnp.maximum(diff, 0.0)),
                                     jnp.exp(lgb[hh] * jnp.maximum(-diff, 0.0))).astype(BF16)
            kdf_ref[hh] = jnp.exp(lgf[hh] * (C - 1.0 - row)).astype(BF16)
            kdb_ref[hh] = jnp.exp(lgb[hh] * row).astype(BF16)
            qdf_ref[hh] = jnp.exp(lgf[hh] * (row + 1.0)).astype(BF16)
            qdb_ref[hh] = jnp.exp(lgb[hh] * (C - row)).astype(BF16)
            cwf_ref[hh] = jnp.exp(lgf[hh] * (n_ctx - 1.0 - crow)).astype(BF16)
            cwb_ref[hh] = jnp.exp(lgb[hh] * crow).astype(BF16)

    cdec_f = [jnp.exp(lgf[hh] * C) for hh in heads]
    cdec_b = [jnp.exp(lgb[hh] * C) for hh in heads]

    for hh in heads:
        kc = kc_ref[0, hh]
        vc = vc_ref[0, hh]
        sf_ref[hh] = _dot_tn(kc * cwf_ref[hh], vc)
        tf_ref[hh] = _dot_tn(kc * cwb_ref[hh], vc)

    for c in range(n_chunks - 1, -1, -1):
        sl = slice(c * C, (c + 1) * C)
        for hh in heads:
            tb_ref[hh, c] = tf_ref[hh].astype(BF16)
            if c > 0:
                tf_ref[hh] = cdec_b[hh] * tf_ref[hh] + _dot_tn(k_ref[0, hh, sl, :] * kdb_ref[hh],
                                                               v_ref[0, hh, sl, :])

    for c in range(n_chunks):
        sl = slice(c * C, (c + 1) * C)
        for hh in heads:
            qq = q_ref[0, hh, sl, :]
            kk = k_ref[0, hh, sl, :]
            vv = v_ref[0, hh, sl, :]
            p = _dot_nt(qq, kk).astype(BF16) * dmat_ref[hh]
            lhs = jnp.concatenate([p, qq * qdf_ref[hh], qq * qdb_ref[hh]], axis=1)
            rhs = jnp.concatenate([vv, sf_ref[hh].astype(BF16), tb_ref[hh, c]], axis=0)
            o = jnp.dot(lhs, rhs, preferred_element_type=F32)
            if c + 1 < n_chunks:
                sf_ref[hh] = cdec_f[hh] * sf_ref[hh] + _dot_tn(kk * kdf_ref[hh], vv)
            mu = jnp.mean(o, axis=-1, keepdims=True)
            oc = o - mu
            rstd = lax.rsqrt(jnp.mean(oc * oc, axis=-1, keepdims=True) + EPS)
            o_ref[0, hh, sl, :] = (oc * rstd * za_ref[0, hh, sl, :].astype(F32)).astype(BF16)


def _retention(dec_f, dec_b, q, k, v, za, kc, vc):
    b, hh, n, dh = q.shape
    n_ctx = kc.shape[2]
    n_chunks = n // RET_CHUNK
    hp = RET_HEADS_PER_STEP
    seq = pl.BlockSpec((1, hp, n, dh), lambda j, i: (i, j, 0, 0))
    cseq = pl.BlockSpec((1, hp, n_ctx, dh), lambda j, i: (i, j, 0, 0))
    table = pltpu.VMEM((hp, RET_CHUNK, dh), BF16)
    ctable = pltpu.VMEM((hp, n_ctx, dh), BF16)
    return pl.pallas_call(
        functools.partial(_ret_kernel, n_chunks=n_chunks, n_ctx=n_ctx),
        grid=(hh // hp, b),
        in_specs=[
            pl.BlockSpec(dec_f.shape, lambda j, i: (0, 0)),
            pl.BlockSpec(dec_b.shape, lambda j, i: (0, 0)),
            seq, seq, seq, seq, cseq, cseq,
        ],
        out_specs=seq,
        out_shape=jax.ShapeDtypeStruct((b, hh, n, dh), BF16),
        scratch_shapes=[
            pltpu.VMEM((hp, dh, dh), F32),
            pltpu.VMEM((hp, dh, dh), F32),
            pltpu.VMEM((hp, n_chunks, dh, dh), BF16),
            pltpu.VMEM((hp, RET_CHUNK, RET_CHUNK), BF16),
            table, table, table, table, ctable, ctable,
        ],
        compiler_params=pltpu.CompilerParams(
            dimension_semantics=("arbitrary", "arbitrary"), vmem_limit_bytes=VMEM_LIMIT),
        name="ret",
    )(dec_f, dec_b, q, k, v, za, kc, vc)


def _out_kernel(x_ref, ya_ref, u_ref, vg_ref, zb_ref, ga_ref, gb_ref, gt_ref, ws_ref, bias_ref,
                wpa_ref, wpb_ref, wo_ref, fg_ref, o_ref):
    tm = x_ref.shape[1]
    gw = D_MODEL // MLP_GROUPS
    n_sub = OUT_SUB_ROWS // MLP_CHUNK
    for r in range(tm // OUT_SUB_ROWS):
        base = r * OUT_SUB_ROWS
        rows = slice(base, base + OUT_SUB_ROWS)
        mixed = [[None] * MLP_GROUPS for _ in range(n_sub)]
        for g in range(MLP_GROUPS):
            cols = slice(g * gw, (g + 1) * gw)
            rhs = jnp.concatenate(
                [vg_ref[0, base + c * MLP_CHUNK:base + (c + 1) * MLP_CHUNK, cols]
                 for c in range(n_sub)], axis=1)
            m = jnp.dot(ws_ref[g], rhs, preferred_element_type=F32)
            for c in range(n_sub):
                mixed[c][g] = m[:, c * gw:(c + 1) * gw] + bias_ref[:, cols]
        mix = jnp.concatenate([jnp.concatenate(mixed[c], axis=1) for c in range(n_sub)], axis=0)
        yb = (u_ref[0, rows, :].astype(F32) * mix * zb_ref[0, rows, :].astype(F32)).astype(BF16)
        ya = jnp.concatenate([ya_ref[0, h, rows, :] for h in range(HEADS)], axis=1)
        pa = jnp.dot(ya, _unpack_rows(wpa_ref[...]), preferred_element_type=F32)
        pb = jnp.dot(yb, _unpack_rows(wpb_ref[...]), preferred_element_type=F32)
        merged = (ga_ref[0, rows, :].astype(F32) * pa
                  + gb_ref[0, rows, :].astype(F32) * pb).astype(BF16)
        out = jnp.dot(merged, _unpack_rows(wo_ref[...]), preferred_element_type=F32)
        xo = x_ref[0, rows, :] + gt_ref[0] * out
        ms = jnp.mean(xo * xo, axis=-1, keepdims=True)
        o_ref[0, rows, :] = xo * lax.rsqrt(ms + EPS) * fg_ref[...]


def _output(x, ya, u, vg, zb, ga, gb, mod_x, ws_b, bias_full, wpa_b, wpb_b, wo_b, fg):
    b, n, d = x.shape
    tm = OUT_ROWS
    flat = pl.BlockSpec((1, tm, d), lambda i, j: (i, j, 0))
    const2 = lambda shape: pl.BlockSpec(shape, lambda i, j: (0,) * len(shape))
    return pl.pallas_call(
        _out_kernel,
        grid=(b, n // tm),
        in_specs=[
            flat,
            pl.BlockSpec((1, HEADS, tm, HEAD_DIM), lambda i, j: (i, 0, j, 0)),
            flat, flat, flat, flat, flat,
            pl.BlockSpec((1, 1, d), lambda i, j: (i, 0, 2)),
            const2(ws_b.shape), const2(bias_full.shape),
            const2(wpa_b.shape), const2(wpb_b.shape), const2(wo_b.shape),
            const2((1, d)),
        ],
        out_specs=flat,
        out_shape=jax.ShapeDtypeStruct((b, n, d), F32),
        compiler_params=pltpu.CompilerParams(
            dimension_semantics=("arbitrary", "arbitrary"), vmem_limit_bytes=VMEM_LIMIT),
        name="out",
    )(x, ya, u, vg, zb, ga, gb, mod_x, ws_b, bias_full, wpa_b, wpb_b, wo_b, fg)


def kernel(x, c, ctx, c_ctx, w_mod, b_mod, norm_g, w_in, ret_decay_fwd, ret_decay_bwd, ret_norm_g,
           mlp_ln_g, mlp_ln_b, mlp_ws, mlp_bs, w_proj_a, w_proj_b, w_out, final_norm_g):
    b, n, d = x.shape
    assert d == D_MODEL and n % RET_CHUNK == 0 and n % OUT_ROWS == 0 and n % PROJ_ROWS == 0
    assert w_mod.shape[0] == 1, "single-layer block"

    mod_x, mod_c = _modulation(c, c_ctx[None, :], w_mod[0], b_mod[0][None, :])

    q_scale = jnp.concatenate([jnp.full((d,), HEAD_DIM ** -0.5, F32),
                               jnp.ones((w_in.shape[2] - d,), F32)])
    (w_in_b,) = _pack_rows([w_in[0]], q_scale[None, :], PACK_COLS)
    wpa_b, wpb_b, wo_b = _pack_rows([w_proj_a[0], w_proj_b[0], w_out[0]], jnp.ones((1, d), F32),
                                    PACK_COLS // 2)
    ng = norm_g[0][None, :]
    cos, sin = _rope_tables(n)

    kc, vc = _ctx_proj(ctx, mod_c, ng, w_in_b)
    q, k, v, za, u, vg, zb, ga, gb = _proj(
        x, mod_x, ng, w_in_b, jnp.asarray(cos).astype(BF16), jnp.asarray(sin).astype(BF16),
        mlp_ln_g[0][None, :], mlp_ln_b[0][None, :], ret_norm_g[0][None, :])
    ya = _retention(ret_decay_fwd.astype(F32), ret_decay_bwd.astype(F32), q, k, v, za, kc, vc)

    bias_full = jnp.repeat(mlp_bs[0].T, d // MLP_GROUPS, axis=1)
    return _output(x, ya, u, vg, zb, ga, gb, mod_x, mlp_ws[0].astype(BF16), bias_full,
                   wpa_b, wpb_b, wo_b,
                   final_norm_g[None, :])
```

```python
import functools

import numpy as np
import jax
import jax.numpy as jnp
from jax import lax
from jax.experimental import pallas as pl
from jax.experimental.pallas import tpu as pltpu

F32 = jnp.float32
BF16 = jnp.bfloat16

D_MODEL = 1024
HEADS = 4
HEAD_DIM = 256
HALF = HEAD_DIM // 2
GRID_W = 64
MLP_GROUPS = 8
MLP_CHUNK = 128
ROPE_BASE = 10000.0
EPS = 1e-6
N_COL_GROUPS = 9

RET_CHUNK = 256
RET_HEADS_PER_STEP = 2
PROJ_ROWS = 512
PROJ_SUB_ROWS = 256
MOD_COLS = 1024
OUT_ROWS = 512
OUT_SUB_ROWS = 256
VMEM_LIMIT = 56 * 1024 * 1024


def _rope_tables(n):
    pos = np.arange(n)
    row = (pos // GRID_W).astype(np.float64)
    col = (pos % GRID_W).astype(np.float64)
    quarter = HALF // 2
    inv = np.power(ROPE_BASE, -np.arange(quarter, dtype=np.float64) / quarter)
    ang = np.concatenate([row[:, None] * inv, col[:, None] * inv], axis=-1)
    return np.cos(ang).astype(np.float32), np.sin(ang).astype(np.float32)


def _silu(x):
    return x * jax.nn.sigmoid(x)


def _pack_rows(w):
    return pltpu.bitcast(w.astype(BF16), jnp.uint32)


def _unpack_rows(w32):
    return pltpu.bitcast(w32, BF16)


def _mod_kernel(c_ref, cc_ref, w_ref, b_ref, ox_ref, oc_ref):
    w = w_ref[...].astype(BF16)
    mx = jnp.dot(_silu(c_ref[...]).astype(BF16), w, preferred_element_type=F32) + b_ref[...]
    for i in range(ox_ref.shape[0]):
        ox_ref[i] = mx[i:i + 1, :]
    oc_ref[...] = jnp.dot(_silu(cc_ref[...]).astype(BF16), w, preferred_element_type=F32) + b_ref[...]


def _modulation(c, c_ctx, w_mod, b_mod):
    rows, d = c.shape
    width = w_mod.shape[1]
    bn = MOD_COLS
    return pl.pallas_call(
        _mod_kernel,
        grid=(width // bn,),
        in_specs=[
            pl.BlockSpec((rows, d), lambda j: (0, 0)),
            pl.BlockSpec((1, d), lambda j: (0, 0)),
            pl.BlockSpec((d, bn), lambda j: (0, j)),
            pl.BlockSpec((1, bn), lambda j: (0, j)),
        ],
        out_specs=[pl.BlockSpec((rows, 1, bn), lambda j: (0, 0, j)),
                   pl.BlockSpec((1, bn), lambda j: (0, j))],
        out_shape=[jax.ShapeDtypeStruct((rows, 1, width), F32),
                   jax.ShapeDtypeStruct((1, width), F32)],
        name="mod",
    )(c, c_ctx, w_mod, b_mod)


def _modulated_norm(x, ng, sh, sc):
    ms = jnp.mean(x * x, axis=-1, keepdims=True)
    return x * lax.rsqrt(ms + EPS) * (ng * (1.0 + sc)) + sh


def _ctx_proj_kernel(x_ref, sh_ref, sc_ref, ng_ref, wk_ref, wv_ref, k_ref, v_ref, wkp_ref, wvp_ref):
    @pl.when(pl.program_id(0) == 0)
    def _():
        wkp_ref[...] = _pack_rows(wk_ref[...])
        wvp_ref[...] = _pack_rows(wv_ref[...])

    hb = _modulated_norm(x_ref[0], ng_ref[...], sh_ref[...], sc_ref[...]).astype(BF16)
    pk = jnp.dot(hb, _unpack_rows(wkp_ref[...]), preferred_element_type=F32)
    pv = jnp.dot(hb, _unpack_rows(wvp_ref[...]), preferred_element_type=F32)
    for h in range(HEADS):
        k_ref[0, h] = pk[:, h * HEAD_DIM:(h + 1) * HEAD_DIM].astype(BF16)
        v_ref[0, h] = pv[:, h * HEAD_DIM:(h + 1) * HEAD_DIM].astype(BF16)


def _ctx_proj(ctx, mod_c, ng, w_in):
    b, n, d = ctx.shape
    head_out = jax.ShapeDtypeStruct((b, HEADS, n, HEAD_DIM), BF16)
    head_spec = pl.BlockSpec((1, HEADS, n, HEAD_DIM), lambda i: (i, 0, 0, 0))
    return pl.pallas_call(
        _ctx_proj_kernel,
        grid=(b,),
        in_specs=[
            pl.BlockSpec((1, n, d), lambda i: (i, 0, 0)),
            pl.BlockSpec((1, d), lambda i: (0, 0)),
            pl.BlockSpec((1, d), lambda i: (0, 1)),
            pl.BlockSpec((1, d), lambda i: (0, 0)),
            pl.BlockSpec((d, d), lambda i: (0, 1), pipeline_mode=pl.Buffered(1)),
            pl.BlockSpec((d, d), lambda i: (0, 2), pipeline_mode=pl.Buffered(1)),
        ],
        out_specs=[head_spec, head_spec],
        out_shape=[head_out, head_out],
        scratch_shapes=[pltpu.VMEM((d // 2, d), jnp.uint32), pltpu.VMEM((d // 2, d), jnp.uint32)],
        compiler_params=pltpu.CompilerParams(
            dimension_semantics=("arbitrary",), vmem_limit_bytes=VMEM_LIMIT),
        name="ctx_proj",
    )(ctx, mod_c, mod_c, ng, w_in, w_in)


def _proj_kernel(x0_ref, xn_ref, sh0_ref, sc0_ref, shn_ref, scn_ref, ng_ref, w_ref, cos_ref, sin_ref,
                 lng_ref, lnb_ref, rg_ref,
                 q_ref, k_ref, v_ref, za_ref, u_ref, vg_ref, zb_ref, ga_ref, gb_ref, hb_ref, wp_ref):
    step = pl.program_id(0)

    @pl.when(step < N_COL_GROUPS)
    def _():
        scale = jnp.where(step == 0, HEAD_DIM ** -0.5, 1.0)
        wp_ref[step] = _pack_rows(w_ref[...] * scale)

    @pl.when(step >= N_COL_GROUPS)
    def _():
        _proj_rows(step == N_COL_GROUPS, x0_ref, xn_ref, sh0_ref, sc0_ref, shn_ref, scn_ref, ng_ref,
                   wp_ref, cos_ref, sin_ref, lng_ref, lnb_ref, rg_ref,
                   q_ref, k_ref, v_ref, za_ref, u_ref, vg_ref, zb_ref, ga_ref, gb_ref, hb_ref)


def _proj_rows(first, x0_ref, xn_ref, sh0_ref, sc0_ref, shn_ref, scn_ref, ng_ref, wp_ref, cos_ref, sin_ref,
               lng_ref, lnb_ref, rg_ref,
               q_ref, k_ref, v_ref, za_ref, u_ref, vg_ref, zb_ref, ga_ref, gb_ref, hb_ref):
    @pl.when(first)
    def _():
        hb_ref[...] = _modulated_norm(x0_ref[...], ng_ref[...], sh0_ref[0], sc0_ref[0]).astype(BF16)

    hb_next = _modulated_norm(xn_ref[...], ng_ref[...], shn_ref[0], scn_ref[0]).astype(BF16)

    for r in range(PROJ_ROWS // PROJ_SUB_ROWS):
        rows = slice(r * PROJ_SUB_ROWS, (r + 1) * PROJ_SUB_ROWS)
        cos = cos_ref[rows, :]
        sin = sin_ref[rows, :]

        def group(g):
            return jnp.dot(hb_ref[rows, :], _unpack_rows(wp_ref[g]), preferred_element_type=F32)

        def store_rope(p, o_ref, transposed=False):
            pb = p.astype(BF16)
            for h in range(HEADS):
                t1 = pb[:, h * HEAD_DIM:h * HEAD_DIM + HALF]
                t2 = pb[:, h * HEAD_DIM + HALF:(h + 1) * HEAD_DIM]
                if transposed:
                    o_ref[0, h, :HALF, rows] = (t1 * cos - t2 * sin).T
                    o_ref[0, h, HALF:, rows] = (t1 * sin + t2 * cos).T
                else:
                    o_ref[0, h, rows, :HALF] = t1 * cos - t2 * sin
                    o_ref[0, h, rows, HALF:] = t1 * sin + t2 * cos

        def store_heads(pb, o_ref):
            for h in range(HEADS):
                o_ref[0, h, rows, :] = pb[:, h * HEAD_DIM:(h + 1) * HEAD_DIM]

        store_rope(group(0), q_ref)
        store_heads(_silu(group(3)).astype(BF16) * rg_ref[...].astype(BF16), za_ref)
        u_ref[0, rows, :] = jax.nn.gelu(group(4)).astype(BF16)
        ga_ref[0, rows, :] = jax.nn.sigmoid(group(7)).astype(BF16)
        t = jax.nn.gelu(group(5))
        mu = jnp.mean(t, axis=-1, keepdims=True)
        tc = t - mu
        tn = tc * lax.rsqrt(jnp.mean(tc * tc, axis=-1, keepdims=True) + EPS)
        vg_ref[0, rows, :] = (tn.astype(BF16) * lng_ref[...].astype(BF16)
                              + lnb_ref[...].astype(BF16))
        gb_ref[0, rows, :] = jax.nn.sigmoid(group(8)).astype(BF16)
        store_rope(group(1), k_ref, transposed=True)
        zb_ref[0, rows, :] = _silu(group(6)).astype(BF16)
        store_heads(group(2).astype(BF16), v_ref)

    hb_ref[...] = hb_next


def _proj(x, mod_x, ng, w_in, cos, sin, ln_g, ln_b, ret_g):
    b, n, d = x.shape
    tm = PROJ_ROWS
    per_b = n // tm
    steps = b * per_b
    pre = N_COL_GROUPS
    x2 = x.reshape(b * n, d)
    row = lambda s: jnp.maximum(s - pre, 0)
    nxt = lambda s: jnp.minimum(row(s) + 1, steps - 1)
    head_out = jax.ShapeDtypeStruct((b, HEADS, n, HEAD_DIM), BF16)
    headt_out = jax.ShapeDtypeStruct((b, HEADS, HEAD_DIM, n), BF16)
    flat_out = jax.ShapeDtypeStruct((b, n, d), BF16)
    head_spec = pl.BlockSpec((1, HEADS, tm, HEAD_DIM), lambda s: (row(s) // per_b, 0, row(s) % per_b, 0))
    headt_spec = pl.BlockSpec((1, HEADS, HEAD_DIM, tm), lambda s: (row(s) // per_b, 0, 0, row(s) % per_b))
    flat_spec = pl.BlockSpec((1, tm, d), lambda s: (row(s) // per_b, row(s) % per_b, 0))
    vec = pl.BlockSpec((1, d), lambda s: (0, 0))
    return pl.pallas_call(
        _proj_kernel,
        grid=(pre + steps,),
        in_specs=[
            pl.BlockSpec((tm, d), lambda s: (0, 0)),
            pl.BlockSpec((tm, d), lambda s: (nxt(s), 0)),
            pl.BlockSpec((1, 1, d), lambda s: (0, 0, 0)),
            pl.BlockSpec((1, 1, d), lambda s: (0, 0, 1)),
            pl.BlockSpec((1, 1, d), lambda s: (nxt(s) // per_b, 0, 0)),
            pl.BlockSpec((1, 1, d), lambda s: (nxt(s) // per_b, 0, 1)),
            vec,
            pl.BlockSpec((d, d), lambda s: (0, jnp.minimum(s, pre - 1))),
            pl.BlockSpec((tm, HALF), lambda s: (row(s) % per_b, 0)),
            pl.BlockSpec((tm, HALF), lambda s: (row(s) % per_b, 0)),
            vec, vec, vec,
        ],
        out_specs=[head_spec, headt_spec, head_spec, head_spec] + [flat_spec] * 5,
        out_shape=[head_out, headt_out, head_out, head_out] + [flat_out] * 5,
        scratch_shapes=[pltpu.VMEM((tm, d), BF16),
                        pltpu.VMEM((N_COL_GROUPS, d // 2, d), jnp.uint32)],
        compiler_params=pltpu.CompilerParams(
            dimension_semantics=("arbitrary",), vmem_limit_bytes=VMEM_LIMIT),
        name="proj",
    )(x2, x2, mod_x, mod_x, mod_x, mod_x, ng, w_in, cos, sin, ln_g, ln_b, ret_g)


def _dot_tn(a, b):
    return lax.dot_general(a, b, (((0,), (0,)), ((), ())), preferred_element_type=F32)


def _ret_kernel(decf_ref, decb_ref, q_ref, kt_ref, v_ref, za_ref, kc_ref, vc_ref, o_ref,
                sf_ref, tf_ref, tb_ref, dmat_ref, kdf_ref, kdb_ref, qdf_ref, qdb_ref, cwf_ref, cwb_ref,
                *, n_chunks, n_ctx):
    C = RET_CHUNK
    heads = range(RET_HEADS_PER_STEP)

    def log_decay(dec_ref, hh):
        x = dec_ref[...]
        ls = jnp.minimum(x, 0.0) - jnp.log1p(jnp.exp(-jnp.abs(x)))
        lane = lax.broadcasted_iota(jnp.int32, x.shape, 1)
        pick = lane == pl.program_id(0) * RET_HEADS_PER_STEP + hh
        return jnp.sum(jnp.where(pick, ls, 0.0), axis=1, keepdims=True)

    lgf = [log_decay(decf_ref, hh) for hh in heads]
    lgb = [log_decay(decb_ref, hh) for hh in heads]

    @pl.when(pl.program_id(1) == 0)
    def _():
        ii = lax.broadcasted_iota(jnp.int32, (C, C), 0)
        jj = lax.broadcasted_iota(jnp.int32, (C, C), 1)
        diff = (ii - jj).astype(F32)
        row = lax.broadcasted_iota(jnp.int32, (C, HEAD_DIM), 0).astype(F32)
        col = lax.broadcasted_iota(jnp.int32, (HEAD_DIM, C), 1).astype(F32)
        crow = lax.broadcasted_iota(jnp.int32, (n_ctx, HEAD_DIM), 0).astype(F32)
        for hh in heads:
            dmat_ref[hh] = jnp.where(diff >= 0.0,
                                     jnp.exp(lgf[hh] * jnp.maximum(diff, 0.0)),
                                     jnp.exp(lgb[hh] * jnp.maximum(-diff, 0.0))).astype(BF16)
            kdf_ref[hh] = jnp.exp(lgf[hh] * (C - 1.0 - col)).astype(BF16)
            kdb_ref[hh] = jnp.exp(lgb[hh] * col).astype(BF16)
            qdf_ref[hh] = jnp.exp(lgf[hh] * (row + 1.0)).astype(BF16)
            qdb_ref[hh] = jnp.exp(lgb[hh] * (C - row)).astype(BF16)
            cwf_ref[hh] = jnp.exp(lgf[hh] * (n_ctx - 1.0 - crow)).astype(BF16)
            cwb_ref[hh] = jnp.exp(lgb[hh] * crow).astype(BF16)

    cdec_f = [jnp.exp(lgf[hh] * C) for hh in heads]
    cdec_b = [jnp.exp(lgb[hh] * C) for hh in heads]

    for hh in heads:
        kc = kc_ref[0, hh]
        vc = vc_ref[0, hh]
        sf_ref[hh] = _dot_tn(kc * cwf_ref[hh], vc)
        tf_ref[hh] = _dot_tn(kc * cwb_ref[hh], vc)

    for c in range(n_chunks - 1, -1, -1):
        sl = slice(c * C, (c + 1) * C)
        for hh in heads:
            tb_ref[hh, c] = tf_ref[hh].astype(BF16)
            if c > 0:
                tf_ref[hh] = cdec_b[hh] * tf_ref[hh] + jnp.dot(
                    kt_ref[0, hh, :, sl] * kdb_ref[hh], v_ref[0, hh, sl, :], preferred_element_type=F32)

    for c in range(n_chunks):
        sl = slice(c * C, (c + 1) * C)
        for hh in heads:
            qq = q_ref[0, hh, sl, :]
            kt = kt_ref[0, hh, :, sl]
            vv = v_ref[0, hh, sl, :]
            p = jnp.dot(qq, kt, preferred_element_type=F32).astype(BF16) * dmat_ref[hh]
            lhs = jnp.concatenate([p, qq * qdf_ref[hh], qq * qdb_ref[hh]], axis=1)
            rhs = jnp.concatenate([vv, sf_ref[hh].astype(BF16), tb_ref[hh, c]], axis=0)
            o = jnp.dot(lhs, rhs, preferred_element_type=F32)
            if c + 1 < n_chunks:
                sf_ref[hh] = cdec_f[hh] * sf_ref[hh] + jnp.dot(kt * kdf_ref[hh], vv,
                                                               preferred_element_type=F32)
            mu = jnp.mean(o, axis=-1, keepdims=True)
            oc = o - mu
            rstd = lax.rsqrt(jnp.mean(oc * oc, axis=-1, keepdims=True) + EPS)
            o_ref[0, hh, sl, :] = (oc * rstd * za_ref[0, hh, sl, :].astype(F32)).astype(BF16)


def _retention(dec_f, dec_b, q, kt, v, za, kc, vc):
    b, hh, n, dh = q.shape
    n_ctx = kc.shape[2]
    n_chunks = n // RET_CHUNK
    hp = RET_HEADS_PER_STEP
    seq = pl.BlockSpec((1, hp, n, dh), lambda j, i: (i, j, 0, 0))
    seqt = pl.BlockSpec((1, hp, dh, n), lambda j, i: (i, j, 0, 0))
    cseq = pl.BlockSpec((1, hp, n_ctx, dh), lambda j, i: (i, j, 0, 0))
    table = pltpu.VMEM((hp, RET_CHUNK, dh), BF16)
    tablet = pltpu.VMEM((hp, dh, RET_CHUNK), BF16)
    ctable = pltpu.VMEM((hp, n_ctx, dh), BF16)
    return pl.pallas_call(
        functools.partial(_ret_kernel, n_chunks=n_chunks, n_ctx=n_ctx),
        grid=(hh // hp, b),
        in_specs=[
            pl.BlockSpec(dec_f.shape, lambda j, i: (0, 0)),
            pl.BlockSpec(dec_b.shape, lambda j, i: (0, 0)),
            seq, seqt, seq, seq, cseq, cseq,
        ],
        out_specs=seq,
        out_shape=jax.ShapeDtypeStruct((b, hh, n, dh), BF16),
        scratch_shapes=[
            pltpu.VMEM((hp, dh, dh), F32),
            pltpu.VMEM((hp, dh, dh), F32),
            pltpu.VMEM((hp, n_chunks, dh, dh), BF16),
            pltpu.VMEM((hp, RET_CHUNK, RET_CHUNK), BF16),
            tablet, tablet, table, table, ctable, ctable,
        ],
        compiler_params=pltpu.CompilerParams(
            dimension_semantics=("arbitrary", "arbitrary"), vmem_limit_bytes=VMEM_LIMIT),
        name="ret",
    )(dec_f, dec_b, q, kt, v, za, kc, vc)


def _out_kernel(x_ref, ya_ref, u_ref, vg_ref, zb_ref, ga_ref, gb_ref, gt_ref, ws_ref, bias_ref,
                wpa_ref, wpb_ref, wo_ref, fg_ref, o_ref, wpa_p, wpb_p, wo_p):
    @pl.when((pl.program_id(0) == 0) & (pl.program_id(1) == 0))
    def _():
        wpa_p[...] = _pack_rows(wpa_ref[...])
        wpb_p[...] = _pack_rows(wpb_ref[...])
        wo_p[...] = _pack_rows(wo_ref[...])

    tm = x_ref.shape[1]
    gw = D_MODEL // MLP_GROUPS
    n_sub = OUT_SUB_ROWS // MLP_CHUNK
    for r in range(tm // OUT_SUB_ROWS):
        base = r * OUT_SUB_ROWS
        rows = slice(base, base + OUT_SUB_ROWS)
        mixed = [[None] * MLP_GROUPS for _ in range(n_sub)]
        for g in range(MLP_GROUPS):
            cols = slice(g * gw, (g + 1) * gw)
            rhs = jnp.concatenate(
                [vg_ref[0, base + c * MLP_CHUNK:base + (c + 1) * MLP_CHUNK, cols]
                 for c in range(n_sub)], axis=1)
            m = jnp.dot(ws_ref[g], rhs, preferred_element_type=F32)
            for c in range(n_sub):
                mixed[c][g] = m[:, c * gw:(c + 1) * gw] + bias_ref[:, cols]
        mix = jnp.concatenate([jnp.concatenate(mixed[c], axis=1) for c in range(n_sub)], axis=0)
        yb = (u_ref[0, rows, :].astype(F32) * mix * zb_ref[0, rows, :].astype(F32)).astype(BF16)
        ya = jnp.concatenate([ya_ref[0, h, rows, :] for h in range(HEADS)], axis=1)
        pa = jnp.dot(ya, _unpack_rows(wpa_p[...]), preferred_element_type=F32)
        pb = jnp.dot(yb, _unpack_rows(wpb_p[...]), preferred_element_type=F32)
        merged = (ga_ref[0, rows, :].astype(F32) * pa
                  + gb_ref[0, rows, :].astype(F32) * pb).astype(BF16)
        out = jnp.dot(merged, _unpack_rows(wo_p[...]), preferred_element_type=F32)
        xo = x_ref[0, rows, :] + gt_ref[0] * out
        ms = jnp.mean(xo * xo, axis=-1, keepdims=True)
        o_ref[0, rows, :] = xo * lax.rsqrt(ms + EPS) * fg_ref[...]


def _output(x, ya, u, vg, zb, ga, gb, mod_x, ws_b, bias_full, wpa, wpb, wo, fg):
    b, n, d = x.shape
    tm = OUT_ROWS
    flat = pl.BlockSpec((1, tm, d), lambda i, j: (i, j, 0))
    const2 = lambda shape: pl.BlockSpec(shape, lambda i, j: (0,) * len(shape))
    once = lambda shape: pl.BlockSpec(shape, lambda i, j: (0,) * len(shape),
                                      pipeline_mode=pl.Buffered(1))
    return pl.pallas_call(
        _out_kernel,
        grid=(b, n // tm),
        in_specs=[
            flat,
            pl.BlockSpec((1, HEADS, tm, HEAD_DIM), lambda i, j: (i, 0, j, 0)),
            flat, flat, flat, flat, flat,
            pl.BlockSpec((1, 1, d), lambda i, j: (i, 0, 2)),
            const2(ws_b.shape), const2(bias_full.shape),
            once(wpa.shape), once(wpb.shape), once(wo.shape),
            const2((1, d)),
        ],
        out_specs=flat,
        out_shape=jax.ShapeDtypeStruct((b, n, d), F32),
        scratch_shapes=[pltpu.VMEM((d // 2, d), jnp.uint32)] * 3,
        compiler_params=pltpu.CompilerParams(
            dimension_semantics=("arbitrary", "arbitrary"), vmem_limit_bytes=VMEM_LIMIT),
        name="out",
    )(x, ya, u, vg, zb, ga, gb, mod_x, ws_b, bias_full, wpa, wpb, wo, fg)


def kernel(x, c, ctx, c_ctx, w_mod, b_mod, norm_g, w_in, ret_decay_fwd, ret_decay_bwd, ret_norm_g,
           mlp_ln_g, mlp_ln_b, mlp_ws, mlp_bs, w_proj_a, w_proj_b, w_out, final_norm_g):
    b, n, d = x.shape
    assert d == D_MODEL and n % RET_CHUNK == 0 and n % OUT_ROWS == 0 and n % PROJ_ROWS == 0
    assert w_mod.shape[0] == 1, "single-layer block"

    mod_x, mod_c = _modulation(c, c_ctx[None, :], w_mod[0], b_mod[0][None, :])

    ng = norm_g[0][None, :]
    cos, sin = _rope_tables(n)
    w_in2 = w_in.reshape(w_in.shape[1:])

    kc, vc = _ctx_proj(ctx, mod_c, ng, w_in2)
    q, kt, v, za, u, vg, zb, ga, gb = _proj(
        x, mod_x, ng, w_in2, jnp.asarray(cos).astype(BF16), jnp.asarray(sin).astype(BF16),
        mlp_ln_g[0][None, :], mlp_ln_b[0][None, :], ret_norm_g[0][None, :])
    ya = _retention(ret_decay_fwd.astype(F32), ret_decay_bwd.astype(F32), q, kt, v, za, kc, vc)

    bias_full = jnp.repeat(mlp_bs[0].T, d // MLP_GROUPS, axis=1)
    return _output(x, ya, u, vg, zb, ga, gb, mod_x, mlp_ws[0].astype(BF16), bias_full,
                   w_proj_a.reshape(d, d), w_proj_b.reshape(d, d), w_out.reshape(d, d),
                   final_norm_g[None, :])
```

```python
import functools

import numpy as np
import jax
import jax.numpy as jnp
from jax import lax
from jax.experimental import pallas as pl
from jax.experimental.pallas import tpu as pltpu

F32 = jnp.float32
BF16 = jnp.bfloat16

D_MODEL = 1024
HEADS = 4
HEAD_DIM = 256
HALF = HEAD_DIM // 2
GRID_W = 64
MLP_GROUPS = 8
MLP_CHUNK = 128
ROPE_BASE = 10000.0
EPS = 1e-6
N_COL_GROUPS = 9

RET_CHUNK = 256
RET_HEADS_PER_STEP = 2
PROJ_ROWS = 512
PROJ_SUB_ROWS = 256
CTX_FIRST_STEP = 2
PROJ_PROLOGUE_STEPS = 10
MOD_COLS = 1024
OUT_ROWS = 512
OUT_SUB_ROWS = 256
VMEM_LIMIT = 60 * 1024 * 1024


def _rope_tables(n):
    pos = np.arange(n)
    row = (pos // GRID_W).astype(np.float64)
    col = (pos % GRID_W).astype(np.float64)
    quarter = HALF // 2
    inv = np.power(ROPE_BASE, -np.arange(quarter, dtype=np.float64) / quarter)
    ang = np.concatenate([row[:, None] * inv, col[:, None] * inv], axis=-1)
    return np.cos(ang).astype(np.float32), np.sin(ang).astype(np.float32)


def _silu(x):
    return x * jax.nn.sigmoid(x)


def _pack_rows(w):
    return pltpu.bitcast(w.astype(BF16), jnp.uint32)


def _unpack_rows(w32):
    return pltpu.bitcast(w32, BF16)


def _mod_kernel(c_ref, cc_ref, w_ref, b_ref, ox_ref, oc_ref):
    w = w_ref[...].astype(BF16)
    mx = jnp.dot(_silu(c_ref[...]).astype(BF16), w, preferred_element_type=F32) + b_ref[...]
    for i in range(ox_ref.shape[0]):
        ox_ref[i] = mx[i:i + 1, :]
    oc_ref[...] = jnp.dot(_silu(cc_ref[...]).astype(BF16), w, preferred_element_type=F32) + b_ref[...]


def _modulation(c, c_ctx, w_mod, b_mod):
    rows, d = c.shape
    width = w_mod.shape[1]
    bn = MOD_COLS
    return pl.pallas_call(
        _mod_kernel,
        grid=(width // bn,),
        in_specs=[
            pl.BlockSpec((rows, d), lambda j: (0, 0)),
            pl.BlockSpec((1, d), lambda j: (0, 0)),
            pl.BlockSpec((d, bn), lambda j: (0, j)),
            pl.BlockSpec((1, bn), lambda j: (0, j)),
        ],
        out_specs=[pl.BlockSpec((rows, 1, bn), lambda j: (0, 0, j)),
                   pl.BlockSpec((1, bn), lambda j: (0, j))],
        out_shape=[jax.ShapeDtypeStruct((rows, 1, width), F32),
                   jax.ShapeDtypeStruct((1, width), F32)],
        name="mod",
    )(c, c_ctx, w_mod, b_mod)


def _modulated_norm(x, ng, sh, sc):
    ms = jnp.mean(x * x, axis=-1, keepdims=True)
    return x * lax.rsqrt(ms + EPS) * (ng * (1.0 + sc)) + sh


def _proj_kernel(xn_ref, shn_ref, scn_ref, ng_ref, w_ref, cos_ref, sin_ref,
                 lng_ref, lnb_ref, rg_ref, ctx_ref, shc_ref, scc_ref,
                 q_ref, k_ref, v_ref, za_ref, u_ref, vg_ref, zb_ref, ga_ref, gb_ref, kc_ref, vc_ref,
                 hb_ref, wp_ref):
    step = pl.program_id(0)

    @pl.when(step < N_COL_GROUPS)
    def _():
        group = _weight_group(step)
        scale = jnp.where(group == 0, HEAD_DIM ** -0.5, 1.0)
        wp_ref[group] = _pack_rows(w_ref[...] * scale)

    @pl.when((step >= CTX_FIRST_STEP) & (step < PROJ_PROLOGUE_STEPS))
    def _():
        hc = _modulated_norm(ctx_ref[0], ng_ref[...], shc_ref[...], scc_ref[...]).astype(BF16)
        pk = jnp.dot(hc, _unpack_rows(wp_ref[1]), preferred_element_type=F32)
        pv = jnp.dot(hc, _unpack_rows(wp_ref[2]), preferred_element_type=F32)
        for h in range(HEADS):
            kc_ref[0, h] = pk[:, h * HEAD_DIM:(h + 1) * HEAD_DIM].astype(BF16)
            vc_ref[0, h] = pv[:, h * HEAD_DIM:(h + 1) * HEAD_DIM].astype(BF16)

    @pl.when(step == PROJ_PROLOGUE_STEPS - 1)
    def _():
        hb_ref[...] = _modulated_norm(xn_ref[...], ng_ref[...], shn_ref[0], scn_ref[0]).astype(BF16)

    @pl.when(step >= PROJ_PROLOGUE_STEPS)
    def _():
        _proj_rows(xn_ref, shn_ref, scn_ref, ng_ref, wp_ref, cos_ref, sin_ref, lng_ref, lnb_ref, rg_ref,
                   q_ref, k_ref, v_ref, za_ref, u_ref, vg_ref, zb_ref, ga_ref, gb_ref, hb_ref)


def _weight_group(step):
    return jnp.where(step < 2, step + 1, jnp.where(step == 2, 0, jnp.minimum(step, N_COL_GROUPS - 1)))


def _proj_rows(xn_ref, shn_ref, scn_ref, ng_ref, wp_ref, cos_ref, sin_ref, lng_ref, lnb_ref, rg_ref,
               q_ref, k_ref, v_ref, za_ref, u_ref, vg_ref, zb_ref, ga_ref, gb_ref, hb_ref):
    hb_next = _modulated_norm(xn_ref[...], ng_ref[...], shn_ref[0], scn_ref[0]).astype(BF16)

    for r in range(PROJ_ROWS // PROJ_SUB_ROWS):
        rows = slice(r * PROJ_SUB_ROWS, (r + 1) * PROJ_SUB_ROWS)
        cos = cos_ref[rows, :]
        sin = sin_ref[rows, :]

        def group(g):
            return jnp.dot(hb_ref[rows, :], _unpack_rows(wp_ref[g]), preferred_element_type=F32)

        def store_rope(p, o_ref, transposed=False):
            pb = p.astype(BF16)
            for h in range(HEADS):
                t1 = pb[:, h * HEAD_DIM:h * HEAD_DIM + HALF]
                t2 = pb[:, h * HEAD_DIM + HALF:(h + 1) * HEAD_DIM]
                if transposed:
                    o_ref[0, h, :HALF, rows] = (t1 * cos - t2 * sin).T
                    o_ref[0, h, HALF:, rows] = (t1 * sin + t2 * cos).T
                else:
                    o_ref[0, h, rows, :HALF] = t1 * cos - t2 * sin
                    o_ref[0, h, rows, HALF:] = t1 * sin + t2 * cos

        def store_heads(pb, o_ref):
            for h in range(HEADS):
                o_ref[0, h, rows, :] = pb[:, h * HEAD_DIM:(h + 1) * HEAD_DIM]

        store_rope(group(0), q_ref)
        store_heads(_silu(group(3)).astype(BF16) * rg_ref[...].astype(BF16), za_ref)
        u_ref[0, rows, :] = jax.nn.gelu(group(4)).astype(BF16)
        ga_ref[0, rows, :] = jax.nn.sigmoid(group(7)).astype(BF16)
        t = jax.nn.gelu(group(5))
        mu = jnp.mean(t, axis=-1, keepdims=True)
        tc = t - mu
        tn = tc * lax.rsqrt(jnp.mean(tc * tc, axis=-1, keepdims=True) + EPS)
        vg_ref[0, rows, :] = (tn.astype(BF16) * lng_ref[...].astype(BF16)
                              + lnb_ref[...].astype(BF16))
        gb_ref[0, rows, :] = jax.nn.sigmoid(group(8)).astype(BF16)
        store_rope(group(1), k_ref, transposed=True)
        zb_ref[0, rows, :] = _silu(group(6)).astype(BF16)
        store_heads(group(2).astype(BF16), v_ref)

    hb_ref[...] = hb_next


def _proj(x, ctx, mod_x, mod_c, ng, w_in, cos, sin, ln_g, ln_b, ret_g):
    b, n, d = x.shape
    tm = PROJ_ROWS
    per_b = n // tm
    steps = b * per_b
    pre = PROJ_PROLOGUE_STEPS
    n_ctx = ctx.shape[1]
    ctx_b = lambda s: jnp.clip(s - CTX_FIRST_STEP, 0, b - 1)
    x2 = x.reshape(b * n, d)
    row = lambda s: jnp.maximum(s - pre, 0)
    nxt = lambda s: jnp.clip(s + 1 - pre, 0, steps - 1)
    head_out = jax.ShapeDtypeStruct((b, HEADS, n, HEAD_DIM), BF16)
    headt_out = jax.ShapeDtypeStruct((b, HEADS, HEAD_DIM, n), BF16)
    flat_out = jax.ShapeDtypeStruct((b, n, d), BF16)
    head_spec = pl.BlockSpec((1, HEADS, tm, HEAD_DIM), lambda s: (row(s) // per_b, 0, row(s) % per_b, 0))
    headt_spec = pl.BlockSpec((1, HEADS, HEAD_DIM, tm), lambda s: (row(s) // per_b, 0, 0, row(s) % per_b))
    flat_spec = pl.BlockSpec((1, tm, d), lambda s: (row(s) // per_b, row(s) % per_b, 0))
    vec = pl.BlockSpec((1, d), lambda s: (0, 0))
    ctx_spec = pl.BlockSpec((1, HEADS, n_ctx, HEAD_DIM), lambda s: (ctx_b(s), 0, 0, 0))
    ctx_out = jax.ShapeDtypeStruct((b, HEADS, n_ctx, HEAD_DIM), BF16)
    assert CTX_FIRST_STEP + b == pre
    return pl.pallas_call(
        _proj_kernel,
        grid=(pre + steps,),
        in_specs=[
            pl.BlockSpec((tm, d), lambda s: (nxt(s), 0)),
            pl.BlockSpec((1, 1, d), lambda s: (nxt(s) // per_b, 0, 0)),
            pl.BlockSpec((1, 1, d), lambda s: (nxt(s) // per_b, 0, 1)),
            vec,
            pl.BlockSpec((d, d), lambda s: (0, _weight_group(s))),
            pl.BlockSpec((tm, HALF), lambda s: (row(s) % per_b, 0)),
            pl.BlockSpec((tm, HALF), lambda s: (row(s) % per_b, 0)),
            vec, vec, vec,
            pl.BlockSpec((1, n_ctx, d), lambda s: (ctx_b(s), 0, 0)),
            pl.BlockSpec((1, d), lambda s: (0, 0)),
            pl.BlockSpec((1, d), lambda s: (0, 1)),
        ],
        out_specs=[head_spec, headt_spec, head_spec, head_spec] + [flat_spec] * 5 + [ctx_spec] * 2,
        out_shape=[head_out, headt_out, head_out, head_out] + [flat_out] * 5 + [ctx_out] * 2,
        scratch_shapes=[pltpu.VMEM((tm, d), BF16),
                        pltpu.VMEM((N_COL_GROUPS, d // 2, d), jnp.uint32)],
        compiler_params=pltpu.CompilerParams(
            dimension_semantics=("arbitrary",), vmem_limit_bytes=VMEM_LIMIT),
        name="proj",
    )(x2, mod_x, mod_x, ng, w_in, cos, sin, ln_g, ln_b, ret_g, ctx, mod_c, mod_c)


def _dot_tn(a, b):
    return lax.dot_general(a, b, (((0,), (0,)), ((), ())), preferred_element_type=F32)


def _ret_kernel(decf_ref, decb_ref, q_ref, kt_ref, v_ref, za_ref, kc_ref, vc_ref, o_ref,
                sf_ref, tf_ref, tb_ref, dmat_ref, kdf_ref, kdb_ref, qdf_ref, qdb_ref, cwf_ref, cwb_ref,
                *, n_chunks, n_ctx):
    C = RET_CHUNK
    heads = range(RET_HEADS_PER_STEP)

    def log_decay(dec_ref, hh):
        x = dec_ref[...]
        ls = jnp.minimum(x, 0.0) - jnp.log1p(jnp.exp(-jnp.abs(x)))
        lane = lax.broadcasted_iota(jnp.int32, x.shape, 1)
        pick = lane == pl.program_id(0) * RET_HEADS_PER_STEP + hh
        return jnp.sum(jnp.where(pick, ls, 0.0), axis=1, keepdims=True)

    lgf = [log_decay(decf_ref, hh) for hh in heads]
    lgb = [log_decay(decb_ref, hh) for hh in heads]

    @pl.when(pl.program_id(1) == 0)
    def _():
        ii = lax.broadcasted_iota(jnp.int32, (C, C), 0)
        jj = lax.broadcasted_iota(jnp.int32, (C, C), 1)
        diff = (ii - jj).astype(F32)
        row = lax.broadcasted_iota(jnp.int32, (C, HEAD_DIM), 0).astype(F32)
        col = lax.broadcasted_iota(jnp.int32, (HEAD_DIM, C), 1).astype(F32)
        crow = lax.broadcasted_iota(jnp.int32, (n_ctx, HEAD_DIM), 0).astype(F32)
        for hh in heads:
            dmat_ref[hh] = jnp.where(diff >= 0.0,
                                     jnp.exp(lgf[hh] * jnp.maximum(diff, 0.0)),
                                     jnp.exp(lgb[hh] * jnp.maximum(-diff, 0.0))).astype(BF16)
            kdf_ref[hh] = jnp.exp(lgf[hh] * (C - 1.0 - col)).astype(BF16)
            kdb_ref[hh] = jnp.exp(lgb[hh] * col).astype(BF16)
            qdf_ref[hh] = jnp.exp(lgf[hh] * (row + 1.0)).astype(BF16)
            qdb_ref[hh] = jnp.exp(lgb[hh] * (C - row)).astype(BF16)
            cwf_ref[hh] = jnp.exp(lgf[hh] * (n_ctx - 1.0 - crow)).astype(BF16)
            cwb_ref[hh] = jnp.exp(lgb[hh] * crow).astype(BF16)

    cdec_f = [jnp.exp(lgf[hh] * C) for hh in heads]
    cdec_b = [jnp.exp(lgb[hh] * C) for hh in heads]

    for hh in heads:
        kc = kc_ref[0, hh]
        vc = vc_ref[0, hh]
        sf_ref[hh] = _dot_tn(kc * cwf_ref[hh], vc)
        tf_ref[hh] = _dot_tn(kc * cwb_ref[hh], vc)

    for c in range(n_chunks - 1, -1, -1):
        sl = slice(c * C, (c + 1) * C)
        for hh in heads:
            tb_ref[hh, c] = tf_ref[hh].astype(BF16)
            if c > 0:
                tf_ref[hh] = cdec_b[hh] * tf_ref[hh] + jnp.dot(
                    kt_ref[0, hh, :, sl] * kdb_ref[hh], v_ref[0, hh, sl, :], preferred_element_type=F32)

    for c in range(n_chunks):
        sl = slice(c * C, (c + 1) * C)
        for hh in heads:
            qq = q_ref[0, hh, sl, :]
            kt = kt_ref[0, hh, :, sl]
            vv = v_ref[0, hh, sl, :]
            p = jnp.dot(qq, kt, preferred_element_type=F32).astype(BF16) * dmat_ref[hh]
            lhs = jnp.concatenate([p, qq * qdf_ref[hh], qq * qdb_ref[hh]], axis=1)
            rhs = jnp.concatenate([vv, sf_ref[hh].astype(BF16), tb_ref[hh, c]], axis=0)
            o = jnp.dot(lhs, rhs, preferred_element_type=F32)
            if c + 1 < n_chunks:
                sf_ref[hh] = cdec_f[hh] * sf_ref[hh] + jnp.dot(kt * kdf_ref[hh], vv,
                                                               preferred_element_type=F32)
            mu = jnp.mean(o, axis=-1, keepdims=True)
            oc = o - mu
            rstd = lax.rsqrt(jnp.mean(oc * oc, axis=-1, keepdims=True) + EPS)
            o_ref[0, hh, sl, :] = (oc * rstd * za_ref[0, hh, sl, :].astype(F32)).astype(BF16)


def _retention(dec_f, dec_b, q, kt, v, za, kc, vc):
    b, hh, n, dh = q.shape
    n_ctx = kc.shape[2]
    n_chunks = n // RET_CHUNK
    hp = RET_HEADS_PER_STEP
    seq = pl.BlockSpec((1, hp, n, dh), lambda j, i: (i, j, 0, 0))
    seqt = pl.BlockSpec((1, hp, dh, n), lambda j, i: (i, j, 0, 0))
    cseq = pl.BlockSpec((1, hp, n_ctx, dh), lambda j, i: (i, j, 0, 0))
    table = pltpu.VMEM((hp, RET_CHUNK, dh), BF16)
    tablet = pltpu.VMEM((hp, dh, RET_CHUNK), BF16)
    ctable = pltpu.VMEM((hp, n_ctx, dh), BF16)
    return pl.pallas_call(
        functools.partial(_ret_kernel, n_chunks=n_chunks, n_ctx=n_ctx),
        grid=(hh // hp, b),
        in_specs=[
            pl.BlockSpec(dec_f.shape, lambda j, i: (0, 0)),
            pl.BlockSpec(dec_b.shape, lambda j, i: (0, 0)),
            seq, seqt, seq, seq, cseq, cseq,
        ],
        out_specs=seq,
        out_shape=jax.ShapeDtypeStruct((b, hh, n, dh), BF16),
        scratch_shapes=[
            pltpu.VMEM((hp, dh, dh), F32),
            pltpu.VMEM((hp, dh, dh), F32),
            pltpu.VMEM((hp, n_chunks, dh, dh), BF16),
            pltpu.VMEM((hp, RET_CHUNK, RET_CHUNK), BF16),
            tablet, tablet, table, table, ctable, ctable,
        ],
        compiler_params=pltpu.CompilerParams(
            dimension_semantics=("arbitrary", "arbitrary"), vmem_limit_bytes=VMEM_LIMIT),
        name="ret",
    )(dec_f, dec_b, q, kt, v, za, kc, vc)


def _out_kernel(x_ref, ya_ref, u_ref, vg_ref, zb_ref, ga_ref, gb_ref, gt_ref, ws_ref, bias_ref,
                wpa_ref, wpb_ref, wo_ref, fg_ref, o_ref, wpa_p, wpb_p, wo_p):
    @pl.when((pl.program_id(0) == 0) & (pl.program_id(1) == 0))
    def _():
        wpa_p[...] = _pack_rows(wpa_ref[...])
        wpb_p[...] = _pack_rows(wpb_ref[...])
        wo_p[...] = _pack_rows(wo_ref[...])

    tm = x_ref.shape[1]
    gw = D_MODEL // MLP_GROUPS
    n_sub = OUT_SUB_ROWS // MLP_CHUNK
    for r in range(tm // OUT_SUB_ROWS):
        base = r * OUT_SUB_ROWS
        rows = slice(base, base + OUT_SUB_ROWS)
        mixed = [[None] * MLP_GROUPS for _ in range(n_sub)]
        for g in range(MLP_GROUPS):
            cols = slice(g * gw, (g + 1) * gw)
            rhs = jnp.concatenate(
                [vg_ref[0, base + c * MLP_CHUNK:base + (c + 1) * MLP_CHUNK, cols]
                 for c in range(n_sub)], axis=1)
            m = jnp.dot(ws_ref[g], rhs, preferred_element_type=F32)
            for c in range(n_sub):
                mixed[c][g] = m[:, c * gw:(c + 1) * gw] + bias_ref[:, cols]
        mix = jnp.concatenate([jnp.concatenate(mixed[c], axis=1) for c in range(n_sub)], axis=0)
        yb = (u_ref[0, rows, :].astype(F32) * mix * zb_ref[0, rows, :].astype(F32)).astype(BF16)
        ya = jnp.concatenate([ya_ref[0, h, rows, :] for h in range(HEADS)], axis=1)
        pa = jnp.dot(ya, _unpack_rows(wpa_p[...]), preferred_element_type=F32)
        pb = jnp.dot(yb, _unpack_rows(wpb_p[...]), preferred_element_type=F32)
        merged = (ga_ref[0, rows, :].astype(F32) * pa
                  + gb_ref[0, rows, :].astype(F32) * pb).astype(BF16)
        out = jnp.dot(merged, _unpack_rows(wo_p[...]), preferred_element_type=F32)
        xo = x_ref[0, rows, :] + gt_ref[0] * out
        ms = jnp.mean(xo * xo, axis=-1, keepdims=True)
        o_ref[0, rows, :] = xo * lax.rsqrt(ms + EPS) * fg_ref[...]


def _output(x, ya, u, vg, zb, ga, gb, mod_x, ws_b, bias_full, wpa, wpb, wo, fg):
    b, n, d = x.shape
    tm = OUT_ROWS
    flat = pl.BlockSpec((1, tm, d), lambda i, j: (i, j, 0))
    const2 = lambda shape: pl.BlockSpec(shape, lambda i, j: (0,) * len(shape))
    once = lambda shape: pl.BlockSpec(shape, lambda i, j: (0,) * len(shape),
                                      pipeline_mode=pl.Buffered(1))
    return pl.pallas_call(
        _out_kernel,
        grid=(b, n // tm),
        in_specs=[
            flat,
            pl.BlockSpec((1, HEADS, tm, HEAD_DIM), lambda i, j: (i, 0, j, 0)),
            flat, flat, flat, flat, flat,
            pl.BlockSpec((1, 1, d), lambda i, j: (i, 0, 2)),
            const2(ws_b.shape), const2(bias_full.shape),
            once(wpa.shape), once(wpb.shape), once(wo.shape),
            const2((1, d)),
        ],
        out_specs=flat,
        out_shape=jax.ShapeDtypeStruct((b, n, d), F32),
        scratch_shapes=[pltpu.VMEM((d // 2, d), jnp.uint32)] * 3,
        compiler_params=pltpu.CompilerParams(
            dimension_semantics=("arbitrary", "arbitrary"), vmem_limit_bytes=VMEM_LIMIT),
        name="out",
    )(x, ya, u, vg, zb, ga, gb, mod_x, ws_b, bias_full, wpa, wpb, wo, fg)


def kernel(x, c, ctx, c_ctx, w_mod, b_mod, norm_g, w_in, ret_decay_fwd, ret_decay_bwd, ret_norm_g,
           mlp_ln_g, mlp_ln_b, mlp_ws, mlp_bs, w_proj_a, w_proj_b, w_out, final_norm_g):
    b, n, d = x.shape
    assert d == D_MODEL and n % RET_CHUNK == 0 and n % OUT_ROWS == 0 and n % PROJ_ROWS == 0
    assert w_mod.shape[0] == 1, "single-layer block"

    mod_x, mod_c = _modulation(c, c_ctx[None, :], w_mod[0], b_mod[0][None, :])

    ng = norm_g[0][None, :]
    cos, sin = _rope_tables(n)
    w_in2 = w_in.reshape(w_in.shape[1:])

    q, kt, v, za, u, vg, zb, ga, gb, kc, vc = _proj(
        x, ctx, mod_x, mod_c, ng, w_in2, jnp.asarray(cos).astype(BF16), jnp.asarray(sin).astype(BF16),
        mlp_ln_g[0][None, :], mlp_ln_b[0][None, :], ret_norm_g[0][None, :])
    ya = _retention(ret_decay_fwd.astype(F32), ret_decay_bwd.astype(F32), q, kt, v, za, kc, vc)

    bias_full = jnp.repeat(mlp_bs[0].T, d // MLP_GROUPS, axis=1)
    return _output(x, ya, u, vg, zb, ga, gb, mod_x, mlp_ws[0].astype(BF16), bias_full,
                   w_proj_a.reshape(d, d), w_proj_b.reshape(d, d), w_out.reshape(d, d),
                   final_norm_g[None, :])
```

```python
import functools

import numpy as np
import jax
import jax.numpy as jnp
from jax import lax
from jax.experimental import pallas as pl
from jax.experimental.pallas import tpu as pltpu

F32 = jnp.float32
BF16 = jnp.bfloat16

D_MODEL = 1024
HEADS = 4
HEAD_DIM = 256
HALF = HEAD_DIM // 2
GRID_W = 64
MLP_GROUPS = 8
MLP_CHUNK = 128
ROPE_BASE = 10000.0
EPS = 1e-6
N_COL_GROUPS = 9

RET_CHUNK = 256
RET_HEADS_PER_STEP = 2
PROJ_ROWS = 512
PROJ_SUB_ROWS = 256
CTX_FIRST_STEP = 2
PROJ_PROLOGUE_STEPS = 10
MOD_COLS = 3072
OUT_ROWS = 512
OUT_SUB_ROWS = 256
VMEM_LIMIT = 60 * 1024 * 1024


def _rope_tables(n):
    pos = np.arange(n)
    row = (pos // GRID_W).astype(np.float64)
    col = (pos % GRID_W).astype(np.float64)
    quarter = HALF // 2
    inv = np.power(ROPE_BASE, -np.arange(quarter, dtype=np.float64) / quarter)
    ang = np.concatenate([row[:, None] * inv, col[:, None] * inv], axis=-1)
    return np.cos(ang).astype(np.float32), np.sin(ang).astype(np.float32)


def _silu(x):
    return x * jax.nn.sigmoid(x)


def _pack_rows(w):
    return pltpu.bitcast(w.astype(BF16), jnp.uint32)


def _unpack_rows(w32):
    return pltpu.bitcast(w32, BF16)


def _mod_kernel(c_ref, cc_ref, w_ref, b_ref, ox_ref, oc_ref):
    w = w_ref[...].astype(BF16)
    mx = jnp.dot(_silu(c_ref[...]).astype(BF16), w, preferred_element_type=F32) + b_ref[...]
    for i in range(ox_ref.shape[0]):
        ox_ref[i] = mx[i:i + 1, :]
    oc_ref[...] = jnp.dot(_silu(cc_ref[...]).astype(BF16), w, preferred_element_type=F32) + b_ref[...]


def _modulation(c, c_ctx, w_mod, b_mod):
    rows, d = c.shape
    width = w_mod.shape[1]
    bn = MOD_COLS
    return pl.pallas_call(
        _mod_kernel,
        grid=(width // bn,),
        in_specs=[
            pl.BlockSpec((rows, d), lambda j: (0, 0)),
            pl.BlockSpec((1, d), lambda j: (0, 0)),
            pl.BlockSpec((d, bn), lambda j: (0, j)),
            pl.BlockSpec((1, bn), lambda j: (0, j)),
        ],
        out_specs=[pl.BlockSpec((rows, 1, bn), lambda j: (0, 0, j)),
                   pl.BlockSpec((1, bn), lambda j: (0, j))],
        out_shape=[jax.ShapeDtypeStruct((rows, 1, width), F32),
                   jax.ShapeDtypeStruct((1, width), F32)],
        name="mod",
    )(c, c_ctx, w_mod, b_mod)


def _modulated_norm(x, ng, sh, sc):
    ms = jnp.mean(x * x, axis=-1, keepdims=True)
    return x * lax.rsqrt(ms + EPS) * (ng * (1.0 + sc)) + sh


def _proj_kernel(xn_ref, shn_ref, scn_ref, ng_ref, w_ref, cos_ref, sin_ref,
                 lng_ref, lnb_ref, rg_ref, ctx_ref, shc_ref, scc_ref,
                 q_ref, k_ref, v_ref, za_ref, u_ref, vg_ref, zb_ref, ga_ref, gb_ref, kc_ref, vc_ref,
                 hb_ref, wp_ref):
    step = pl.program_id(0)

    @pl.when(step < N_COL_GROUPS)
    def _():
        group = _weight_group(step)
        scale = jnp.where(group == 0, HEAD_DIM ** -0.5, 1.0)
        wp_ref[group] = _pack_rows(w_ref[...] * scale)

    @pl.when((step >= CTX_FIRST_STEP) & (step < PROJ_PROLOGUE_STEPS))
    def _():
        hc = _modulated_norm(ctx_ref[0], ng_ref[...], shc_ref[...], scc_ref[...]).astype(BF16)
        pk = jnp.dot(hc, _unpack_rows(wp_ref[1]), preferred_element_type=F32)
        pv = jnp.dot(hc, _unpack_rows(wp_ref[2]), preferred_element_type=F32)
        for h in range(HEADS):
            kc_ref[0, h] = pk[:, h * HEAD_DIM:(h + 1) * HEAD_DIM].astype(BF16)
            vc_ref[0, h] = pv[:, h * HEAD_DIM:(h + 1) * HEAD_DIM].astype(BF16)

    @pl.when(step == PROJ_PROLOGUE_STEPS - 1)
    def _():
        hb_ref[...] = _modulated_norm(xn_ref[...], ng_ref[...], shn_ref[0], scn_ref[0]).astype(BF16)

    @pl.when(step >= PROJ_PROLOGUE_STEPS)
    def _():
        _proj_rows(xn_ref, shn_ref, scn_ref, ng_ref, wp_ref, cos_ref, sin_ref, lng_ref, lnb_ref, rg_ref,
                   q_ref, k_ref, v_ref, za_ref, u_ref, vg_ref, zb_ref, ga_ref, gb_ref, hb_ref)


def _weight_group(step):
    return jnp.where(step < 2, step + 1, jnp.where(step == 2, 0, jnp.minimum(step, N_COL_GROUPS - 1)))


def _proj_rows(xn_ref, shn_ref, scn_ref, ng_ref, wp_ref, cos_ref, sin_ref, lng_ref, lnb_ref, rg_ref,
               q_ref, k_ref, v_ref, za_ref, u_ref, vg_ref, zb_ref, ga_ref, gb_ref, hb_ref):
    hb_next = _modulated_norm(xn_ref[...], ng_ref[...], shn_ref[0], scn_ref[0]).astype(BF16)

    for r in range(PROJ_ROWS // PROJ_SUB_ROWS):
        rows = slice(r * PROJ_SUB_ROWS, (r + 1) * PROJ_SUB_ROWS)
        cos = cos_ref[rows, :]
        sin = sin_ref[rows, :]

        def group(g):
            return jnp.dot(hb_ref[rows, :], _unpack_rows(wp_ref[g]), preferred_element_type=F32)

        def store_rope(p, o_ref, transposed=False):
            pb = p.astype(BF16)
            for h in range(HEADS):
                t1 = pb[:, h * HEAD_DIM:h * HEAD_DIM + HALF]
                t2 = pb[:, h * HEAD_DIM + HALF:(h + 1) * HEAD_DIM]
                if transposed:
                    o_ref[0, h, :HALF, rows] = (t1 * cos - t2 * sin).T
                    o_ref[0, h, HALF:, rows] = (t1 * sin + t2 * cos).T
                else:
                    o_ref[0, h, rows, :HALF] = t1 * cos - t2 * sin
                    o_ref[0, h, rows, HALF:] = t1 * sin + t2 * cos

        def store_heads(pb, o_ref):
            for h in range(HEADS):
                o_ref[0, h, rows, :] = pb[:, h * HEAD_DIM:(h + 1) * HEAD_DIM]

        store_rope(group(0), q_ref)
        store_heads(_silu(group(3)).astype(BF16) * rg_ref[...].astype(BF16), za_ref)
        u_ref[0, rows, :] = jax.nn.gelu(group(4)).astype(BF16)
        ga_ref[0, rows, :] = jax.nn.sigmoid(group(7)).astype(BF16)
        t = jax.nn.gelu(group(5))
        mu = jnp.mean(t, axis=-1, keepdims=True)
        tc = t - mu
        tn = tc * lax.rsqrt(jnp.mean(tc * tc, axis=-1, keepdims=True) + EPS)
        vg_ref[0, rows, :] = (tn.astype(BF16) * lng_ref[...].astype(BF16)
                              + lnb_ref[...].astype(BF16))
        gb_ref[0, rows, :] = jax.nn.sigmoid(group(8)).astype(BF16)
        store_rope(group(1), k_ref, transposed=True)
        zb_ref[0, rows, :] = _silu(group(6)).astype(BF16)
        store_heads(group(2).astype(BF16), v_ref)

    hb_ref[...] = hb_next


def _proj(x, ctx, mod_x, mod_c, ng, w_in, cos, sin, ln_g, ln_b, ret_g):
    b, n, d = x.shape
    tm = PROJ_ROWS
    per_b = n // tm
    steps = b * per_b
    pre = PROJ_PROLOGUE_STEPS
    n_ctx = ctx.shape[1]
    ctx_b = lambda s: jnp.clip(s - CTX_FIRST_STEP, 0, b - 1)
    x2 = x.reshape(b * n, d)
    row = lambda s: jnp.maximum(s - pre, 0)
    nxt = lambda s: jnp.clip(s + 1 - pre, 0, steps - 1)
    head_out = jax.ShapeDtypeStruct((b, HEADS, n, HEAD_DIM), BF16)
    headt_out = jax.ShapeDtypeStruct((b, HEADS, HEAD_DIM, n), BF16)
    flat_out = jax.ShapeDtypeStruct((b, n, d), BF16)
    head_spec = pl.BlockSpec((1, HEADS, tm, HEAD_DIM), lambda s: (row(s) // per_b, 0, row(s) % per_b, 0))
    headt_spec = pl.BlockSpec((1, HEADS, HEAD_DIM, tm), lambda s: (row(s) // per_b, 0, 0, row(s) % per_b))
    flat_spec = pl.BlockSpec((1, tm, d), lambda s: (row(s) // per_b, row(s) % per_b, 0))
    vec = pl.BlockSpec((1, d), lambda s: (0, 0))
    ctx_spec = pl.BlockSpec((1, HEADS, n_ctx, HEAD_DIM), lambda s: (ctx_b(s), 0, 0, 0))
    ctx_out = jax.ShapeDtypeStruct((b, HEADS, n_ctx, HEAD_DIM), BF16)
    assert CTX_FIRST_STEP + b == pre
    return pl.pallas_call(
        _proj_kernel,
        grid=(pre + steps,),
        in_specs=[
            pl.BlockSpec((tm, d), lambda s: (nxt(s), 0)),
            pl.BlockSpec((1, 1, d), lambda s: (nxt(s) // per_b, 0, 0)),
            pl.BlockSpec((1, 1, d), lambda s: (nxt(s) // per_b, 0, 1)),
            vec,
            pl.BlockSpec((d, d), lambda s: (0, _weight_group(s))),
            pl.BlockSpec((tm, HALF), lambda s: (row(s) % per_b, 0)),
            pl.BlockSpec((tm, HALF), lambda s: (row(s) % per_b, 0)),
            vec, vec, vec,
            pl.BlockSpec((1, n_ctx, d), lambda s: (ctx_b(s), 0, 0)),
            pl.BlockSpec((1, d), lambda s: (0, 0)),
            pl.BlockSpec((1, d), lambda s: (0, 1)),
        ],
        out_specs=[head_spec, headt_spec, head_spec, head_spec] + [flat_spec] * 5 + [ctx_spec] * 2,
        out_shape=[head_out, headt_out, head_out, head_out] + [flat_out] * 5 + [ctx_out] * 2,
        scratch_shapes=[pltpu.VMEM((tm, d), BF16),
                        pltpu.VMEM((N_COL_GROUPS, d // 2, d), jnp.uint32)],
        compiler_params=pltpu.CompilerParams(
            dimension_semantics=("arbitrary",), vmem_limit_bytes=VMEM_LIMIT),
        name="proj",
    )(x2, mod_x, mod_x, ng, w_in, cos, sin, ln_g, ln_b, ret_g, ctx, mod_c, mod_c)


def _dot_tn(a, b):
    return lax.dot_general(a, b, (((0,), (0,)), ((), ())), preferred_element_type=F32)


def _ret_kernel(decf_ref, decb_ref, q_ref, kt_ref, v_ref, za_ref, kc_ref, vc_ref, o_ref,
                sf_ref, tf_ref, tb_ref, dmat_ref, kdf_ref, kdb_ref, qdf_ref, qdb_ref, cwf_ref, cwb_ref,
                *, n_chunks, n_ctx):
    C = RET_CHUNK
    heads = range(RET_HEADS_PER_STEP)

    def log_decay(dec_ref, hh):
        x = dec_ref[...]
        ls = jnp.minimum(x, 0.0) - jnp.log1p(jnp.exp(-jnp.abs(x)))
        lane = lax.broadcasted_iota(jnp.int32, x.shape, 1)
        pick = lane == pl.program_id(0) * RET_HEADS_PER_STEP + hh
        return jnp.sum(jnp.where(pick, ls, 0.0), axis=1, keepdims=True)

    lgf = [log_decay(decf_ref, hh) for hh in heads]
    lgb = [log_decay(decb_ref, hh) for hh in heads]

    @pl.when(pl.program_id(1) == 0)
    def _():
        ii = lax.broadcasted_iota(jnp.int32, (C, C), 0)
        jj = lax.broadcasted_iota(jnp.int32, (C, C), 1)
        diff = (ii - jj).astype(F32)
        row = lax.broadcasted_iota(jnp.int32, (C, HEAD_DIM), 0).astype(F32)
        col = lax.broadcasted_iota(jnp.int32, (HEAD_DIM, C), 1).astype(F32)
        crow = lax.broadcasted_iota(jnp.int32, (n_ctx, HEAD_DIM), 0).astype(F32)
        for hh in heads:
            dmat_ref[hh] = jnp.where(diff >= 0.0,
                                     jnp.exp(lgf[hh] * jnp.maximum(diff, 0.0)),
                                     jnp.exp(lgb[hh] * jnp.maximum(-diff, 0.0))).astype(BF16)
            kdf_ref[hh] = jnp.exp(lgf[hh] * (C - 1.0 - col)).astype(BF16)
            kdb_ref[hh] = jnp.exp(lgb[hh] * col).astype(BF16)
            qdf_ref[hh] = jnp.exp(lgf[hh] * (row + 1.0)).astype(BF16)
            qdb_ref[hh] = jnp.exp(lgb[hh] * (C - row)).astype(BF16)
            cwf_ref[hh] = jnp.exp(lgf[hh] * (n_ctx - 1.0 - crow)).astype(BF16)
            cwb_ref[hh] = jnp.exp(lgb[hh] * crow).astype(BF16)

    cdec_f = [jnp.exp(lgf[hh] * C) for hh in heads]
    cdec_b = [jnp.exp(lgb[hh] * C) for hh in heads]

    for hh in heads:
        kc = kc_ref[0, hh]
        vc = vc_ref[0, hh]
        sf_ref[hh] = _dot_tn(kc * cwf_ref[hh], vc)
        tf_ref[hh] = _dot_tn(kc * cwb_ref[hh], vc)

    for c in range(n_chunks - 1, -1, -1):
        sl = slice(c * C, (c + 1) * C)
        for hh in heads:
            tb_ref[hh, c] = tf_ref[hh].astype(BF16)
            if c > 0:
                tf_ref[hh] = cdec_b[hh] * tf_ref[hh] + jnp.dot(
                    kt_ref[0, hh, :, sl] * kdb_ref[hh], v_ref[0, hh, sl, :], preferred_element_type=F32)

    for c in range(n_chunks):
        sl = slice(c * C, (c + 1) * C)
        for hh in heads:
            qq = q_ref[0, hh, sl, :]
            kt = kt_ref[0, hh, :, sl]
            vv = v_ref[0, hh, sl, :]
            p = jnp.dot(qq, kt, preferred_element_type=F32).astype(BF16) * dmat_ref[hh]
            lhs = jnp.concatenate([p, qq * qdf_ref[hh], qq * qdb_ref[hh]], axis=1)
            rhs = jnp.concatenate([vv, sf_ref[hh].astype(BF16), tb_ref[hh, c]], axis=0)
            o = jnp.dot(lhs, rhs, preferred_element_type=F32)
            if c + 1 < n_chunks:
                sf_ref[hh] = cdec_f[hh] * sf_ref[hh] + jnp.dot(kt * kdf_ref[hh], vv,
                                                               preferred_element_type=F32)
            mu = jnp.mean(o, axis=-1, keepdims=True)
            oc = o - mu
            rstd = lax.rsqrt(jnp.mean(oc * oc, axis=-1, keepdims=True) + EPS)
            o_ref[0, hh, sl, :] = (oc * rstd).astype(BF16) * za_ref[0, hh, sl, :]


def _retention(dec_f, dec_b, q, kt, v, za, kc, vc):
    b, hh, n, dh = q.shape
    n_ctx = kc.shape[2]
    n_chunks = n // RET_CHUNK
    hp = RET_HEADS_PER_STEP
    seq = pl.BlockSpec((1, hp, n, dh), lambda j, i: (i, j, 0, 0))
    seqt = pl.BlockSpec((1, hp, dh, n), lambda j, i: (i, j, 0, 0))
    cseq = pl.BlockSpec((1, hp, n_ctx, dh), lambda j, i: (i, j, 0, 0))
    table = pltpu.VMEM((hp, RET_CHUNK, dh), BF16)
    tablet = pltpu.VMEM((hp, dh, RET_CHUNK), BF16)
    ctable = pltpu.VMEM((hp, n_ctx, dh), BF16)
    return pl.pallas_call(
        functools.partial(_ret_kernel, n_chunks=n_chunks, n_ctx=n_ctx),
        grid=(hh // hp, b),
        in_specs=[
            pl.BlockSpec(dec_f.shape, lambda j, i: (0, 0)),
            pl.BlockSpec(dec_b.shape, lambda j, i: (0, 0)),
            seq, seqt, seq, seq, cseq, cseq,
        ],
        out_specs=seq,
        out_shape=jax.ShapeDtypeStruct((b, hh, n, dh), BF16),
        scratch_shapes=[
            pltpu.VMEM((hp, dh, dh), F32),
            pltpu.VMEM((hp, dh, dh), F32),
            pltpu.VMEM((hp, n_chunks, dh, dh), BF16),
            pltpu.VMEM((hp, RET_CHUNK, RET_CHUNK), BF16),
            tablet, tablet, table, table, ctable, ctable,
        ],
        compiler_params=pltpu.CompilerParams(
            dimension_semantics=("arbitrary", "arbitrary"), vmem_limit_bytes=VMEM_LIMIT),
        name="ret",
    )(dec_f, dec_b, q, kt, v, za, kc, vc)


def _out_kernel(x_ref, ya_ref, u_ref, vg_ref, zb_ref, ga_ref, gb_ref, gt_ref, ws_ref, bias_ref,
                wpa_ref, wpb_ref, wo_ref, fg_ref, o_ref, wpa_p, wpb_p, wo_p):
    @pl.when((pl.program_id(0) == 0) & (pl.program_id(1) == 0))
    def _():
        wpa_p[...] = _pack_rows(wpa_ref[...])
        wpb_p[...] = _pack_rows(wpb_ref[...])
        wo_p[...] = _pack_rows(wo_ref[...])

    tm = x_ref.shape[1]
    gw = D_MODEL // MLP_GROUPS
    n_sub = OUT_SUB_ROWS // MLP_CHUNK
    for r in range(tm // OUT_SUB_ROWS):
        base = r * OUT_SUB_ROWS
        rows = slice(base, base + OUT_SUB_ROWS)
        mixed = [[None] * MLP_GROUPS for _ in range(n_sub)]
        for g in range(MLP_GROUPS):
            cols = slice(g * gw, (g + 1) * gw)
            rhs = jnp.concatenate(
                [vg_ref[0, base + c * MLP_CHUNK:base + (c + 1) * MLP_CHUNK, cols]
                 for c in range(n_sub)], axis=1)
            m = jnp.dot(ws_ref[g], rhs, preferred_element_type=F32)
            for c in range(n_sub):
                mixed[c][g] = m[:, c * gw:(c + 1) * gw] + bias_ref[:, cols]
        mix = jnp.concatenate([jnp.concatenate(mixed[c], axis=1) for c in range(n_sub)], axis=0)
        yb = (u_ref[0, rows, :].astype(F32) * mix * zb_ref[0, rows, :].astype(F32)).astype(BF16)
        ya = jnp.concatenate([ya_ref[0, h, rows, :] for h in range(HEADS)], axis=1)
        pa = jnp.dot(ya, _unpack_rows(wpa_p[...]), preferred_element_type=F32)
        pb = jnp.dot(yb, _unpack_rows(wpb_p[...]), preferred_element_type=F32)
        merged = (ga_ref[0, rows, :].astype(F32) * pa
                  + gb_ref[0, rows, :].astype(F32) * pb).astype(BF16)
        out = jnp.dot(merged, _unpack_rows(wo_p[...]), preferred_element_type=F32)
        xo = x_ref[0, rows, :] + gt_ref[0] * out
        ms = jnp.mean(xo * xo, axis=-1, keepdims=True)
        o_ref[0, rows, :] = xo * lax.rsqrt(ms + EPS) * fg_ref[...]


def _output(x, ya, u, vg, zb, ga, gb, mod_x, ws_b, bias_full, wpa, wpb, wo, fg):
    b, n, d = x.shape
    tm = OUT_ROWS
    flat = pl.BlockSpec((1, tm, d), lambda i, j: (i, j, 0))
    const2 = lambda shape: pl.BlockSpec(shape, lambda i, j: (0,) * len(shape))
    once = lambda shape: pl.BlockSpec(shape, lambda i, j: (0,) * len(shape),
                                      pipeline_mode=pl.Buffered(1))
    return pl.pallas_call(
        _out_kernel,
        grid=(b, n // tm),
        in_specs=[
            flat,
            pl.BlockSpec((1, HEADS, tm, HEAD_DIM), lambda i, j: (i, 0, j, 0)),
            flat, flat, flat, flat, flat,
            pl.BlockSpec((1, 1, d), lambda i, j: (i, 0, 2)),
            const2(ws_b.shape), const2(bias_full.shape),
            once(wpa.shape), once(wpb.shape), once(wo.shape),
            const2((1, d)),
        ],
        out_specs=flat,
        out_shape=jax.ShapeDtypeStruct((b, n, d), F32),
        scratch_shapes=[pltpu.VMEM((d // 2, d), jnp.uint32)] * 3,
        compiler_params=pltpu.CompilerParams(
            dimension_semantics=("arbitrary", "arbitrary"), vmem_limit_bytes=VMEM_LIMIT),
        name="out",
    )(x, ya, u, vg, zb, ga, gb, mod_x, ws_b, bias_full, wpa, wpb, wo, fg)


def kernel(x, c, ctx, c_ctx, w_mod, b_mod, norm_g, w_in, ret_decay_fwd, ret_decay_bwd, ret_norm_g,
           mlp_ln_g, mlp_ln_b, mlp_ws, mlp_bs, w_proj_a, w_proj_b, w_out, final_norm_g):
    b, n, d = x.shape
    assert d == D_MODEL and n % RET_CHUNK == 0 and n % OUT_ROWS == 0 and n % PROJ_ROWS == 0
    assert w_mod.shape[0] == 1, "single-layer block"

    mod_x, mod_c = _modulation(c, c_ctx[None, :], w_mod[0], b_mod[0][None, :])

    ng = norm_g[0][None, :]
    cos, sin = _rope_tables(n)
    w_in2 = w_in.reshape(w_in.shape[1:])

    q, kt, v, za, u, vg, zb, ga, gb, kc, vc = _proj(
        x, ctx, mod_x, mod_c, ng, w_in2, jnp.asarray(cos).astype(BF16), jnp.asarray(sin).astype(BF16),
        mlp_ln_g[0][None, :], mlp_ln_b[0][None, :], ret_norm_g[0][None, :])
    ya = _retention(ret_decay_fwd.astype(F32), ret_decay_bwd.astype(F32), q, kt, v, za, kc, vc)

    bias_full = jnp.repeat(mlp_bs[0].T, d // MLP_GROUPS, axis=1)
    return _output(x, ya, u, vg, zb, ga, gb, mod_x, mlp_ws[0].astype(BF16), bias_full,
                   w_proj_a.reshape(d, d), w_proj_b.reshape(d, d), w_out.reshape(d, d),
                   final_norm_g[None, :])
```

```python
import functools

import numpy as np
import jax
import jax.numpy as jnp
from jax import lax
from jax.experimental import pallas as pl
from jax.experimental.pallas import tpu as pltpu

F32 = jnp.float32
BF16 = jnp.bfloat16

D_MODEL = 1024
HEADS = 4
HEAD_DIM = 256
HALF = HEAD_DIM // 2
GRID_W = 64
MLP_GROUPS = 8
MLP_CHUNK = 128
ROPE_BASE = 10000.0
EPS = 1e-6
N_COL_GROUPS = 9

RET_CHUNK = 256
RET_HEADS_PER_STEP = 2
PROJ_ROWS = 512
PROJ_SUB_ROWS = 256
CTX_FIRST_STEP = 2
PROJ_PROLOGUE_STEPS = 10
MOD_COLS = 3072
OUT_ROWS = 1024
OUT_SUB_ROWS = 256
VMEM_LIMIT = 60 * 1024 * 1024


def _rope_tables(n):
    pos = np.arange(n)
    row = (pos // GRID_W).astype(np.float64)
    col = (pos % GRID_W).astype(np.float64)
    quarter = HALF // 2
    inv = np.power(ROPE_BASE, -np.arange(quarter, dtype=np.float64) / quarter)
    ang = np.concatenate([row[:, None] * inv, col[:, None] * inv], axis=-1)
    return np.cos(ang).astype(np.float32), np.sin(ang).astype(np.float32)


def _silu(x):
    return x * jax.nn.sigmoid(x)


def _pack_rows(w):
    return pltpu.bitcast(w.astype(BF16), jnp.uint32)


def _unpack_rows(w32):
    return pltpu.bitcast(w32, BF16)


def _mod_kernel(c_ref, cc_ref, w_ref, b_ref, ox_ref, oc_ref):
    w = w_ref[...].astype(BF16)
    mx = jnp.dot(_silu(c_ref[...]).astype(BF16), w, preferred_element_type=F32) + b_ref[...]
    for i in range(ox_ref.shape[0]):
        ox_ref[i] = mx[i:i + 1, :]
    oc_ref[...] = jnp.dot(_silu(cc_ref[...]).astype(BF16), w, preferred_element_type=F32) + b_ref[...]


def _modulation(c, c_ctx, w_mod, b_mod):
    rows, d = c.shape
    width = w_mod.shape[1]
    bn = MOD_COLS
    return pl.pallas_call(
        _mod_kernel,
        grid=(width // bn,),
        in_specs=[
            pl.BlockSpec((rows, d), lambda j: (0, 0)),
            pl.BlockSpec((1, d), lambda j: (0, 0)),
            pl.BlockSpec((d, bn), lambda j: (0, j)),
            pl.BlockSpec((1, bn), lambda j: (0, j)),
        ],
        out_specs=[pl.BlockSpec((rows, 1, bn), lambda j: (0, 0, j)),
                   pl.BlockSpec((1, bn), lambda j: (0, j))],
        out_shape=[jax.ShapeDtypeStruct((rows, 1, width), F32),
                   jax.ShapeDtypeStruct((1, width), F32)],
        name="mod",
    )(c, c_ctx, w_mod, b_mod)


def _modulated_norm(x, ng, sh, sc):
    ms = jnp.mean(x * x, axis=-1, keepdims=True)
    return x * lax.rsqrt(ms + EPS) * (ng * (1.0 + sc)) + sh


def _proj_kernel(xn_ref, shn_ref, scn_ref, ng_ref, w_ref, cos_ref, sin_ref,
                 lng_ref, lnb_ref, rg_ref, ctx_ref, shc_ref, scc_ref,
                 q_ref, k_ref, v_ref, za_ref, uz_ref, vg_ref, ga_ref, gb_ref, kc_ref, vc_ref,
                 hb_ref, wp_ref):
    step = pl.program_id(0)

    @pl.when(step < N_COL_GROUPS)
    def _():
        group = _weight_group(step)
        scale = jnp.where(group == 0, HEAD_DIM ** -0.5, 1.0)
        wp_ref[group] = _pack_rows(w_ref[...] * scale)

    @pl.when((step >= CTX_FIRST_STEP) & (step < PROJ_PROLOGUE_STEPS))
    def _():
        hc = _modulated_norm(ctx_ref[0], ng_ref[...], shc_ref[...], scc_ref[...]).astype(BF16)
        pk = jnp.dot(hc, _unpack_rows(wp_ref[1]), preferred_element_type=F32)
        pv = jnp.dot(hc, _unpack_rows(wp_ref[2]), preferred_element_type=F32)
        for h in range(HEADS):
            kc_ref[0, h] = pk[:, h * HEAD_DIM:(h + 1) * HEAD_DIM].astype(BF16)
            vc_ref[0, h] = pv[:, h * HEAD_DIM:(h + 1) * HEAD_DIM].astype(BF16)

    @pl.when(step == PROJ_PROLOGUE_STEPS - 1)
    def _():
        hb_ref[...] = _modulated_norm(xn_ref[...], ng_ref[...], shn_ref[0], scn_ref[0]).astype(BF16)

    @pl.when(step >= PROJ_PROLOGUE_STEPS)
    def _():
        _proj_rows(xn_ref, shn_ref, scn_ref, ng_ref, wp_ref, cos_ref, sin_ref, lng_ref, lnb_ref, rg_ref,
                   q_ref, k_ref, v_ref, za_ref, uz_ref, vg_ref, ga_ref, gb_ref, hb_ref)


def _weight_group(step):
    return jnp.where(step < 2, step + 1, jnp.where(step == 2, 0, jnp.minimum(step, N_COL_GROUPS - 1)))


def _proj_rows(xn_ref, shn_ref, scn_ref, ng_ref, wp_ref, cos_ref, sin_ref, lng_ref, lnb_ref, rg_ref,
               q_ref, k_ref, v_ref, za_ref, uz_ref, vg_ref, ga_ref, gb_ref, hb_ref):
    hb_next = _modulated_norm(xn_ref[...], ng_ref[...], shn_ref[0], scn_ref[0]).astype(BF16)

    for r in range(PROJ_ROWS // PROJ_SUB_ROWS):
        rows = slice(r * PROJ_SUB_ROWS, (r + 1) * PROJ_SUB_ROWS)
        cos = cos_ref[rows, :]
        sin = sin_ref[rows, :]

        def group(g):
            return jnp.dot(hb_ref[rows, :], _unpack_rows(wp_ref[g]), preferred_element_type=F32)

        def store_rope(p, o_ref, transposed=False):
            pb = p.astype(BF16)
            for h in range(HEADS):
                t1 = pb[:, h * HEAD_DIM:h * HEAD_DIM + HALF]
                t2 = pb[:, h * HEAD_DIM + HALF:(h + 1) * HEAD_DIM]
                if transposed:
                    o_ref[0, h, :HALF, rows] = (t1 * cos - t2 * sin).T
                    o_ref[0, h, HALF:, rows] = (t1 * sin + t2 * cos).T
                else:
                    o_ref[0, h, rows, :HALF] = t1 * cos - t2 * sin
                    o_ref[0, h, rows, HALF:] = t1 * sin + t2 * cos

        def store_heads(pb, o_ref):
            for h in range(HEADS):
                o_ref[0, h, rows, :] = pb[:, h * HEAD_DIM:(h + 1) * HEAD_DIM]

        store_rope(group(0), q_ref)
        store_heads(_silu(group(3)).astype(BF16) * rg_ref[...].astype(BF16), za_ref)
        uz_ref[0, rows, :] = jax.nn.gelu(group(4)).astype(BF16)
        ga_ref[0, rows, :] = jax.nn.sigmoid(group(7)).astype(BF16)
        t = jax.nn.gelu(group(5))
        mu = jnp.mean(t, axis=-1, keepdims=True)
        tc = t - mu
        tn = tc * lax.rsqrt(jnp.mean(tc * tc, axis=-1, keepdims=True) + EPS)
        vg_ref[0, rows, :] = (tn.astype(BF16) * lng_ref[...].astype(BF16)
                              + lnb_ref[...].astype(BF16))
        gb_ref[0, rows, :] = jax.nn.sigmoid(group(8)).astype(BF16)
        store_rope(group(1), k_ref, transposed=True)
        uz_ref[0, rows, :] = uz_ref[0, rows, :] * _silu(group(6)).astype(BF16)
        store_heads(group(2).astype(BF16), v_ref)

    hb_ref[...] = hb_next


def _proj(x, ctx, mod_x, mod_c, ng, w_in, cos, sin, ln_g, ln_b, ret_g):
    b, n, d = x.shape
    tm = PROJ_ROWS
    per_b = n // tm
    steps = b * per_b
    pre = PROJ_PROLOGUE_STEPS
    n_ctx = ctx.shape[1]
    ctx_b = lambda s: jnp.clip(s - CTX_FIRST_STEP, 0, b - 1)
    x2 = x.reshape(b * n, d)
    row = lambda s: jnp.maximum(s - pre, 0)
    nxt = lambda s: jnp.clip(s + 1 - pre, 0, steps - 1)
    head_out = jax.ShapeDtypeStruct((b, HEADS, n, HEAD_DIM), BF16)
    headt_out = jax.ShapeDtypeStruct((b, HEADS, HEAD_DIM, n), BF16)
    flat_out = jax.ShapeDtypeStruct((b, n, d), BF16)
    head_spec = pl.BlockSpec((1, HEADS, tm, HEAD_DIM), lambda s: (row(s) // per_b, 0, row(s) % per_b, 0))
    headt_spec = pl.BlockSpec((1, HEADS, HEAD_DIM, tm), lambda s: (row(s) // per_b, 0, 0, row(s) % per_b))
    flat_spec = pl.BlockSpec((1, tm, d), lambda s: (row(s) // per_b, row(s) % per_b, 0))
    vec = pl.BlockSpec((1, d), lambda s: (0, 0))
    ctx_spec = pl.BlockSpec((1, HEADS, n_ctx, HEAD_DIM), lambda s: (ctx_b(s), 0, 0, 0))
    ctx_out = jax.ShapeDtypeStruct((b, HEADS, n_ctx, HEAD_DIM), BF16)
    assert CTX_FIRST_STEP + b == pre
    return pl.pallas_call(
        _proj_kernel,
        grid=(pre + steps,),
        in_specs=[
            pl.BlockSpec((tm, d), lambda s: (nxt(s), 0)),
            pl.BlockSpec((1, 1, d), lambda s: (nxt(s) // per_b, 0, 0)),
            pl.BlockSpec((1, 1, d), lambda s: (nxt(s) // per_b, 0, 1)),
            vec,
            pl.BlockSpec((d, d), lambda s: (0, _weight_group(s))),
            pl.BlockSpec((tm, HALF), lambda s: (row(s) % per_b, 0)),
            pl.BlockSpec((tm, HALF), lambda s: (row(s) % per_b, 0)),
            vec, vec, vec,
            pl.BlockSpec((1, n_ctx, d), lambda s: (ctx_b(s), 0, 0)),
            pl.BlockSpec((1, d), lambda s: (0, 0)),
            pl.BlockSpec((1, d), lambda s: (0, 1)),
        ],
        out_specs=[head_spec, headt_spec, head_spec, head_spec] + [flat_spec] * 4 + [ctx_spec] * 2,
        out_shape=[head_out, headt_out, head_out, head_out] + [flat_out] * 4 + [ctx_out] * 2,
        scratch_shapes=[pltpu.VMEM((tm, d), BF16),
                        pltpu.VMEM((N_COL_GROUPS, d // 2, d), jnp.uint32)],
        compiler_params=pltpu.CompilerParams(
            dimension_semantics=("arbitrary",), vmem_limit_bytes=VMEM_LIMIT),
        name="proj",
    )(x2, mod_x, mod_x, ng, w_in, cos, sin, ln_g, ln_b, ret_g, ctx, mod_c, mod_c)


def _dot_tn(a, b):
    return lax.dot_general(a, b, (((0,), (0,)), ((), ())), preferred_element_type=F32)


def _ret_kernel(decf_ref, decb_ref, q_ref, kt_ref, v_ref, za_ref, kc_ref, vc_ref, o_ref,
                sf_ref, tf_ref, tb_ref, dmat_ref, kdf_ref, kdb_ref, qdf_ref, qdb_ref, cwf_ref, cwb_ref,
                *, n_chunks, n_ctx):
    C = RET_CHUNK
    heads = range(RET_HEADS_PER_STEP)

    def log_decay(dec_ref, hh):
        x = dec_ref[...]
        ls = jnp.minimum(x, 0.0) - jnp.log1p(jnp.exp(-jnp.abs(x)))
        lane = lax.broadcasted_iota(jnp.int32, x.shape, 1)
        pick = lane == pl.program_id(0) * RET_HEADS_PER_STEP + hh
        return jnp.sum(jnp.where(pick, ls, 0.0), axis=1, keepdims=True)

    lgf = [log_decay(decf_ref, hh) for hh in heads]
    lgb = [log_decay(decb_ref, hh) for hh in heads]

    @pl.when(pl.program_id(1) == 0)
    def _():
        ii = lax.broadcasted_iota(jnp.int32, (C, C), 0)
        jj = lax.broadcasted_iota(jnp.int32, (C, C), 1)
        diff = (ii - jj).astype(F32)
        row = lax.broadcasted_iota(jnp.int32, (C, HEAD_DIM), 0).astype(F32)
        col = lax.broadcasted_iota(jnp.int32, (HEAD_DIM, C), 1).astype(F32)
        crow = lax.broadcasted_iota(jnp.int32, (n_ctx, HEAD_DIM), 0).astype(F32)
        for hh in heads:
            dmat_ref[hh] = jnp.where(diff >= 0.0,
                                     jnp.exp(lgf[hh] * jnp.maximum(diff, 0.0)),
                                     jnp.exp(lgb[hh] * jnp.maximum(-diff, 0.0))).astype(BF16)
            kdf_ref[hh] = jnp.exp(lgf[hh] * (C - 1.0 - col)).astype(BF16)
            kdb_ref[hh] = jnp.exp(lgb[hh] * col).astype(BF16)
            qdf_ref[hh] = jnp.exp(lgf[hh] * (row + 1.0)).astype(BF16)
            qdb_ref[hh] = jnp.exp(lgb[hh] * (C - row)).astype(BF16)
            cwf_ref[hh] = jnp.exp(lgf[hh] * (n_ctx - 1.0 - crow)).astype(BF16)
            cwb_ref[hh] = jnp.exp(lgb[hh] * crow).astype(BF16)

    cdec_f = [jnp.exp(lgf[hh] * C) for hh in heads]
    cdec_b = [jnp.exp(lgb[hh] * C) for hh in heads]

    for hh in heads:
        kc = kc_ref[0, hh]
        vc = vc_ref[0, hh]
        sf_ref[hh] = _dot_tn(kc * cwf_ref[hh], vc)
        tf_ref[hh] = _dot_tn(kc * cwb_ref[hh], vc)

    for c in range(n_chunks - 1, -1, -1):
        sl = slice(c * C, (c + 1) * C)
        for hh in heads:
            tb_ref[hh, c] = tf_ref[hh].astype(BF16)
            if c > 0:
                tf_ref[hh] = cdec_b[hh] * tf_ref[hh] + jnp.dot(
                    kt_ref[0, hh, :, sl] * kdb_ref[hh], v_ref[0, hh, sl, :], preferred_element_type=F32)

    for c in range(n_chunks):
        sl = slice(c * C, (c + 1) * C)
        for hh in heads:
            qq = q_ref[0, hh, sl, :]
            kt = kt_ref[0, hh, :, sl]
            vv = v_ref[0, hh, sl, :]
            p = jnp.dot(qq, kt, preferred_element_type=F32).astype(BF16) * dmat_ref[hh]
            lhs = jnp.concatenate([p, qq * qdf_ref[hh], qq * qdb_ref[hh]], axis=1)
            rhs = jnp.concatenate([vv, sf_ref[hh].astype(BF16), tb_ref[hh, c]], axis=0)
            o = jnp.dot(lhs, rhs, preferred_element_type=F32)
            if c + 1 < n_chunks:
                sf_ref[hh] = cdec_f[hh] * sf_ref[hh] + jnp.dot(kt * kdf_ref[hh], vv,
                                                               preferred_element_type=F32)
            mu = jnp.mean(o, axis=-1, keepdims=True)
            oc = o - mu
            rstd = lax.rsqrt(jnp.mean(oc * oc, axis=-1, keepdims=True) + EPS)
            o_ref[0, hh, sl, :] = (oc * rstd).astype(BF16) * za_ref[0, hh, sl, :]


def _retention(dec_f, dec_b, q, kt, v, za, kc, vc):
    b, hh, n, dh = q.shape
    n_ctx = kc.shape[2]
    n_chunks = n // RET_CHUNK
    hp = RET_HEADS_PER_STEP
    seq = pl.BlockSpec((1, hp, n, dh), lambda j, i: (i, j, 0, 0))
    seqt = pl.BlockSpec((1, hp, dh, n), lambda j, i: (i, j, 0, 0))
    cseq = pl.BlockSpec((1, hp, n_ctx, dh), lambda j, i: (i, j, 0, 0))
    table = pltpu.VMEM((hp, RET_CHUNK, dh), BF16)
    tablet = pltpu.VMEM((hp, dh, RET_CHUNK), BF16)
    ctable = pltpu.VMEM((hp, n_ctx, dh), BF16)
    return pl.pallas_call(
        functools.partial(_ret_kernel, n_chunks=n_chunks, n_ctx=n_ctx),
        grid=(hh // hp, b),
        in_specs=[
            pl.BlockSpec(dec_f.shape, lambda j, i: (0, 0)),
            pl.BlockSpec(dec_b.shape, lambda j, i: (0, 0)),
            seq, seqt, seq, seq, cseq, cseq,
        ],
        out_specs=seq,
        out_shape=jax.ShapeDtypeStruct((b, hh, n, dh), BF16),
        scratch_shapes=[
            pltpu.VMEM((hp, dh, dh), F32),
            pltpu.VMEM((hp, dh, dh), F32),
            pltpu.VMEM((hp, n_chunks, dh, dh), BF16),
            pltpu.VMEM((hp, RET_CHUNK, RET_CHUNK), BF16),
            tablet, tablet, table, table, ctable, ctable,
        ],
        compiler_params=pltpu.CompilerParams(
            dimension_semantics=("arbitrary", "arbitrary"), vmem_limit_bytes=VMEM_LIMIT),
        name="ret",
    )(dec_f, dec_b, q, kt, v, za, kc, vc)


def _out_kernel(x_ref, ya_ref, uz_ref, vg_ref, ga_ref, gb_ref, gt_ref, ws_ref, bias_ref,
                wpa_ref, wpb_ref, wo_ref, fg_ref, o_ref, wpa_p, wpb_p, wo_p):
    @pl.when((pl.program_id(0) == 0) & (pl.program_id(1) == 0))
    def _():
        wpa_p[...] = _pack_rows(wpa_ref[...])
        wpb_p[...] = _pack_rows(wpb_ref[...])
        wo_p[...] = _pack_rows(wo_ref[...])

    tm = x_ref.shape[1]
    gw = D_MODEL // MLP_GROUPS
    n_sub = OUT_SUB_ROWS // MLP_CHUNK
    for r in range(tm // OUT_SUB_ROWS):
        base = r * OUT_SUB_ROWS
        rows = slice(base, base + OUT_SUB_ROWS)
        mixed = [[None] * MLP_GROUPS for _ in range(n_sub)]
        for g in range(MLP_GROUPS):
            cols = slice(g * gw, (g + 1) * gw)
            rhs = jnp.concatenate(
                [vg_ref[0, base + c * MLP_CHUNK:base + (c + 1) * MLP_CHUNK, cols]
                 for c in range(n_sub)], axis=1)
            m = jnp.dot(ws_ref[g], rhs, preferred_element_type=F32)
            for c in range(n_sub):
                mixed[c][g] = m[:, c * gw:(c + 1) * gw] + bias_ref[:, cols]
        mix = jnp.concatenate([jnp.concatenate(mixed[c], axis=1) for c in range(n_sub)], axis=0)
        yb = (uz_ref[0, rows, :].astype(F32) * mix).astype(BF16)
        ya = jnp.concatenate([ya_ref[0, h, rows, :] for h in range(HEADS)], axis=1)
        pa = jnp.dot(ya, _unpack_rows(wpa_p[...]), preferred_element_type=F32)
        pb = jnp.dot(yb, _unpack_rows(wpb_p[...]), preferred_element_type=F32)
        merged = (ga_ref[0, rows, :].astype(F32) * pa
                  + gb_ref[0, rows, :].astype(F32) * pb).astype(BF16)
        out = jnp.dot(merged, _unpack_rows(wo_p[...]), preferred_element_type=F32)
        xo = x_ref[0, rows, :] + gt_ref[0] * out
        ms = jnp.mean(xo * xo, axis=-1, keepdims=True)
        o_ref[0, rows, :] = xo * lax.rsqrt(ms + EPS) * fg_ref[...]


def _output(x, ya, uz, vg, ga, gb, mod_x, ws_b, bias_full, wpa, wpb, wo, fg):
    b, n, d = x.shape
    tm = OUT_ROWS
    flat = pl.BlockSpec((1, tm, d), lambda i, j: (i, j, 0))
    const2 = lambda shape: pl.BlockSpec(shape, lambda i, j: (0,) * len(shape))
    once = lambda shape: pl.BlockSpec(shape, lambda i, j: (0,) * len(shape),
                                      pipeline_mode=pl.Buffered(1))
    return pl.pallas_call(
        _out_kernel,
        grid=(b, n // tm),
        in_specs=[
            flat,
            pl.BlockSpec((1, HEADS, tm, HEAD_DIM), lambda i, j: (i, 0, j, 0)),
            flat, flat, flat, flat,
            pl.BlockSpec((1, 1, d), lambda i, j: (i, 0, 2)),
            const2(ws_b.shape), const2(bias_full.shape),
            once(wpa.shape), once(wpb.shape), once(wo.shape),
            const2((1, d)),
        ],
        out_specs=flat,
        out_shape=jax.ShapeDtypeStruct((b, n, d), F32),
        scratch_shapes=[pltpu.VMEM((d // 2, d), jnp.uint32)] * 3,
        compiler_params=pltpu.CompilerParams(
            dimension_semantics=("arbitrary", "arbitrary"), vmem_limit_bytes=VMEM_LIMIT),
        name="out",
    )(x, ya, uz, vg, ga, gb, mod_x, ws_b, bias_full, wpa, wpb, wo, fg)


def kernel(x, c, ctx, c_ctx, w_mod, b_mod, norm_g, w_in, ret_decay_fwd, ret_decay_bwd, ret_norm_g,
           mlp_ln_g, mlp_ln_b, mlp_ws, mlp_bs, w_proj_a, w_proj_b, w_out, final_norm_g):
    b, n, d = x.shape
    assert d == D_MODEL and n % RET_CHUNK == 0 and n % OUT_ROWS == 0 and n % PROJ_ROWS == 0
    assert w_mod.shape[0] == 1, "single-layer block"

    mod_x, mod_c = _modulation(c, c_ctx[None, :], w_mod[0], b_mod[0][None, :])

    ng = norm_g[0][None, :]
    cos, sin = _rope_tables(n)
    w_in2 = w_in.reshape(w_in.shape[1:])

    q, kt, v, za, uz, vg, ga, gb, kc, vc = _proj(
        x, ctx, mod_x, mod_c, ng, w_in2, jnp.asarray(cos).astype(BF16), jnp.asarray(sin).astype(BF16),
        mlp_ln_g[0][None, :], mlp_ln_b[0][None, :], ret_norm_g[0][None, :])
    ya = _retention(ret_decay_fwd.astype(F32), ret_decay_bwd.astype(F32), q, kt, v, za, kc, vc)

    bias_full = jnp.repeat(mlp_bs[0].T, d // MLP_GROUPS, axis=1)
    return _output(x, ya, uz, vg, ga, gb, mod_x, mlp_ws[0].astype(BF16), bias_full,
                   w_proj_a.reshape(d, d), w_proj_b.reshape(d, d), w_out.reshape(d, d),
                   final_norm_g[None, :])
```

```python
import functools

import numpy as np
import jax
import jax.numpy as jnp
from jax import lax
from jax.experimental import pallas as pl
from jax.experimental.pallas import tpu as pltpu

F32 = jnp.float32
BF16 = jnp.bfloat16

D_MODEL = 1024
HEADS = 4
HEAD_DIM = 256
HALF = HEAD_DIM // 2
GRID_W = 64
MLP_GROUPS = 8
MLP_CHUNK = 128
ROPE_BASE = 10000.0
EPS = 1e-6
N_COL_GROUPS = 9

RET_CHUNK = 256
RET_HEADS_PER_STEP = 4
PROJ_ROWS = 512
PROJ_SUB_ROWS = 256
CTX_FIRST_STEP = 2
PROJ_PROLOGUE_STEPS = 10
MOD_COLS = 3072
OUT_ROWS = 1024
OUT_SUB_ROWS = 256
VMEM_LIMIT = 60 * 1024 * 1024


def _rope_tables(n):
    pos = np.arange(n)
    row = (pos // GRID_W).astype(np.float64)
    col = (pos % GRID_W).astype(np.float64)
    quarter = HALF // 2
    inv = np.power(ROPE_BASE, -np.arange(quarter, dtype=np.float64) / quarter)
    ang = np.concatenate([row[:, None] * inv, col[:, None] * inv], axis=-1)
    return np.cos(ang).astype(np.float32), np.sin(ang).astype(np.float32)


def _silu(x):
    return x * jax.nn.sigmoid(x)


def _pack_rows(w):
    return pltpu.bitcast(w.astype(BF16), jnp.uint32)


def _unpack_rows(w32):
    return pltpu.bitcast(w32, BF16)


def _mod_kernel(c_ref, cc_ref, w_ref, b_ref, ox_ref, oc_ref):
    w = w_ref[...].astype(BF16)
    mx = jnp.dot(_silu(c_ref[...]).astype(BF16), w, preferred_element_type=F32) + b_ref[...]
    for i in range(ox_ref.shape[0]):
        ox_ref[i] = mx[i:i + 1, :]
    oc_ref[...] = jnp.dot(_silu(cc_ref[...]).astype(BF16), w, preferred_element_type=F32) + b_ref[...]


def _modulation(c, c_ctx, w_mod, b_mod):
    rows, d = c.shape
    width = w_mod.shape[1]
    bn = MOD_COLS
    return pl.pallas_call(
        _mod_kernel,
        grid=(width // bn,),
        in_specs=[
            pl.BlockSpec((rows, d), lambda j: (0, 0)),
            pl.BlockSpec((1, d), lambda j: (0, 0)),
            pl.BlockSpec((d, bn), lambda j: (0, j)),
            pl.BlockSpec((1, bn), lambda j: (0, j)),
        ],
        out_specs=[pl.BlockSpec((rows, 1, bn), lambda j: (0, 0, j)),
                   pl.BlockSpec((1, bn), lambda j: (0, j))],
        out_shape=[jax.ShapeDtypeStruct((rows, 1, width), F32),
                   jax.ShapeDtypeStruct((1, width), F32)],
        name="mod",
    )(c, c_ctx, w_mod, b_mod)


def _modulated_norm(x, ng, sh, sc):
    ms = jnp.mean(x * x, axis=-1, keepdims=True)
    return x * lax.rsqrt(ms + EPS) * (ng * (1.0 + sc)) + sh


def _proj_kernel(xn_ref, shn_ref, scn_ref, ng_ref, w_ref, cos_ref, sin_ref,
                 lng_ref, lnb_ref, rg_ref, ctx_ref, shc_ref, scc_ref,
                 q_ref, k_ref, v_ref, za_ref, uz_ref, vg_ref, ga_ref, gb_ref, kc_ref, vc_ref,
                 hb_ref, wp_ref):
    step = pl.program_id(0)

    @pl.when(step < N_COL_GROUPS)
    def _():
        group = _weight_group(step)
        scale = jnp.where(group == 0, HEAD_DIM ** -0.5, 1.0)
        wp_ref[group] = _pack_rows(w_ref[...] * scale)

    @pl.when((step >= CTX_FIRST_STEP) & (step < PROJ_PROLOGUE_STEPS))
    def _():
        hc = _modulated_norm(ctx_ref[0], ng_ref[...], shc_ref[...], scc_ref[...]).astype(BF16)
        pk = jnp.dot(hc, _unpack_rows(wp_ref[1]), preferred_element_type=F32)
        pv = jnp.dot(hc, _unpack_rows(wp_ref[2]), preferred_element_type=F32)
        for h in range(HEADS):
            kc_ref[0, h] = pk[:, h * HEAD_DIM:(h + 1) * HEAD_DIM].astype(BF16)
            vc_ref[0, h] = pv[:, h * HEAD_DIM:(h + 1) * HEAD_DIM].astype(BF16)

    @pl.when(step == PROJ_PROLOGUE_STEPS - 1)
    def _():
        hb_ref[...] = _modulated_norm(xn_ref[...], ng_ref[...], shn_ref[0], scn_ref[0]).astype(BF16)

    @pl.when(step >= PROJ_PROLOGUE_STEPS)
    def _():
        _proj_rows(xn_ref, shn_ref, scn_ref, ng_ref, wp_ref, cos_ref, sin_ref, lng_ref, lnb_ref, rg_ref,
                   q_ref, k_ref, v_ref, za_ref, uz_ref, vg_ref, ga_ref, gb_ref, hb_ref)


def _weight_group(step):
    return jnp.where(step < 2, step + 1, jnp.where(step == 2, 0, jnp.minimum(step, N_COL_GROUPS - 1)))


def _proj_rows(xn_ref, shn_ref, scn_ref, ng_ref, wp_ref, cos_ref, sin_ref, lng_ref, lnb_ref, rg_ref,
               q_ref, k_ref, v_ref, za_ref, uz_ref, vg_ref, ga_ref, gb_ref, hb_ref):
    hb_next = _modulated_norm(xn_ref[...], ng_ref[...], shn_ref[0], scn_ref[0]).astype(BF16)

    for r in range(PROJ_ROWS // PROJ_SUB_ROWS):
        rows = slice(r * PROJ_SUB_ROWS, (r + 1) * PROJ_SUB_ROWS)
        cos = cos_ref[rows, :]
        sin = sin_ref[rows, :]

        def group(g):
            return jnp.dot(hb_ref[rows, :], _unpack_rows(wp_ref[g]), preferred_element_type=F32)

        def store_rope(p, o_ref, transposed=False):
            pb = p.astype(BF16)
            for h in range(HEADS):
                t1 = pb[:, h * HEAD_DIM:h * HEAD_DIM + HALF]
                t2 = pb[:, h * HEAD_DIM + HALF:(h + 1) * HEAD_DIM]
                if transposed:
                    o_ref[0, h, :HALF, rows] = (t1 * cos - t2 * sin).T
                    o_ref[0, h, HALF:, rows] = (t1 * sin + t2 * cos).T
                else:
                    o_ref[0, h, rows, :HALF] = t1 * cos - t2 * sin
                    o_ref[0, h, rows, HALF:] = t1 * sin + t2 * cos

        def store_heads(pb, o_ref):
            for h in range(HEADS):
                o_ref[0, h, rows, :] = pb[:, h * HEAD_DIM:(h + 1) * HEAD_DIM]

        store_rope(group(0), q_ref)
        store_heads(_silu(group(3)).astype(BF16) * rg_ref[...].astype(BF16), za_ref)
        uz_ref[0, rows, :] = jax.nn.gelu(group(4)).astype(BF16)
        ga_ref[0, rows, :] = jax.nn.sigmoid(group(7)).astype(BF16)
        t = jax.nn.gelu(group(5))
        mu = jnp.mean(t, axis=-1, keepdims=True)
        tc = t - mu
        tn = tc * lax.rsqrt(jnp.mean(tc * tc, axis=-1, keepdims=True) + EPS)
        vg_ref[0, rows, :] = (tn.astype(BF16) * lng_ref[...].astype(BF16)
                              + lnb_ref[...].astype(BF16))
        gb_ref[0, rows, :] = jax.nn.sigmoid(group(8)).astype(BF16)
        store_rope(group(1), k_ref, transposed=True)
        uz_ref[0, rows, :] = uz_ref[0, rows, :] * _silu(group(6)).astype(BF16)
        store_heads(group(2).astype(BF16), v_ref)

    hb_ref[...] = hb_next


def _proj(x, ctx, mod_x, mod_c, ng, w_in, cos, sin, ln_g, ln_b, ret_g):
    b, n, d = x.shape
    tm = PROJ_ROWS
    per_b = n // tm
    steps = b * per_b
    pre = PROJ_PROLOGUE_STEPS
    n_ctx = ctx.shape[1]
    ctx_b = lambda s: jnp.clip(s - CTX_FIRST_STEP, 0, b - 1)
    x2 = x.reshape(b * n, d)
    row = lambda s: jnp.maximum(s - pre, 0)
    nxt = lambda s: jnp.clip(s + 1 - pre, 0, steps - 1)
    head_out = jax.ShapeDtypeStruct((b, HEADS, n, HEAD_DIM), BF16)
    headt_out = jax.ShapeDtypeStruct((b, HEADS, HEAD_DIM, n), BF16)
    flat_out = jax.ShapeDtypeStruct((b, n, d), BF16)
    head_spec = pl.BlockSpec((1, HEADS, tm, HEAD_DIM), lambda s: (row(s) // per_b, 0, row(s) % per_b, 0))
    headt_spec = pl.BlockSpec((1, HEADS, HEAD_DIM, tm), lambda s: (row(s) // per_b, 0, 0, row(s) % per_b))
    flat_spec = pl.BlockSpec((1, tm, d), lambda s: (row(s) // per_b, row(s) % per_b, 0))
    vec = pl.BlockSpec((1, d), lambda s: (0, 0))
    ctx_spec = pl.BlockSpec((1, HEADS, n_ctx, HEAD_DIM), lambda s: (ctx_b(s), 0, 0, 0))
    ctx_out = jax.ShapeDtypeStruct((b, HEADS, n_ctx, HEAD_DIM), BF16)
    assert CTX_FIRST_STEP + b == pre
    return pl.pallas_call(
        _proj_kernel,
        grid=(pre + steps,),
        in_specs=[
            pl.BlockSpec((tm, d), lambda s: (nxt(s), 0)),
            pl.BlockSpec((1, 1, d), lambda s: (nxt(s) // per_b, 0, 0)),
            pl.BlockSpec((1, 1, d), lambda s: (nxt(s) // per_b, 0, 1)),
            vec,
            pl.BlockSpec((d, d), lambda s: (0, _weight_group(s))),
            pl.BlockSpec((tm, HALF), lambda s: (row(s) % per_b, 0)),
            pl.BlockSpec((tm, HALF), lambda s: (row(s) % per_b, 0)),
            vec, vec, vec,
            pl.BlockSpec((1, n_ctx, d), lambda s: (ctx_b(s), 0, 0)),
            pl.BlockSpec((1, d), lambda s: (0, 0)),
            pl.BlockSpec((1, d), lambda s: (0, 1)),
        ],
        out_specs=[head_spec, headt_spec, head_spec, head_spec] + [flat_spec] * 4 + [ctx_spec] * 2,
        out_shape=[head_out, headt_out, head_out, head_out] + [flat_out] * 4 + [ctx_out] * 2,
        scratch_shapes=[pltpu.VMEM((tm, d), BF16),
                        pltpu.VMEM((N_COL_GROUPS, d // 2, d), jnp.uint32)],
        compiler_params=pltpu.CompilerParams(
            dimension_semantics=("arbitrary",), vmem_limit_bytes=VMEM_LIMIT),
        name="proj",
    )(x2, mod_x, mod_x, ng, w_in, cos, sin, ln_g, ln_b, ret_g, ctx, mod_c, mod_c)


def _dot_tn(a, b):
    return lax.dot_general(a, b, (((0,), (0,)), ((), ())), preferred_element_type=F32)


def _ret_kernel(decf_ref, decb_ref, q_ref, kt_ref, v_ref, za_ref, kc_ref, vc_ref, o_ref,
                sf_ref, tf_ref, tb_ref, dmat_ref, kdf_ref, kdb_ref, qdf_ref, qdb_ref, cwf_ref, cwb_ref,
                *, n_chunks, n_ctx):
    C = RET_CHUNK
    heads = range(RET_HEADS_PER_STEP)

    def log_decay(dec_ref, hh):
        x = dec_ref[...]
        ls = jnp.minimum(x, 0.0) - jnp.log1p(jnp.exp(-jnp.abs(x)))
        lane = lax.broadcasted_iota(jnp.int32, x.shape, 1)
        pick = lane == pl.program_id(0) * RET_HEADS_PER_STEP + hh
        return jnp.sum(jnp.where(pick, ls, 0.0), axis=1, keepdims=True)

    lgf = [log_decay(decf_ref, hh) for hh in heads]
    lgb = [log_decay(decb_ref, hh) for hh in heads]

    @pl.when(pl.program_id(1) == 0)
    def _():
        ii = lax.broadcasted_iota(jnp.int32, (C, C), 0)
        jj = lax.broadcasted_iota(jnp.int32, (C, C), 1)
        diff = (ii - jj).astype(F32)
        row = lax.broadcasted_iota(jnp.int32, (C, HEAD_DIM), 0).astype(F32)
        col = lax.broadcasted_iota(jnp.int32, (HEAD_DIM, C), 1).astype(F32)
        crow = lax.broadcasted_iota(jnp.int32, (n_ctx, HEAD_DIM), 0).astype(F32)
        for hh in heads:
            dmat_ref[hh] = jnp.where(diff >= 0.0,
                                     jnp.exp(lgf[hh] * jnp.maximum(diff, 0.0)),
                                     jnp.exp(lgb[hh] * jnp.maximum(-diff, 0.0))).astype(BF16)
            kdf_ref[hh] = jnp.exp(lgf[hh] * (C - 1.0 - col)).astype(BF16)
            kdb_ref[hh] = jnp.exp(lgb[hh] * col).astype(BF16)
            qdf_ref[hh] = jnp.exp(lgf[hh] * (row + 1.0)).astype(BF16)
            qdb_ref[hh] = jnp.exp(lgb[hh] * (C - row)).astype(BF16)
            cwf_ref[hh] = jnp.exp(lgf[hh] * (n_ctx - 1.0 - crow)).astype(BF16)
            cwb_ref[hh] = jnp.exp(lgb[hh] * crow).astype(BF16)

    cdec_f = [jnp.exp(lgf[hh] * C) for hh in heads]
    cdec_b = [jnp.exp(lgb[hh] * C) for hh in heads]

    for hh in heads:
        kc = kc_ref[0, hh]
        vc = vc_ref[0, hh]
        sf_ref[hh] = _dot_tn(kc * cwf_ref[hh], vc)
        tf_ref[hh] = _dot_tn(kc * cwb_ref[hh], vc)

    for c in range(n_chunks - 1, -1, -1):
        sl = slice(c * C, (c + 1) * C)
        for hh in heads:
            tb_ref[hh, c] = tf_ref[hh].astype(BF16)
            if c > 0:
                tf_ref[hh] = cdec_b[hh] * tf_ref[hh] + jnp.dot(
                    kt_ref[0, hh, :, sl] * kdb_ref[hh], v_ref[0, hh, sl, :], preferred_element_type=F32)

    for c in range(n_chunks):
        sl = slice(c * C, (c + 1) * C)
        for hh in heads:
            qq = q_ref[0, hh, sl, :]
            kt = kt_ref[0, hh, :, sl]
            vv = v_ref[0, hh, sl, :]
            p = jnp.dot(qq, kt, preferred_element_type=F32).astype(BF16) * dmat_ref[hh]
            lhs = jnp.concatenate([p, qq * qdf_ref[hh], qq * qdb_ref[hh]], axis=1)
            rhs = jnp.concatenate([vv, sf_ref[hh].astype(BF16), tb_ref[hh, c]], axis=0)
            o = jnp.dot(lhs, rhs, preferred_element_type=F32)
            if c + 1 < n_chunks:
                sf_ref[hh] = cdec_f[hh] * sf_ref[hh] + jnp.dot(kt * kdf_ref[hh], vv,
                                                               preferred_element_type=F32)
            mu = jnp.mean(o, axis=-1, keepdims=True)
            oc = o - mu
            rstd = lax.rsqrt(jnp.mean(oc * oc, axis=-1, keepdims=True) + EPS)
            o_ref[0, hh, sl, :] = (oc * rstd).astype(BF16) * za_ref[0, hh, sl, :]


def _retention(dec_f, dec_b, q, kt, v, za, kc, vc):
    b, hh, n, dh = q.shape
    n_ctx = kc.shape[2]
    n_chunks = n // RET_CHUNK
    hp = RET_HEADS_PER_STEP
    seq = pl.BlockSpec((1, hp, n, dh), lambda j, i: (i, j, 0, 0))
    seqt = pl.BlockSpec((1, hp, dh, n), lambda j, i: (i, j, 0, 0))
    cseq = pl.BlockSpec((1, hp, n_ctx, dh), lambda j, i: (i, j, 0, 0))
    table = pltpu.VMEM((hp, RET_CHUNK, dh), BF16)
    tablet = pltpu.VMEM((hp, dh, RET_CHUNK), BF16)
    ctable = pltpu.VMEM((hp, n_ctx, dh), BF16)
    return pl.pallas_call(
        functools.partial(_ret_kernel, n_chunks=n_chunks, n_ctx=n_ctx),
        grid=(hh // hp, b),
        in_specs=[
            pl.BlockSpec(dec_f.shape, lambda j, i: (0, 0)),
            pl.BlockSpec(dec_b.shape, lambda j, i: (0, 0)),
            seq, seqt, seq, seq, cseq, cseq,
        ],
        out_specs=seq,
        out_shape=jax.ShapeDtypeStruct((b, hh, n, dh), BF16),
        scratch_shapes=[
            pltpu.VMEM((hp, dh, dh), F32),
            pltpu.VMEM((hp, dh, dh), F32),
            pltpu.VMEM((hp, n_chunks, dh, dh), BF16),
            pltpu.VMEM((hp, RET_CHUNK, RET_CHUNK), BF16),
            tablet, tablet, table, table, ctable, ctable,
        ],
        compiler_params=pltpu.CompilerParams(
            dimension_semantics=("arbitrary", "arbitrary"), vmem_limit_bytes=VMEM_LIMIT),
        name="ret",
    )(dec_f, dec_b, q, kt, v, za, kc, vc)


def _out_kernel(x_ref, ya_ref, uz_ref, vg_ref, ga_ref, gb_ref, gt_ref, ws_ref, bias_ref,
                wpa_ref, wpb_ref, wo_ref, fg_ref, o_ref, wpa_p, wpb_p, wo_p):
    @pl.when((pl.program_id(0) == 0) & (pl.program_id(1) == 0))
    def _():
        wpa_p[...] = _pack_rows(wpa_ref[...])
        wpb_p[...] = _pack_rows(wpb_ref[...])
        wo_p[...] = _pack_rows(wo_ref[...])

    tm = x_ref.shape[1]
    gw = D_MODEL // MLP_GROUPS
    n_sub = OUT_SUB_ROWS // MLP_CHUNK
    for r in range(tm // OUT_SUB_ROWS):
        base = r * OUT_SUB_ROWS
        rows = slice(base, base + OUT_SUB_ROWS)
        mixed = [[None] * MLP_GROUPS for _ in range(n_sub)]
        for g in range(MLP_GROUPS):
            cols = slice(g * gw, (g + 1) * gw)
            rhs = jnp.concatenate(
                [vg_ref[0, base + c * MLP_CHUNK:base + (c + 1) * MLP_CHUNK, cols]
                 for c in range(n_sub)], axis=1)
            m = jnp.dot(ws_ref[g], rhs, preferred_element_type=F32)
            for c in range(n_sub):
                mixed[c][g] = m[:, c * gw:(c + 1) * gw] + bias_ref[:, cols]
        mix = jnp.concatenate([jnp.concatenate(mixed[c], axis=1) for c in range(n_sub)], axis=0)
        yb = (uz_ref[0, rows, :].astype(F32) * mix).astype(BF16)
        ya = jnp.concatenate([ya_ref[0, h, rows, :] for h in range(HEADS)], axis=1)
        pa = jnp.dot(ya, _unpack_rows(wpa_p[...]), preferred_element_type=F32)
        pb = jnp.dot(yb, _unpack_rows(wpb_p[...]), preferred_element_type=F32)
        merged = (ga_ref[0, rows, :].astype(F32) * pa
                  + gb_ref[0, rows, :].astype(F32) * pb).astype(BF16)
        out = jnp.dot(merged, _unpack_rows(wo_p[...]), preferred_element_type=F32)
        xo = x_ref[0, rows, :] + gt_ref[0] * out
        ms = jnp.mean(xo * xo, axis=-1, keepdims=True)
        o_ref[0, rows, :] = xo * lax.rsqrt(ms + EPS) * fg_ref[...]


def _output(x, ya, uz, vg, ga, gb, mod_x, ws_b, bias_full, wpa, wpb, wo, fg):
    b, n, d = x.shape
    tm = OUT_ROWS
    flat = pl.BlockSpec((1, tm, d), lambda i, j: (i, j, 0))
    const2 = lambda shape: pl.BlockSpec(shape, lambda i, j: (0,) * len(shape))
    once = lambda shape: pl.BlockSpec(shape, lambda i, j: (0,) * len(shape),
                                      pipeline_mode=pl.Buffered(1))
    return pl.pallas_call(
        _out_kernel,
        grid=(b, n // tm),
        in_specs=[
            flat,
            pl.BlockSpec((1, HEADS, tm, HEAD_DIM), lambda i, j: (i, 0, j, 0)),
            flat, flat, flat, flat,
            pl.BlockSpec((1, 1, d), lambda i, j: (i, 0, 2)),
            const2(ws_b.shape), const2(bias_full.shape),
            once(wpa.shape), once(wpb.shape), once(wo.shape),
            const2((1, d)),
        ],
        out_specs=flat,
        out_shape=jax.ShapeDtypeStruct((b, n, d), F32),
        scratch_shapes=[pltpu.VMEM((d // 2, d), jnp.uint32)] * 3,
        compiler_params=pltpu.CompilerParams(
            dimension_semantics=("arbitrary", "arbitrary"), vmem_limit_bytes=VMEM_LIMIT),
        name="out",
    )(x, ya, uz, vg, ga, gb, mod_x, ws_b, bias_full, wpa, wpb, wo, fg)


def kernel(x, c, ctx, c_ctx, w_mod, b_mod, norm_g, w_in, ret_decay_fwd, ret_decay_bwd, ret_norm_g,
           mlp_ln_g, mlp_ln_b, mlp_ws, mlp_bs, w_proj_a, w_proj_b, w_out, final_norm_g):
    b, n, d = x.shape
    assert d == D_MODEL and n % RET_CHUNK == 0 and n % OUT_ROWS == 0 and n % PROJ_ROWS == 0
    assert w_mod.shape[0] == 1, "single-layer block"

    mod_x, mod_c = _modulation(c, c_ctx[None, :], w_mod[0], b_mod[0][None, :])

    ng = norm_g[0][None, :]
    cos, sin = _rope_tables(n)
    w_in2 = w_in.reshape(w_in.shape[1:])

    q, kt, v, za, uz, vg, ga, gb, kc, vc = _proj(
        x, ctx, mod_x, mod_c, ng, w_in2, jnp.asarray(cos).astype(BF16), jnp.asarray(sin).astype(BF16),
        mlp_ln_g[0][None, :], mlp_ln_b[0][None, :], ret_norm_g[0][None, :])
    ya = _retention(ret_decay_fwd.astype(F32), ret_decay_bwd.astype(F32), q, kt, v, za, kc, vc)

    bias_full = jnp.repeat(mlp_bs[0].T, d // MLP_GROUPS, axis=1)
    return _output(x, ya, uz, vg, ga, gb, mod_x, mlp_ws[0].astype(BF16), bias_full,
                   w_proj_a.reshape(d, d), w_proj_b.reshape(d, d), w_out.reshape(d, d),
                   final_norm_g[None, :])
```

```python
import functools

import numpy as np
import jax
import jax.numpy as jnp
from jax import lax
from jax.experimental import pallas as pl
from jax.experimental.pallas import tpu as pltpu

F32 = jnp.float32
BF16 = jnp.bfloat16

D_MODEL = 1024
HEADS = 4
HEAD_DIM = 256
HALF = HEAD_DIM // 2
GRID_W = 64
MLP_GROUPS = 8
MLP_CHUNK = 128
ROPE_BASE = 10000.0
EPS = 1e-6
N_COL_GROUPS = 9

RET_CHUNK = 256
RET_HEADS_PER_STEP = 2
PROJ_ROWS = 512
PROJ_SUB_ROWS = 128
CTX_FIRST_STEP = 2
PROJ_PROLOGUE_STEPS = 10
MOD_COLS = 3072
OUT_ROWS = 1024
OUT_SUB_ROWS = 256
VMEM_LIMIT = 60 * 1024 * 1024


def _rope_tables(n):
    pos = np.arange(n)
    row = (pos // GRID_W).astype(np.float64)
    col = (pos % GRID_W).astype(np.float64)
    quarter = HALF // 2
    inv = np.power(ROPE_BASE, -np.arange(quarter, dtype=np.float64) / quarter)
    ang = np.concatenate([row[:, None] * inv, col[:, None] * inv], axis=-1)
    return np.cos(ang).astype(np.float32), np.sin(ang).astype(np.float32)


def _silu(x):
    return x * jax.nn.sigmoid(x)


def _pack_rows(w):
    return pltpu.bitcast(w.astype(BF16), jnp.uint32)


def _unpack_rows(w32):
    return pltpu.bitcast(w32, BF16)


def _mod_kernel(c_ref, cc_ref, w_ref, b_ref, ox_ref, oc_ref):
    w = w_ref[...].astype(BF16)
    mx = jnp.dot(_silu(c_ref[...]).astype(BF16), w, preferred_element_type=F32) + b_ref[...]
    for i in range(ox_ref.shape[0]):
        ox_ref[i] = mx[i:i + 1, :]
    oc_ref[...] = jnp.dot(_silu(cc_ref[...]).astype(BF16), w, preferred_element_type=F32) + b_ref[...]


def _modulation(c, c_ctx, w_mod, b_mod):
    rows, d = c.shape
    width = w_mod.shape[1]
    bn = MOD_COLS
    return pl.pallas_call(
        _mod_kernel,
        grid=(width // bn,),
        in_specs=[
            pl.BlockSpec((rows, d), lambda j: (0, 0)),
            pl.BlockSpec((1, d), lambda j: (0, 0)),
            pl.BlockSpec((d, bn), lambda j: (0, j)),
            pl.BlockSpec((1, bn), lambda j: (0, j)),
        ],
        out_specs=[pl.BlockSpec((rows, 1, bn), lambda j: (0, 0, j)),
                   pl.BlockSpec((1, bn), lambda j: (0, j))],
        out_shape=[jax.ShapeDtypeStruct((rows, 1, width), F32),
                   jax.ShapeDtypeStruct((1, width), F32)],
        name="mod",
    )(c, c_ctx, w_mod, b_mod)


def _modulated_norm(x, ng, sh, sc):
    ms = jnp.mean(x * x, axis=-1, keepdims=True)
    return x * lax.rsqrt(ms + EPS) * (ng * (1.0 + sc)) + sh


def _proj_kernel(xn_ref, shn_ref, scn_ref, ng_ref, w_ref, cos_ref, sin_ref,
                 lng_ref, lnb_ref, rg_ref, ctx_ref, shc_ref, scc_ref,
                 q_ref, k_ref, v_ref, za_ref, uz_ref, vg_ref, ga_ref, gb_ref, kc_ref, vc_ref,
                 hb_ref, wp_ref):
    step = pl.program_id(0)

    @pl.when(step < N_COL_GROUPS)
    def _():
        group = _weight_group(step)
        scale = jnp.where(group == 0, HEAD_DIM ** -0.5, 1.0)
        wp_ref[group] = _pack_rows(w_ref[...] * scale)

    @pl.when((step >= CTX_FIRST_STEP) & (step < PROJ_PROLOGUE_STEPS))
    def _():
        hc = _modulated_norm(ctx_ref[0], ng_ref[...], shc_ref[...], scc_ref[...]).astype(BF16)
        pk = jnp.dot(hc, _unpack_rows(wp_ref[1]), preferred_element_type=F32)
        pv = jnp.dot(hc, _unpack_rows(wp_ref[2]), preferred_element_type=F32)
        for h in range(HEADS):
            kc_ref[0, h] = pk[:, h * HEAD_DIM:(h + 1) * HEAD_DIM].astype(BF16)
            vc_ref[0, h] = pv[:, h * HEAD_DIM:(h + 1) * HEAD_DIM].astype(BF16)

    @pl.when(step == PROJ_PROLOGUE_STEPS - 1)
    def _():
        hb_ref[...] = _modulated_norm(xn_ref[...], ng_ref[...], shn_ref[0], scn_ref[0]).astype(BF16)

    @pl.when(step >= PROJ_PROLOGUE_STEPS)
    def _():
        _proj_rows(xn_ref, shn_ref, scn_ref, ng_ref, wp_ref, cos_ref, sin_ref, lng_ref, lnb_ref, rg_ref,
                   q_ref, k_ref, v_ref, za_ref, uz_ref, vg_ref, ga_ref, gb_ref, hb_ref)


def _weight_group(step):
    return jnp.where(step < 2, step + 1, jnp.where(step == 2, 0, jnp.minimum(step, N_COL_GROUPS - 1)))


def _proj_rows(xn_ref, shn_ref, scn_ref, ng_ref, wp_ref, cos_ref, sin_ref, lng_ref, lnb_ref, rg_ref,
               q_ref, k_ref, v_ref, za_ref, uz_ref, vg_ref, ga_ref, gb_ref, hb_ref):
    hb_next = _modulated_norm(xn_ref[...], ng_ref[...], shn_ref[0], scn_ref[0]).astype(BF16)

    for r in range(PROJ_ROWS // PROJ_SUB_ROWS):
        rows = slice(r * PROJ_SUB_ROWS, (r + 1) * PROJ_SUB_ROWS)
        cos = cos_ref[rows, :]
        sin = sin_ref[rows, :]

        def group(g):
            return jnp.dot(hb_ref[rows, :], _unpack_rows(wp_ref[g]), preferred_element_type=F32)

        def store_rope(p, o_ref, transposed=False):
            pb = p.astype(BF16)
            for h in range(HEADS):
                t1 = pb[:, h * HEAD_DIM:h * HEAD_DIM + HALF]
                t2 = pb[:, h * HEAD_DIM + HALF:(h + 1) * HEAD_DIM]
                if transposed:
                    o_ref[0, h, :HALF, rows] = (t1 * cos - t2 * sin).T
                    o_ref[0, h, HALF:, rows] = (t1 * sin + t2 * cos).T
                else:
                    o_ref[0, h, rows, :HALF] = t1 * cos - t2 * sin
                    o_ref[0, h, rows, HALF:] = t1 * sin + t2 * cos

        def store_heads(pb, o_ref):
            for h in range(HEADS):
                o_ref[0, h, rows, :] = pb[:, h * HEAD_DIM:(h + 1) * HEAD_DIM]

        store_rope(group(0), q_ref)
        store_heads(_silu(group(3)).astype(BF16) * rg_ref[...].astype(BF16), za_ref)
        uz_ref[0, rows, :] = jax.nn.gelu(group(4)).astype(BF16)
        ga_ref[0, rows, :] = jax.nn.sigmoid(group(7)).astype(BF16)
        t = jax.nn.gelu(group(5))
        mu = jnp.mean(t, axis=-1, keepdims=True)
        tc = t - mu
        tn = tc * lax.rsqrt(jnp.mean(tc * tc, axis=-1, keepdims=True) + EPS)
        vg_ref[0, rows, :] = (tn.astype(BF16) * lng_ref[...].astype(BF16)
                              + lnb_ref[...].astype(BF16))
        gb_ref[0, rows, :] = jax.nn.sigmoid(group(8)).astype(BF16)
        store_rope(group(1), k_ref, transposed=True)
        uz_ref[0, rows, :] = uz_ref[0, rows, :] * _silu(group(6)).astype(BF16)
        store_heads(group(2).astype(BF16), v_ref)

    hb_ref[...] = hb_next


def _proj(x, ctx, mod_x, mod_c, ng, w_in, cos, sin, ln_g, ln_b, ret_g):
    b, n, d = x.shape
    tm = PROJ_ROWS
    per_b = n // tm
    steps = b * per_b
    pre = PROJ_PROLOGUE_STEPS
    n_ctx = ctx.shape[1]
    ctx_b = lambda s: jnp.clip(s - CTX_FIRST_STEP, 0, b - 1)
    x2 = x.reshape(b * n, d)
    row = lambda s: jnp.maximum(s - pre, 0)
    nxt = lambda s: jnp.clip(s + 1 - pre, 0, steps - 1)
    head_out = jax.ShapeDtypeStruct((b, HEADS, n, HEAD_DIM), BF16)
    headt_out = jax.ShapeDtypeStruct((b, HEADS, HEAD_DIM, n), BF16)
    flat_out = jax.ShapeDtypeStruct((b, n, d), BF16)
    head_spec = pl.BlockSpec((1, HEADS, tm, HEAD_DIM), lambda s: (row(s) // per_b, 0, row(s) % per_b, 0))
    headt_spec = pl.BlockSpec((1, HEADS, HEAD_DIM, tm), lambda s: (row(s) // per_b, 0, 0, row(s) % per_b))
    flat_spec = pl.BlockSpec((1, tm, d), lambda s: (row(s) // per_b, row(s) % per_b, 0))
    vec = pl.BlockSpec((1, d), lambda s: (0, 0))
    ctx_spec = pl.BlockSpec((1, HEADS, n_ctx, HEAD_DIM), lambda s: (ctx_b(s), 0, 0, 0))
    ctx_out = jax.ShapeDtypeStruct((b, HEADS, n_ctx, HEAD_DIM), BF16)
    assert CTX_FIRST_STEP + b == pre
    return pl.pallas_call(
        _proj_kernel,
        grid=(pre + steps,),
        in_specs=[
            pl.BlockSpec((tm, d), lambda s: (nxt(s), 0)),
            pl.BlockSpec((1, 1, d), lambda s: (nxt(s) // per_b, 0, 0)),
            pl.BlockSpec((1, 1, d), lambda s: (nxt(s) // per_b, 0, 1)),
            vec,
            pl.BlockSpec((d, d), lambda s: (0, _weight_group(s))),
            pl.BlockSpec((tm, HALF), lambda s: (row(s) % per_b, 0)),
            pl.BlockSpec((tm, HALF), lambda s: (row(s) % per_b, 0)),
            vec, vec, vec,
            pl.BlockSpec((1, n_ctx, d), lambda s: (ctx_b(s), 0, 0)),
            pl.BlockSpec((1, d), lambda s: (0, 0)),
            pl.BlockSpec((1, d), lambda s: (0, 1)),
        ],
        out_specs=[head_spec, headt_spec, head_spec, head_spec] + [flat_spec] * 4 + [ctx_spec] * 2,
        out_shape=[head_out, headt_out, head_out, head_out] + [flat_out] * 4 + [ctx_out] * 2,
        scratch_shapes=[pltpu.VMEM((tm, d), BF16),
                        pltpu.VMEM((N_COL_GROUPS, d // 2, d), jnp.uint32)],
        compiler_params=pltpu.CompilerParams(
            dimension_semantics=("arbitrary",), vmem_limit_bytes=VMEM_LIMIT),
        name="proj",
    )(x2, mod_x, mod_x, ng, w_in, cos, sin, ln_g, ln_b, ret_g, ctx, mod_c, mod_c)


def _dot_tn(a, b):
    return lax.dot_general(a, b, (((0,), (0,)), ((), ())), preferred_element_type=F32)


def _ret_kernel(decf_ref, decb_ref, q_ref, kt_ref, v_ref, za_ref, kc_ref, vc_ref, o_ref,
                sf_ref, tf_ref, tb_ref, dmat_ref, kdf_ref, kdb_ref, qdf_ref, qdb_ref, cwf_ref, cwb_ref,
                *, n_chunks, n_ctx):
    C = RET_CHUNK
    heads = range(RET_HEADS_PER_STEP)

    def log_decay(dec_ref, hh):
        x = dec_ref[...]
        ls = jnp.minimum(x, 0.0) - jnp.log1p(jnp.exp(-jnp.abs(x)))
        lane = lax.broadcasted_iota(jnp.int32, x.shape, 1)
        pick = lane == pl.program_id(0) * RET_HEADS_PER_STEP + hh
        return jnp.sum(jnp.where(pick, ls, 0.0), axis=1, keepdims=True)

    lgf = [log_decay(decf_ref, hh) for hh in heads]
    lgb = [log_decay(decb_ref, hh) for hh in heads]

    @pl.when(pl.program_id(1) == 0)
    def _():
        ii = lax.broadcasted_iota(jnp.int32, (C, C), 0)
        jj = lax.broadcasted_iota(jnp.int32, (C, C), 1)
        diff = (ii - jj).astype(F32)
        row = lax.broadcasted_iota(jnp.int32, (C, HEAD_DIM), 0).astype(F32)
        col = lax.broadcasted_iota(jnp.int32, (HEAD_DIM, C), 1).astype(F32)
        crow = lax.broadcasted_iota(jnp.int32, (n_ctx, HEAD_DIM), 0).astype(F32)
        for hh in heads:
            dmat_ref[hh] = jnp.where(diff >= 0.0,
                                     jnp.exp(lgf[hh] * jnp.maximum(diff, 0.0)),
                                     jnp.exp(lgb[hh] * jnp.maximum(-diff, 0.0))).astype(BF16)
            kdf_ref[hh] = jnp.exp(lgf[hh] * (C - 1.0 - col)).astype(BF16)
            kdb_ref[hh] = jnp.exp(lgb[hh] * col).astype(BF16)
            qdf_ref[hh] = jnp.exp(lgf[hh] * (row + 1.0)).astype(BF16)
            qdb_ref[hh] = jnp.exp(lgb[hh] * (C - row)).astype(BF16)
            cwf_ref[hh] = jnp.exp(lgf[hh] * (n_ctx - 1.0 - crow)).astype(BF16)
            cwb_ref[hh] = jnp.exp(lgb[hh] * crow).astype(BF16)

    cdec_f = [jnp.exp(lgf[hh] * C) for hh in heads]
    cdec_b = [jnp.exp(lgb[hh] * C) for hh in heads]

    for hh in heads:
        kc = kc_ref[0, hh]
        vc = vc_ref[0, hh]
        sf_ref[hh] = _dot_tn(kc * cwf_ref[hh], vc)
        tf_ref[hh] = _dot_tn(kc * cwb_ref[hh], vc)

    for c in range(n_chunks - 1, -1, -1):
        sl = slice(c * C, (c + 1) * C)
        for hh in heads:
            tb_ref[hh, c] = tf_ref[hh].astype(BF16)
            if c > 0:
                tf_ref[hh] = cdec_b[hh] * tf_ref[hh] + jnp.dot(
                    kt_ref[0, hh, :, sl] * kdb_ref[hh], v_ref[0, hh, sl, :], preferred_element_type=F32)

    for c in range(n_chunks):
        sl = slice(c * C, (c + 1) * C)
        for hh in heads:
            qq = q_ref[0, hh, sl, :]
            kt = kt_ref[0, hh, :, sl]
            vv = v_ref[0, hh, sl, :]
            p = jnp.dot(qq, kt, preferred_element_type=F32).astype(BF16) * dmat_ref[hh]
            lhs = jnp.concatenate([p, qq * qdf_ref[hh], qq * qdb_ref[hh]], axis=1)
            rhs = jnp.concatenate([vv, sf_ref[hh].astype(BF16), tb_ref[hh, c]], axis=0)
            o = jnp.dot(lhs, rhs, preferred_element_type=F32)
            if c + 1 < n_chunks:
                sf_ref[hh] = cdec_f[hh] * sf_ref[hh] + jnp.dot(kt * kdf_ref[hh], vv,
                                                               preferred_element_type=F32)
            mu = jnp.mean(o, axis=-1, keepdims=True)
            oc = o - mu
            rstd = lax.rsqrt(jnp.mean(oc * oc, axis=-1, keepdims=True) + EPS)
            o_ref[0, hh, sl, :] = (oc * rstd).astype(BF16) * za_ref[0, hh, sl, :]


def _retention(dec_f, dec_b, q, kt, v, za, kc, vc):
    b, hh, n, dh = q.shape
    n_ctx = kc.shape[2]
    n_chunks = n // RET_CHUNK
    hp = RET_HEADS_PER_STEP
    seq = pl.BlockSpec((1, hp, n, dh), lambda j, i: (i, j, 0, 0))
    seqt = pl.BlockSpec((1, hp, dh, n), lambda j, i: (i, j, 0, 0))
    cseq = pl.BlockSpec((1, hp, n_ctx, dh), lambda j, i: (i, j, 0, 0))
    table = pltpu.VMEM((hp, RET_CHUNK, dh), BF16)
    tablet = pltpu.VMEM((hp, dh, RET_CHUNK), BF16)
    ctable = pltpu.VMEM((hp, n_ctx, dh), BF16)
    return pl.pallas_call(
        functools.partial(_ret_kernel, n_chunks=n_chunks, n_ctx=n_ctx),
        grid=(hh // hp, b),
        in_specs=[
            pl.BlockSpec(dec_f.shape, lambda j, i: (0, 0)),
            pl.BlockSpec(dec_b.shape, lambda j, i: (0, 0)),
            seq, seqt, seq, seq, cseq, cseq,
        ],
        out_specs=seq,
        out_shape=jax.ShapeDtypeStruct((b, hh, n, dh), BF16),
        scratch_shapes=[
            pltpu.VMEM((hp, dh, dh), F32),
            pltpu.VMEM((hp, dh, dh), F32),
            pltpu.VMEM((hp, n_chunks, dh, dh), BF16),
            pltpu.VMEM((hp, RET_CHUNK, RET_CHUNK), BF16),
            tablet, tablet, table, table, ctable, ctable,
        ],
        compiler_params=pltpu.CompilerParams(
            dimension_semantics=("arbitrary", "arbitrary"), vmem_limit_bytes=VMEM_LIMIT),
        name="ret",
    )(dec_f, dec_b, q, kt, v, za, kc, vc)


def _out_kernel(x_ref, ya_ref, uz_ref, vg_ref, ga_ref, gb_ref, gt_ref, ws_ref, bias_ref,
                wpa_ref, wpb_ref, wo_ref, fg_ref, o_ref, wpa_p, wpb_p, wo_p):
    @pl.when((pl.program_id(0) == 0) & (pl.program_id(1) == 0))
    def _():
        wpa_p[...] = _pack_rows(wpa_ref[...])
        wpb_p[...] = _pack_rows(wpb_ref[...])
        wo_p[...] = _pack_rows(wo_ref[...])

    tm = x_ref.shape[1]
    gw = D_MODEL // MLP_GROUPS
    n_sub = OUT_SUB_ROWS // MLP_CHUNK
    for r in range(tm // OUT_SUB_ROWS):
        base = r * OUT_SUB_ROWS
        rows = slice(base, base + OUT_SUB_ROWS)
        mixed = [[None] * MLP_GROUPS for _ in range(n_sub)]
        for g in range(MLP_GROUPS):
            cols = slice(g * gw, (g + 1) * gw)
            rhs = jnp.concatenate(
                [vg_ref[0, base + c * MLP_CHUNK:base + (c + 1) * MLP_CHUNK, cols]
                 for c in range(n_sub)], axis=1)
            m = jnp.dot(ws_ref[g], rhs, preferred_element_type=F32)
            for c in range(n_sub):
                mixed[c][g] = m[:, c * gw:(c + 1) * gw] + bias_ref[:, cols]
        mix = jnp.concatenate([jnp.concatenate(mixed[c], axis=1) for c in range(n_sub)], axis=0)
        yb = (uz_ref[0, rows, :].astype(F32) * mix).astype(BF16)
        ya = jnp.concatenate([ya_ref[0, h, rows, :] for h in range(HEADS)], axis=1)
        pa = jnp.dot(ya, _unpack_rows(wpa_p[...]), preferred_element_type=F32)
        pb = jnp.dot(yb, _unpack_rows(wpb_p[...]), preferred_element_type=F32)
        merged = (ga_ref[0, rows, :].astype(F32) * pa
                  + gb_ref[0, rows, :].astype(F32) * pb).astype(BF16)
        out = jnp.dot(merged, _unpack_rows(wo_p[...]), preferred_element_type=F32)
        xo = x_ref[0, rows, :] + gt_ref[0] * out
        ms = jnp.mean(xo * xo, axis=-1, keepdims=True)
        o_ref[0, rows, :] = xo * lax.rsqrt(ms + EPS) * fg_ref[...]


def _output(x, ya, uz, vg, ga, gb, mod_x, ws_b, bias_full, wpa, wpb, wo, fg):
    b, n, d = x.shape
    tm = OUT_ROWS
    flat = pl.BlockSpec((1, tm, d), lambda i, j: (i, j, 0))
    const2 = lambda shape: pl.BlockSpec(shape, lambda i, j: (0,) * len(shape))
    once = lambda shape: pl.BlockSpec(shape, lambda i, j: (0,) * len(shape),
                                      pipeline_mode=pl.Buffered(1))
    return pl.pallas_call(
        _out_kernel,
        grid=(b, n // tm),
        in_specs=[
            flat,
            pl.BlockSpec((1, HEADS, tm, HEAD_DIM), lambda i, j: (i, 0, j, 0)),
            flat, flat, flat, flat,
            pl.BlockSpec((1, 1, d), lambda i, j: (i, 0, 2)),
            const2(ws_b.shape), const2(bias_full.shape),
            once(wpa.shape), once(wpb.shape), once(wo.shape),
            const2((1, d)),
        ],
        out_specs=flat,
        out_shape=jax.ShapeDtypeStruct((b, n, d), F32),
        scratch_shapes=[pltpu.VMEM((d // 2, d), jnp.uint32)] * 3,
        compiler_params=pltpu.CompilerParams(
            dimension_semantics=("arbitrary", "arbitrary"), vmem_limit_bytes=VMEM_LIMIT),
        name="out",
    )(x, ya, uz, vg, ga, gb, mod_x, ws_b, bias_full, wpa, wpb, wo, fg)


def kernel(x, c, ctx, c_ctx, w_mod, b_mod, norm_g, w_in, ret_decay_fwd, ret_decay_bwd, ret_norm_g,
           mlp_ln_g, mlp_ln_b, mlp_ws, mlp_bs, w_proj_a, w_proj_b, w_out, final_norm_g):
    b, n, d = x.shape
    assert d == D_MODEL and n % RET_CHUNK == 0 and n % OUT_ROWS == 0 and n % PROJ_ROWS == 0
    assert w_mod.shape[0] == 1, "single-layer block"

    mod_x, mod_c = _modulation(c, c_ctx[None, :], w_mod[0], b_mod[0][None, :])

    ng = norm_g[0][None, :]
    cos, sin = _rope_tables(n)
    w_in2 = w_in.reshape(w_in.shape[1:])

    q, kt, v, za, uz, vg, ga, gb, kc, vc = _proj(
        x, ctx, mod_x, mod_c, ng, w_in2, jnp.asarray(cos).astype(BF16), jnp.asarray(sin).astype(BF16),
        mlp_ln_g[0][None, :], mlp_ln_b[0][None, :], ret_norm_g[0][None, :])
    ya = _retention(ret_decay_fwd.astype(F32), ret_decay_bwd.astype(F32), q, kt, v, za, kc, vc)

    bias_full = jnp.repeat(mlp_bs[0].T, d // MLP_GROUPS, axis=1)
    return _output(x, ya, uz, vg, ga, gb, mod_x, mlp_ws[0].astype(BF16), bias_full,
                   w_proj_a.reshape(d, d), w_proj_b.reshape(d, d), w_out.reshape(d, d),
                   final_norm_g[None, :])
```

```python
import functools

import numpy as np
import jax
import jax.numpy as jnp
from jax import lax
from jax.experimental import pallas as pl
from jax.experimental.pallas import tpu as pltpu

F32 = jnp.float32
BF16 = jnp.bfloat16

D_MODEL = 1024
HEADS = 4
HEAD_DIM = 256
HALF = HEAD_DIM // 2
GRID_W = 64
MLP_GROUPS = 8
MLP_CHUNK = 128
ROPE_BASE = 10000.0
EPS = 1e-6
N_COL_GROUPS = 9

RET_CHUNK = 256
RET_HEADS_PER_STEP = 2
PROJ_ROWS = 512
PROJ_SUB_ROWS = 256
CTX_FIRST_STEP = 2
PROJ_PROLOGUE_STEPS = 10
MOD_COLS = 3072
OUT_ROWS = 1024
OUT_SUB_ROWS = 256
VMEM_LIMIT = 60 * 1024 * 1024


def _rope_tables(n):
    pos = np.arange(n)
    row = (pos // GRID_W).astype(np.float64)
    col = (pos % GRID_W).astype(np.float64)
    quarter = HALF // 2
    inv = np.power(ROPE_BASE, -np.arange(quarter, dtype=np.float64) / quarter)
    ang = np.concatenate([row[:, None] * inv, col[:, None] * inv], axis=-1)
    return np.cos(ang).astype(np.float32), np.sin(ang).astype(np.float32)


def _silu(x):
    return x * jax.nn.sigmoid(x)


def _pack_rows(w):
    return pltpu.bitcast(w.astype(BF16), jnp.uint32)


def _unpack_rows(w32):
    return pltpu.bitcast(w32, BF16)


def _mod_kernel(c_ref, cc_ref, w_ref, b_ref, ox_ref, oc_ref):
    w = w_ref[...].astype(BF16)
    mx = jnp.dot(_silu(c_ref[...]).astype(BF16), w, preferred_element_type=F32) + b_ref[...]
    for i in range(ox_ref.shape[0]):
        ox_ref[i] = mx[i:i + 1, :]
    oc_ref[...] = jnp.dot(_silu(cc_ref[...]).astype(BF16), w, preferred_element_type=F32) + b_ref[...]


def _modulation(c, c_ctx, w_mod, b_mod):
    rows, d = c.shape
    width = w_mod.shape[1]
    bn = MOD_COLS
    return pl.pallas_call(
        _mod_kernel,
        grid=(width // bn,),
        in_specs=[
            pl.BlockSpec((rows, d), lambda j: (0, 0)),
            pl.BlockSpec((1, d), lambda j: (0, 0)),
            pl.BlockSpec((d, bn), lambda j: (0, j)),
            pl.BlockSpec((1, bn), lambda j: (0, j)),
        ],
        out_specs=[pl.BlockSpec((rows, 1, bn), lambda j: (0, 0, j)),
                   pl.BlockSpec((1, bn), lambda j: (0, j))],
        out_shape=[jax.ShapeDtypeStruct((rows, 1, width), F32),
                   jax.ShapeDtypeStruct((1, width), F32)],
        name="mod",
    )(c, c_ctx, w_mod, b_mod)


def _modulated_norm(x, ng, sh, sc):
    ms = jnp.mean(x * x, axis=-1, keepdims=True)
    return x * lax.rsqrt(ms + EPS) * (ng * (1.0 + sc)) + sh


def _proj_kernel(xn_ref, shn_ref, scn_ref, ng_ref, w_ref, cos_ref, sin_ref,
                 lng_ref, lnb_ref, rg_ref, ctx_ref, shc_ref, scc_ref,
                 q_ref, k_ref, v_ref, za_ref, uz_ref, vg_ref, ga_ref, gb_ref, kc_ref, vc_ref,
                 hb_ref, wp_ref):
    step = pl.program_id(0)

    @pl.when(step < N_COL_GROUPS)
    def _():
        group = _weight_group(step)
        scale = jnp.where(group == 0, HEAD_DIM ** -0.5, 1.0)
        wp_ref[group] = _pack_rows(w_ref[...] * scale)

    @pl.when((step >= CTX_FIRST_STEP) & (step < PROJ_PROLOGUE_STEPS))
    def _():
        hc = _modulated_norm(ctx_ref[0], ng_ref[...], shc_ref[...], scc_ref[...]).astype(BF16)
        pk = jnp.dot(hc, _unpack_rows(wp_ref[1]), preferred_element_type=F32)
        pv = jnp.dot(hc, _unpack_rows(wp_ref[2]), preferred_element_type=F32)
        for h in range(HEADS):
            kc_ref[0, h] = pk[:, h * HEAD_DIM:(h + 1) * HEAD_DIM].astype(BF16)
            vc_ref[0, h] = pv[:, h * HEAD_DIM:(h + 1) * HEAD_DIM].astype(BF16)

    @pl.when(step == PROJ_PROLOGUE_STEPS - 1)
    def _():
        hb_ref[...] = _modulated_norm(xn_ref[...], ng_ref[...], shn_ref[0], scn_ref[0]).astype(BF16)

    @pl.when(step >= PROJ_PROLOGUE_STEPS)
    def _():
        _proj_rows(xn_ref, shn_ref, scn_ref, ng_ref, wp_ref, cos_ref, sin_ref, lng_ref, lnb_ref, rg_ref,
                   q_ref, k_ref, v_ref, za_ref, uz_ref, vg_ref, ga_ref, gb_ref, hb_ref)


def _weight_group(step):
    return jnp.where(step < 2, step + 1, jnp.where(step == 2, 0, jnp.minimum(step, N_COL_GROUPS - 1)))


def _proj_rows(xn_ref, shn_ref, scn_ref, ng_ref, wp_ref, cos_ref, sin_ref, lng_ref, lnb_ref, rg_ref,
               q_ref, k_ref, v_ref, za_ref, uz_ref, vg_ref, ga_ref, gb_ref, hb_ref):
    hb_next = _modulated_norm(xn_ref[...], ng_ref[...], shn_ref[0], scn_ref[0]).astype(BF16)

    for r in range(PROJ_ROWS // PROJ_SUB_ROWS):
        rows = slice(r * PROJ_SUB_ROWS, (r + 1) * PROJ_SUB_ROWS)
        cos = cos_ref[rows, :]
        sin = sin_ref[rows, :]

        def group(g):
            return jnp.dot(hb_ref[rows, :], _unpack_rows(wp_ref[g]), preferred_element_type=F32)

        def store_rope(p, o_ref, transposed=False):
            pb = p.astype(BF16)
            for h in range(HEADS):
                t1 = pb[:, h * HEAD_DIM:h * HEAD_DIM + HALF]
                t2 = pb[:, h * HEAD_DIM + HALF:(h + 1) * HEAD_DIM]
                if transposed:
                    o_ref[0, h, :HALF, rows] = (t1 * cos - t2 * sin).T
                    o_ref[0, h, HALF:, rows] = (t1 * sin + t2 * cos).T
                else:
                    o_ref[0, h, rows, :HALF] = t1 * cos - t2 * sin
                    o_ref[0, h, rows, HALF:] = t1 * sin + t2 * cos

        def store_heads(pb, o_ref):
            for h in range(HEADS):
                o_ref[0, h, rows, :] = pb[:, h * HEAD_DIM:(h + 1) * HEAD_DIM]

        store_rope(group(0), q_ref)
        store_heads(_silu(group(3)).astype(BF16) * rg_ref[...].astype(BF16), za_ref)
        uz_ref[0, rows, :] = jax.nn.gelu(group(4)).astype(BF16)
        ga_ref[0, rows, :] = jax.nn.sigmoid(group(7)).astype(BF16)
        t = jax.nn.gelu(group(5))
        mu = jnp.mean(t, axis=-1, keepdims=True)
        tc = t - mu
        tn = tc * lax.rsqrt(jnp.mean(tc * tc, axis=-1, keepdims=True) + EPS)
        vg_ref[0, rows, :] = (tn.astype(BF16) * lng_ref[...].astype(BF16)
                              + lnb_ref[...].astype(BF16))
        gb_ref[0, rows, :] = jax.nn.sigmoid(group(8)).astype(BF16)
        store_rope(group(1), k_ref, transposed=True)
        uz_ref[0, rows, :] = uz_ref[0, rows, :] * _silu(group(6)).astype(BF16)
        store_heads(group(2).astype(BF16), v_ref)

    hb_ref[...] = hb_next


def _proj(x, ctx, mod_x, mod_c, ng, w_in, cos, sin, ln_g, ln_b, ret_g):
    b, n, d = x.shape
    tm = PROJ_ROWS
    per_b = n // tm
    steps = b * per_b
    pre = PROJ_PROLOGUE_STEPS
    n_ctx = ctx.shape[1]
    ctx_b = lambda s: jnp.clip(s - CTX_FIRST_STEP, 0, b - 1)
    x2 = x.reshape(b * n, d)
    row = lambda s: jnp.maximum(s - pre, 0)
    nxt = lambda s: jnp.clip(s + 1 - pre, 0, steps - 1)
    head_out = jax.ShapeDtypeStruct((b, HEADS, n, HEAD_DIM), BF16)
    headt_out = jax.ShapeDtypeStruct((b, HEADS, HEAD_DIM, n), BF16)
    flat_out = jax.ShapeDtypeStruct((b, n, d), BF16)
    head_spec = pl.BlockSpec((1, HEADS, tm, HEAD_DIM), lambda s: (row(s) // per_b, 0, row(s) % per_b, 0))
    headt_spec = pl.BlockSpec((1, HEADS, HEAD_DIM, tm), lambda s: (row(s) // per_b, 0, 0, row(s) % per_b))
    flat_spec = pl.BlockSpec((1, tm, d), lambda s: (row(s) // per_b, row(s) % per_b, 0))
    vec = pl.BlockSpec((1, d), lambda s: (0, 0))
    ctx_spec = pl.BlockSpec((1, HEADS, n_ctx, HEAD_DIM), lambda s: (ctx_b(s), 0, 0, 0))
    ctx_out = jax.ShapeDtypeStruct((b, HEADS, n_ctx, HEAD_DIM), BF16)
    assert CTX_FIRST_STEP + b == pre
    return pl.pallas_call(
        _proj_kernel,
        grid=(pre + steps,),
        in_specs=[
            pl.BlockSpec((tm, d), lambda s: (nxt(s), 0)),
            pl.BlockSpec((1, 1, d), lambda s: (nxt(s) // per_b, 0, 0)),
            pl.BlockSpec((1, 1, d), lambda s: (nxt(s) // per_b, 0, 1)),
            vec,
            pl.BlockSpec((d, d), lambda s: (0, _weight_group(s))),
            pl.BlockSpec((tm, HALF), lambda s: (row(s) % per_b, 0)),
            pl.BlockSpec((tm, HALF), lambda s: (row(s) % per_b, 0)),
            vec, vec, vec,
            pl.BlockSpec((1, n_ctx, d), lambda s: (ctx_b(s), 0, 0)),
            pl.BlockSpec((1, d), lambda s: (0, 0)),
            pl.BlockSpec((1, d), lambda s: (0, 1)),
        ],
        out_specs=[head_spec, headt_spec, head_spec, head_spec] + [flat_spec] * 4 + [ctx_spec] * 2,
        out_shape=[head_out, headt_out, head_out, head_out] + [flat_out] * 4 + [ctx_out] * 2,
        scratch_shapes=[pltpu.VMEM((tm, d), BF16),
                        pltpu.VMEM((N_COL_GROUPS, d // 2, d), jnp.uint32)],
        compiler_params=pltpu.CompilerParams(
            dimension_semantics=("arbitrary",), vmem_limit_bytes=VMEM_LIMIT),
        name="proj",
    )(x2, mod_x, mod_x, ng, w_in, cos, sin, ln_g, ln_b, ret_g, ctx, mod_c, mod_c)


def _dot_tn(a, b):
    return lax.dot_general(a, b, (((0,), (0,)), ((), ())), preferred_element_type=F32)


def _ret_kernel(decf_ref, decb_ref, q_ref, kt_ref, v_ref, za_ref, kc_ref, vc_ref, o_ref,
                sf_ref, tf_ref, tb_ref, dmat_ref, kdf_ref, kdb_ref, qdf_ref, qdb_ref, cwf_ref, cwb_ref,
                *, n_chunks, n_ctx):
    C = RET_CHUNK
    heads = range(RET_HEADS_PER_STEP)

    def log_decay(dec_ref, hh):
        x = dec_ref[...]
        ls = jnp.minimum(x, 0.0) - jnp.log1p(jnp.exp(-jnp.abs(x)))
        lane = lax.broadcasted_iota(jnp.int32, x.shape, 1)
        pick = lane == pl.program_id(0) * RET_HEADS_PER_STEP + hh
        return jnp.sum(jnp.where(pick, ls, 0.0), axis=1, keepdims=True)

    lgf = [log_decay(decf_ref, hh) for hh in heads]
    lgb = [log_decay(decb_ref, hh) for hh in heads]

    @pl.when(pl.program_id(1) == 0)
    def _():
        ii = lax.broadcasted_iota(jnp.int32, (C, C), 0)
        jj = lax.broadcasted_iota(jnp.int32, (C, C), 1)
        diff = (ii - jj).astype(F32)
        row = lax.broadcasted_iota(jnp.int32, (C, HEAD_DIM), 0).astype(F32)
        col = lax.broadcasted_iota(jnp.int32, (HEAD_DIM, C), 1).astype(F32)
        crow = lax.broadcasted_iota(jnp.int32, (n_ctx, HEAD_DIM), 0).astype(F32)
        for hh in heads:
            dmat_ref[hh] = jnp.where(diff >= 0.0,
                                     jnp.exp(lgf[hh] * jnp.maximum(diff, 0.0)),
                                     jnp.exp(lgb[hh] * jnp.maximum(-diff, 0.0))).astype(BF16)
            kdf_ref[hh] = jnp.exp(lgf[hh] * (C - 1.0 - col)).astype(BF16)
            kdb_ref[hh] = jnp.exp(lgb[hh] * col).astype(BF16)
            qdf_ref[hh] = jnp.exp(lgf[hh] * (row + 1.0)).astype(BF16)
            qdb_ref[hh] = jnp.exp(lgb[hh] * (C - row)).astype(BF16)
            cwf_ref[hh] = jnp.exp(lgf[hh] * (n_ctx - 1.0 - crow)).astype(BF16)
            cwb_ref[hh] = jnp.exp(lgb[hh] * crow).astype(BF16)

    cdec_f = [jnp.exp(lgf[hh] * C) for hh in heads]
    cdec_b = [jnp.exp(lgb[hh] * C) for hh in heads]

    for hh in heads:
        kc = kc_ref[0, hh]
        vc = vc_ref[0, hh]
        sf_ref[hh] = _dot_tn(kc * cwf_ref[hh], vc)
        tf_ref[hh] = _dot_tn(kc * cwb_ref[hh], vc)

    for c in range(n_chunks - 1, -1, -1):
        sl = slice(c * C, (c + 1) * C)
        for hh in heads:
            tb_ref[hh, c] = tf_ref[hh].astype(BF16)
            if c > 0:
                tf_ref[hh] = cdec_b[hh] * tf_ref[hh] + jnp.dot(
                    kt_ref[0, hh, :, sl] * kdb_ref[hh], v_ref[0, hh, sl, :], preferred_element_type=F32)

    for c in range(n_chunks):
        sl = slice(c * C, (c + 1) * C)
        for hh in heads:
            qq = q_ref[0, hh, sl, :]
            kt = kt_ref[0, hh, :, sl]
            vv = v_ref[0, hh, sl, :]
            p = jnp.dot(qq, kt, preferred_element_type=F32).astype(BF16) * dmat_ref[hh]
            o = (jnp.dot(p, vv, preferred_element_type=F32)
                 + jnp.dot(qq * qdf_ref[hh], sf_ref[hh].astype(BF16), preferred_element_type=F32)
                 + jnp.dot(qq * qdb_ref[hh], tb_ref[hh, c], preferred_element_type=F32))
            if c + 1 < n_chunks:
                sf_ref[hh] = cdec_f[hh] * sf_ref[hh] + jnp.dot(kt * kdf_ref[hh], vv,
                                                               preferred_element_type=F32)
            mu = jnp.mean(o, axis=-1, keepdims=True)
            oc = o - mu
            rstd = lax.rsqrt(jnp.mean(oc * oc, axis=-1, keepdims=True) + EPS)
            o_ref[0, hh, sl, :] = (oc * rstd).astype(BF16) * za_ref[0, hh, sl, :]


def _retention(dec_f, dec_b, q, kt, v, za, kc, vc):
    b, hh, n, dh = q.shape
    n_ctx = kc.shape[2]
    n_chunks = n // RET_CHUNK
    hp = RET_HEADS_PER_STEP
    seq = pl.BlockSpec((1, hp, n, dh), lambda j, i: (i, j, 0, 0))
    seqt = pl.BlockSpec((1, hp, dh, n), lambda j, i: (i, j, 0, 0))
    cseq = pl.BlockSpec((1, hp, n_ctx, dh), lambda j, i: (i, j, 0, 0))
    table = pltpu.VMEM((hp, RET_CHUNK, dh), BF16)
    tablet = pltpu.VMEM((hp, dh, RET_CHUNK), BF16)
    ctable = pltpu.VMEM((hp, n_ctx, dh), BF16)
    return pl.pallas_call(
        functools.partial(_ret_kernel, n_chunks=n_chunks, n_ctx=n_ctx),
        grid=(hh // hp, b),
        in_specs=[
            pl.BlockSpec(dec_f.shape, lambda j, i: (0, 0)),
            pl.BlockSpec(dec_b.shape, lambda j, i: (0, 0)),
            seq, seqt, seq, seq, cseq, cseq,
        ],
        out_specs=seq,
        out_shape=jax.ShapeDtypeStruct((b, hh, n, dh), BF16),
        scratch_shapes=[
            pltpu.VMEM((hp, dh, dh), F32),
            pltpu.VMEM((hp, dh, dh), F32),
            pltpu.VMEM((hp, n_chunks, dh, dh), BF16),
            pltpu.VMEM((hp, RET_CHUNK, RET_CHUNK), BF16),
            tablet, tablet, table, table, ctable, ctable,
        ],
        compiler_params=pltpu.CompilerParams(
            dimension_semantics=("arbitrary", "arbitrary"), vmem_limit_bytes=VMEM_LIMIT),
        name="ret",
    )(dec_f, dec_b, q, kt, v, za, kc, vc)


def _out_kernel(x_ref, ya_ref, uz_ref, vg_ref, ga_ref, gb_ref, gt_ref, ws_ref, bias_ref,
                wpa_ref, wpb_ref, wo_ref, fg_ref, o_ref, wpa_p, wpb_p, wo_p):
    @pl.when((pl.program_id(0) == 0) & (pl.program_id(1) == 0))
    def _():
        wpa_p[...] = _pack_rows(wpa_ref[...])
        wpb_p[...] = _pack_rows(wpb_ref[...])
        wo_p[...] = _pack_rows(wo_ref[...])

    tm = x_ref.shape[1]
    gw = D_MODEL // MLP_GROUPS
    n_sub = OUT_SUB_ROWS // MLP_CHUNK
    for r in range(tm // OUT_SUB_ROWS):
        base = r * OUT_SUB_ROWS
        rows = slice(base, base + OUT_SUB_ROWS)
        mixed = [[None] * MLP_GROUPS for _ in range(n_sub)]
        for g in range(MLP_GROUPS):
            cols = slice(g * gw, (g + 1) * gw)
            rhs = jnp.concatenate(
                [vg_ref[0, base + c * MLP_CHUNK:base + (c + 1) * MLP_CHUNK, cols]
                 for c in range(n_sub)], axis=1)
            m = jnp.dot(ws_ref[g], rhs, preferred_element_type=F32)
            for c in range(n_sub):
                mixed[c][g] = m[:, c * gw:(c + 1) * gw] + bias_ref[:, cols]
        mix = jnp.concatenate([jnp.concatenate(mixed[c], axis=1) for c in range(n_sub)], axis=0)
        yb = (uz_ref[0, rows, :].astype(F32) * mix).astype(BF16)
        ya = jnp.concatenate([ya_ref[0, h, rows, :] for h in range(HEADS)], axis=1)
        pa = jnp.dot(ya, _unpack_rows(wpa_p[...]), preferred_element_type=F32)
        pb = jnp.dot(yb, _unpack_rows(wpb_p[...]), preferred_element_type=F32)
        merged = (ga_ref[0, rows, :].astype(F32) * pa
                  + gb_ref[0, rows, :].astype(F32) * pb).astype(BF16)
        out = jnp.dot(merged, _unpack_rows(wo_p[...]), preferred_element_type=F32)
        xo = x_ref[0, rows, :] + gt_ref[0] * out
        ms = jnp.mean(xo * xo, axis=-1, keepdims=True)
        o_ref[0, rows, :] = xo * lax.rsqrt(ms + EPS) * fg_ref[...]


def _output(x, ya, uz, vg, ga, gb, mod_x, ws_b, bias_full, wpa, wpb, wo, fg):
    b, n, d = x.shape
    tm = OUT_ROWS
    flat = pl.BlockSpec((1, tm, d), lambda i, j: (i, j, 0))
    const2 = lambda shape: pl.BlockSpec(shape, lambda i, j: (0,) * len(shape))
    once = lambda shape: pl.BlockSpec(shape, lambda i, j: (0,) * len(shape),
                                      pipeline_mode=pl.Buffered(1))
    return pl.pallas_call(
        _out_kernel,
        grid=(b, n // tm),
        in_specs=[
            flat,
            pl.BlockSpec((1, HEADS, tm, HEAD_DIM), lambda i, j: (i, 0, j, 0)),
            flat, flat, flat, flat,
            pl.BlockSpec((1, 1, d), lambda i, j: (i, 0, 2)),
            const2(ws_b.shape), const2(bias_full.shape),
            once(wpa.shape), once(wpb.shape), once(wo.shape),
            const2((1, d)),
        ],
        out_specs=flat,
        out_shape=jax.ShapeDtypeStruct((b, n, d), F32),
        scratch_shapes=[pltpu.VMEM((d // 2, d), jnp.uint32)] * 3,
        compiler_params=pltpu.CompilerParams(
            dimension_semantics=("arbitrary", "arbitrary"), vmem_limit_bytes=VMEM_LIMIT),
        name="out",
    )(x, ya, uz, vg, ga, gb, mod_x, ws_b, bias_full, wpa, wpb, wo, fg)


def kernel(x, c, ctx, c_ctx, w_mod, b_mod, norm_g, w_in, ret_decay_fwd, ret_decay_bwd, ret_norm_g,
           mlp_ln_g, mlp_ln_b, mlp_ws, mlp_bs, w_proj_a, w_proj_b, w_out, final_norm_g):
    b, n, d = x.shape
    assert d == D_MODEL and n % RET_CHUNK == 0 and n % OUT_ROWS == 0 and n % PROJ_ROWS == 0
    assert w_mod.shape[0] == 1, "single-layer block"

    mod_x, mod_c = _modulation(c, c_ctx[None, :], w_mod[0], b_mod[0][None, :])

    ng = norm_g[0][None, :]
    cos, sin = _rope_tables(n)
    w_in2 = w_in.reshape(w_in.shape[1:])

    q, kt, v, za, uz, vg, ga, gb, kc, vc = _proj(
        x, ctx, mod_x, mod_c, ng, w_in2, jnp.asarray(cos).astype(BF16), jnp.asarray(sin).astype(BF16),
        mlp_ln_g[0][None, :], mlp_ln_b[0][None, :], ret_norm_g[0][None, :])
    ya = _retention(ret_decay_fwd.astype(F32), ret_decay_bwd.astype(F32), q, kt, v, za, kc, vc)

    bias_full = jnp.repeat(mlp_bs[0].T, d // MLP_GROUPS, axis=1)
    return _output(x, ya, uz, vg, ga, gb, mod_x, mlp_ws[0].astype(BF16), bias_full,
                   w_proj_a.reshape(d, d), w_proj_b.reshape(d, d), w_out.reshape(d, d),
                   final_norm_g[None, :])
```

```python
import functools

import numpy as np
import jax
import jax.numpy as jnp
from jax import lax
from jax.experimental import pallas as pl
from jax.experimental.pallas import tpu as pltpu

F32 = jnp.float32
BF16 = jnp.bfloat16

D_MODEL = 1024
HEADS = 4
HEAD_DIM = 256
HALF = HEAD_DIM // 2
GRID_W = 64
MLP_GROUPS = 8
MLP_CHUNK = 128
ROPE_BASE = 10000.0
EPS = 1e-6
N_COL_GROUPS = 9

RET_CHUNK = 256
RET_HEADS_PER_STEP = 2
PROJ_ROWS = 512
PROJ_SUB_ROWS = 256
CTX_FIRST_STEP = 2
PROJ_PROLOGUE_STEPS = 10
MOD_COLS = 3072
OUT_ROWS = 1024
OUT_SUB_ROWS = 256
VMEM_LIMIT = 60 * 1024 * 1024


def _rope_tables(n):
    pos = np.arange(n)
    row = (pos // GRID_W).astype(np.float64)
    col = (pos % GRID_W).astype(np.float64)
    quarter = HALF // 2
    inv = np.power(ROPE_BASE, -np.arange(quarter, dtype=np.float64) / quarter)
    ang = np.concatenate([row[:, None] * inv, col[:, None] * inv], axis=-1)
    return np.cos(ang).astype(np.float32), np.sin(ang).astype(np.float32)


def _silu(x):
    return x * jax.nn.sigmoid(x)


def _pack_rows(w):
    return pltpu.bitcast(w.astype(BF16), jnp.uint32)


def _unpack_rows(w32):
    return pltpu.bitcast(w32, BF16)


def _mod_kernel(c_ref, cc_ref, w_ref, b_ref, ox_ref, oc_ref):
    w = w_ref[...].astype(BF16)
    mx = jnp.dot(_silu(c_ref[...]).astype(BF16), w, preferred_element_type=F32) + b_ref[...]
    for i in range(ox_ref.shape[0]):
        ox_ref[i] = mx[i:i + 1, :]
    oc_ref[...] = jnp.dot(_silu(cc_ref[...]).astype(BF16), w, preferred_element_type=F32) + b_ref[...]


def _modulation(c, c_ctx, w_mod, b_mod):
    rows, d = c.shape
    width = w_mod.shape[1]
    bn = MOD_COLS
    return pl.pallas_call(
        _mod_kernel,
        grid=(width // bn,),
        in_specs=[
            pl.BlockSpec((rows, d), lambda j: (0, 0)),
            pl.BlockSpec((1, d), lambda j: (0, 0)),
            pl.BlockSpec((d, bn), lambda j: (0, j)),
            pl.BlockSpec((1, bn), lambda j: (0, j)),
        ],
        out_specs=[pl.BlockSpec((rows, 1, bn), lambda j: (0, 0, j)),
                   pl.BlockSpec((1, bn), lambda j: (0, j))],
        out_shape=[jax.ShapeDtypeStruct((rows, 1, width), F32),
                   jax.ShapeDtypeStruct((1, width), F32)],
        name="mod",
    )(c, c_ctx, w_mod, b_mod)


def _modulated_norm(x, ng, sh, sc):
    ms = jnp.mean(x * x, axis=-1, keepdims=True)
    return x * lax.rsqrt(ms + EPS) * (ng * (1.0 + sc)) + sh


def _proj_kernel(xn_ref, shn_ref, scn_ref, ng_ref, w_ref, cos_ref, sin_ref,
                 lng_ref, lnb_ref, rg_ref, ctx_ref, shc_ref, scc_ref,
                 q_ref, k_ref, v_ref, za_ref, uz_ref, vg_ref, ga_ref, gb_ref, kc_ref, vc_ref,
                 hb_ref, wp_ref):
    step = pl.program_id(0)

    @pl.when(step < N_COL_GROUPS)
    def _():
        group = _weight_group(step)
        scale = jnp.where(group == 0, HEAD_DIM ** -0.5, 1.0)
        wp_ref[group] = _pack_rows(w_ref[...] * scale)

    @pl.when((step >= CTX_FIRST_STEP) & (step < PROJ_PROLOGUE_STEPS))
    def _():
        hc = _modulated_norm(ctx_ref[0], ng_ref[...], shc_ref[...], scc_ref[...]).astype(BF16)
        pk = jnp.dot(hc, _unpack_rows(wp_ref[1]), preferred_element_type=F32)
        pv = jnp.dot(hc, _unpack_rows(wp_ref[2]), preferred_element_type=F32)
        for h in range(HEADS):
            kc_ref[0, h] = pk[:, h * HEAD_DIM:(h + 1) * HEAD_DIM].astype(BF16)
            vc_ref[0, h] = pv[:, h * HEAD_DIM:(h + 1) * HEAD_DIM].astype(BF16)

    @pl.when(step == PROJ_PROLOGUE_STEPS - 1)
    def _():
        hb_ref[...] = _modulated_norm(xn_ref[...], ng_ref[...], shn_ref[0], scn_ref[0]).astype(BF16)

    @pl.when(step >= PROJ_PROLOGUE_STEPS)
    def _():
        _proj_rows(xn_ref, shn_ref, scn_ref, ng_ref, wp_ref, cos_ref, sin_ref, lng_ref, lnb_ref, rg_ref,
                   q_ref, k_ref, v_ref, za_ref, uz_ref, vg_ref, ga_ref, gb_ref, hb_ref)


def _weight_group(step):
    return jnp.where(step < 2, step + 1, jnp.where(step == 2, 0, jnp.minimum(step, N_COL_GROUPS - 1)))


def _proj_rows(xn_ref, shn_ref, scn_ref, ng_ref, wp_ref, cos_ref, sin_ref, lng_ref, lnb_ref, rg_ref,
               q_ref, k_ref, v_ref, za_ref, uz_ref, vg_ref, ga_ref, gb_ref, hb_ref):
    hb_next = _modulated_norm(xn_ref[...], ng_ref[...], shn_ref[0], scn_ref[0]).astype(BF16)

    for r in range(PROJ_ROWS // PROJ_SUB_ROWS):
        rows = slice(r * PROJ_SUB_ROWS, (r + 1) * PROJ_SUB_ROWS)
        cos = cos_ref[rows, :]
        sin = sin_ref[rows, :]

        def group(g):
            return jnp.dot(hb_ref[rows, :], _unpack_rows(wp_ref[g]), preferred_element_type=F32)

        def store_rope(p, o_ref, transposed=False):
            pb = p.astype(BF16)
            for h in range(HEADS):
                t1 = pb[:, h * HEAD_DIM:h * HEAD_DIM + HALF]
                t2 = pb[:, h * HEAD_DIM + HALF:(h + 1) * HEAD_DIM]
                if transposed:
                    o_ref[0, h, :HALF, rows] = (t1 * cos - t2 * sin).T
                    o_ref[0, h, HALF:, rows] = (t1 * sin + t2 * cos).T
                else:
                    o_ref[0, h, rows, :HALF] = t1 * cos - t2 * sin
                    o_ref[0, h, rows, HALF:] = t1 * sin + t2 * cos

        def store_heads(pb, o_ref):
            for h in range(HEADS):
                o_ref[0, h, rows, :] = pb[:, h * HEAD_DIM:(h + 1) * HEAD_DIM]

        store_rope(group(0), q_ref)
        store_heads(_silu(group(3)).astype(BF16) * rg_ref[...].astype(BF16), za_ref)
        uz_ref[0, rows, :] = jax.nn.gelu(group(4)).astype(BF16)
        ga_ref[0, rows, :] = jax.nn.sigmoid(group(7)).astype(BF16)
        t = jax.nn.gelu(group(5))
        mu = jnp.mean(t, axis=-1, keepdims=True)
        tc = t - mu
        tn = tc * lax.rsqrt(jnp.mean(tc * tc, axis=-1, keepdims=True) + EPS)
        vg_ref[0, rows, :] = (tn.astype(BF16) * lng_ref[...].astype(BF16)
                              + lnb_ref[...].astype(BF16))
        gb_ref[0, rows, :] = jax.nn.sigmoid(group(8)).astype(BF16)
        store_rope(group(1), k_ref, transposed=True)
        uz_ref[0, rows, :] = uz_ref[0, rows, :] * _silu(group(6)).astype(BF16)
        store_heads(group(2).astype(BF16), v_ref)

    hb_ref[...] = hb_next


def _proj(x, ctx, mod_x, mod_c, ng, w_in, cos, sin, ln_g, ln_b, ret_g):
    b, n, d = x.shape
    tm = PROJ_ROWS
    per_b = n // tm
    steps = b * per_b
    pre = PROJ_PROLOGUE_STEPS
    n_ctx = ctx.shape[1]
    ctx_b = lambda s: jnp.clip(s - CTX_FIRST_STEP, 0, b - 1)
    x2 = x.reshape(b * n, d)
    row = lambda s: jnp.maximum(s - pre, 0)
    nxt = lambda s: jnp.clip(s + 1 - pre, 0, steps - 1)
    head_out = jax.ShapeDtypeStruct((b, HEADS, n, HEAD_DIM), BF16)
    headt_out = jax.ShapeDtypeStruct((b, HEADS, HEAD_DIM, n), BF16)
    flat_out = jax.ShapeDtypeStruct((b, n, d), BF16)
    head_spec = pl.BlockSpec((1, HEADS, tm, HEAD_DIM), lambda s: (row(s) // per_b, 0, row(s) % per_b, 0))
    headt_spec = pl.BlockSpec((1, HEADS, HEAD_DIM, tm), lambda s: (row(s) // per_b, 0, 0, row(s) % per_b))
    flat_spec = pl.BlockSpec((1, tm, d), lambda s: (row(s) // per_b, row(s) % per_b, 0))
    vec = pl.BlockSpec((1, d), lambda s: (0, 0))
    ctx_spec = pl.BlockSpec((1, HEADS, n_ctx, HEAD_DIM), lambda s: (ctx_b(s), 0, 0, 0))
    ctx_out = jax.ShapeDtypeStruct((b, HEADS, n_ctx, HEAD_DIM), BF16)
    assert CTX_FIRST_STEP + b == pre
    return pl.pallas_call(
        _proj_kernel,
        grid=(pre + steps,),
        in_specs=[
            pl.BlockSpec((tm, d), lambda s: (nxt(s), 0)),
            pl.BlockSpec((1, 1, d), lambda s: (nxt(s) // per_b, 0, 0)),
            pl.BlockSpec((1, 1, d), lambda s: (nxt(s) // per_b, 0, 1)),
            vec,
            pl.BlockSpec((d, d), lambda s: (0, _weight_group(s))),
            pl.BlockSpec((tm, HALF), lambda s: (row(s) % per_b, 0)),
            pl.BlockSpec((tm, HALF), lambda s: (row(s) % per_b, 0)),
            vec, vec, vec,
            pl.BlockSpec((1, n_ctx, d), lambda s: (ctx_b(s), 0, 0)),
            pl.BlockSpec((1, d), lambda s: (0, 0)),
            pl.BlockSpec((1, d), lambda s: (0, 1)),
        ],
        out_specs=[head_spec, headt_spec, head_spec, head_spec] + [flat_spec] * 4 + [ctx_spec] * 2,
        out_shape=[head_out, headt_out, head_out, head_out] + [flat_out] * 4 + [ctx_out] * 2,
        scratch_shapes=[pltpu.VMEM((tm, d), BF16),
                        pltpu.VMEM((N_COL_GROUPS, d // 2, d), jnp.uint32)],
        compiler_params=pltpu.CompilerParams(
            dimension_semantics=("arbitrary",), vmem_limit_bytes=VMEM_LIMIT),
        name="proj",
    )(x2, mod_x, mod_x, ng, w_in, cos, sin, ln_g, ln_b, ret_g, ctx, mod_c, mod_c)


def _dot_tn(a, b):
    return lax.dot_general(a, b, (((0,), (0,)), ((), ())), preferred_element_type=F32)


def _ret_kernel(decf_ref, decb_ref, q_ref, kt_ref, v_ref, za_ref, kc_ref, vc_ref, o_ref,
                sf_ref, tf_ref, tb_ref, sb_ref, dmat_ref, kdf_ref, kdb_ref, qdf_ref, qdb_ref, cwf_ref, cwb_ref,
                *, n_chunks, n_ctx):
    C = RET_CHUNK
    heads = range(RET_HEADS_PER_STEP)

    def log_decay(dec_ref, hh):
        x = dec_ref[...]
        ls = jnp.minimum(x, 0.0) - jnp.log1p(jnp.exp(-jnp.abs(x)))
        lane = lax.broadcasted_iota(jnp.int32, x.shape, 1)
        pick = lane == pl.program_id(0) * RET_HEADS_PER_STEP + hh
        return jnp.sum(jnp.where(pick, ls, 0.0), axis=1, keepdims=True)

    lgf = [log_decay(decf_ref, hh) for hh in heads]
    lgb = [log_decay(decb_ref, hh) for hh in heads]

    @pl.when(pl.program_id(1) == 0)
    def _():
        ii = lax.broadcasted_iota(jnp.int32, (C, C), 0)
        jj = lax.broadcasted_iota(jnp.int32, (C, C), 1)
        diff = (ii - jj).astype(F32)
        row = lax.broadcasted_iota(jnp.int32, (C, HEAD_DIM), 0).astype(F32)
        col = lax.broadcasted_iota(jnp.int32, (HEAD_DIM, C), 1).astype(F32)
        crow = lax.broadcasted_iota(jnp.int32, (n_ctx, HEAD_DIM), 0).astype(F32)
        for hh in heads:
            dmat_ref[hh] = jnp.where(diff >= 0.0,
                                     jnp.exp(lgf[hh] * jnp.maximum(diff, 0.0)),
                                     jnp.exp(lgb[hh] * jnp.maximum(-diff, 0.0))).astype(BF16)
            kdf_ref[hh] = jnp.exp(lgf[hh] * (C - 1.0 - col)).astype(BF16)
            kdb_ref[hh] = jnp.exp(lgb[hh] * col).astype(BF16)
            qdf_ref[hh] = jnp.exp(lgf[hh] * (row + 1.0)).astype(BF16)
            qdb_ref[hh] = jnp.exp(lgb[hh] * (C - row)).astype(BF16)
            cwf_ref[hh] = jnp.exp(lgf[hh] * (n_ctx - 1.0 - crow)).astype(BF16)
            cwb_ref[hh] = jnp.exp(lgb[hh] * crow).astype(BF16)

    cdec_f = [jnp.exp(lgf[hh] * C) for hh in heads]
    cdec_b = [jnp.exp(lgb[hh] * C) for hh in heads]

    for hh in heads:
        kc = kc_ref[0, hh]
        vc = vc_ref[0, hh]
        sf_ref[hh] = _dot_tn(kc * cwf_ref[hh], vc)
        tf_ref[hh] = _dot_tn(kc * cwb_ref[hh], vc)

    for c in range(n_chunks - 1, -1, -1):
        sl = slice(c * C, (c + 1) * C)
        for hh in heads:
            tb_ref[hh, c] = tf_ref[hh].astype(BF16)
            if c > 0:
                tf_ref[hh] = cdec_b[hh] * tf_ref[hh] + jnp.dot(
                    kt_ref[0, hh, :, sl] * kdb_ref[hh], v_ref[0, hh, sl, :], preferred_element_type=F32)

    for c in range(n_chunks):
        sl = slice(c * C, (c + 1) * C)
        for hh in heads:
            sb_ref[hh, c] = sf_ref[hh].astype(BF16)
            if c + 1 < n_chunks:
                sf_ref[hh] = cdec_f[hh] * sf_ref[hh] + jnp.dot(
                    kt_ref[0, hh, :, sl] * kdf_ref[hh], v_ref[0, hh, sl, :], preferred_element_type=F32)

    for c in range(n_chunks):
        sl = slice(c * C, (c + 1) * C)
        for hh in heads:
            qq = q_ref[0, hh, sl, :]
            p = (jnp.dot(qq, kt_ref[0, hh, :, sl], preferred_element_type=F32).astype(BF16)
                 * dmat_ref[hh])
            o = (jnp.dot(p, v_ref[0, hh, sl, :], preferred_element_type=F32)
                 + jnp.dot(qq * qdf_ref[hh], sb_ref[hh, c], preferred_element_type=F32)
                 + jnp.dot(qq * qdb_ref[hh], tb_ref[hh, c], preferred_element_type=F32))
            mu = jnp.mean(o, axis=-1, keepdims=True)
            oc = o - mu
            rstd = lax.rsqrt(jnp.mean(oc * oc, axis=-1, keepdims=True) + EPS)
            o_ref[0, hh, sl, :] = (oc * rstd).astype(BF16) * za_ref[0, hh, sl, :]


def _retention(dec_f, dec_b, q, kt, v, za, kc, vc):
    b, hh, n, dh = q.shape
    n_ctx = kc.shape[2]
    n_chunks = n // RET_CHUNK
    hp = RET_HEADS_PER_STEP
    seq = pl.BlockSpec((1, hp, n, dh), lambda j, i: (i, j, 0, 0))
    seqt = pl.BlockSpec((1, hp, dh, n), lambda j, i: (i, j, 0, 0))
    cseq = pl.BlockSpec((1, hp, n_ctx, dh), lambda j, i: (i, j, 0, 0))
    table = pltpu.VMEM((hp, RET_CHUNK, dh), BF16)
    tablet = pltpu.VMEM((hp, dh, RET_CHUNK), BF16)
    ctable = pltpu.VMEM((hp, n_ctx, dh), BF16)
    return pl.pallas_call(
        functools.partial(_ret_kernel, n_chunks=n_chunks, n_ctx=n_ctx),
        grid=(hh // hp, b),
        in_specs=[
            pl.BlockSpec(dec_f.shape, lambda j, i: (0, 0)),
            pl.BlockSpec(dec_b.shape, lambda j, i: (0, 0)),
            seq, seqt, seq, seq, cseq, cseq,
        ],
        out_specs=seq,
        out_shape=jax.ShapeDtypeStruct((b, hh, n, dh), BF16),
        scratch_shapes=[
            pltpu.VMEM((hp, dh, dh), F32),
            pltpu.VMEM((hp, dh, dh), F32),
            pltpu.VMEM((hp, n_chunks, dh, dh), BF16),
            pltpu.VMEM((hp, n_chunks, dh, dh), BF16),
            pltpu.VMEM((hp, RET_CHUNK, RET_CHUNK), BF16),
            tablet, tablet, table, table, ctable, ctable,
        ],
        compiler_params=pltpu.CompilerParams(
            dimension_semantics=("arbitrary", "arbitrary"), vmem_limit_bytes=VMEM_LIMIT),
        name="ret",
    )(dec_f, dec_b, q, kt, v, za, kc, vc)


def _out_kernel(x_ref, ya_ref, uz_ref, vg_ref, ga_ref, gb_ref, gt_ref, ws_ref, bias_ref,
                wpa_ref, wpb_ref, wo_ref, fg_ref, o_ref, wpa_p, wpb_p, wo_p):
    @pl.when((pl.program_id(0) == 0) & (pl.program_id(1) == 0))
    def _():
        wpa_p[...] = _pack_rows(wpa_ref[...])
        wpb_p[...] = _pack_rows(wpb_ref[...])
        wo_p[...] = _pack_rows(wo_ref[...])

    tm = x_ref.shape[1]
    gw = D_MODEL // MLP_GROUPS
    n_sub = OUT_SUB_ROWS // MLP_CHUNK
    for r in range(tm // OUT_SUB_ROWS):
        base = r * OUT_SUB_ROWS
        rows = slice(base, base + OUT_SUB_ROWS)
        mixed = [[None] * MLP_GROUPS for _ in range(n_sub)]
        for g in range(MLP_GROUPS):
            cols = slice(g * gw, (g + 1) * gw)
            rhs = jnp.concatenate(
                [vg_ref[0, base + c * MLP_CHUNK:base + (c + 1) * MLP_CHUNK, cols]
                 for c in range(n_sub)], axis=1)
            m = jnp.dot(ws_ref[g], rhs, preferred_element_type=F32)
            for c in range(n_sub):
                mixed[c][g] = m[:, c * gw:(c + 1) * gw] + bias_ref[:, cols]
        mix = jnp.concatenate([jnp.concatenate(mixed[c], axis=1) for c in range(n_sub)], axis=0)
        yb = (uz_ref[0, rows, :].astype(F32) * mix).astype(BF16)
        ya = jnp.concatenate([ya_ref[0, h, rows, :] for h in range(HEADS)], axis=1)
        pa = jnp.dot(ya, _unpack_rows(wpa_p[...]), preferred_element_type=F32)
        pb = jnp.dot(yb, _unpack_rows(wpb_p[...]), preferred_element_type=F32)
        merged = (ga_ref[0, rows, :].astype(F32) * pa
                  + gb_ref[0, rows, :].astype(F32) * pb).astype(BF16)
        out = jnp.dot(merged, _unpack_rows(wo_p[...]), preferred_element_type=F32)
        xo = x_ref[0, rows, :] + gt_ref[0] * out
        ms = jnp.mean(xo * xo, axis=-1, keepdims=True)
        o_ref[0, rows, :] = xo * lax.rsqrt(ms + EPS) * fg_ref[...]


def _output(x, ya, uz, vg, ga, gb, mod_x, ws_b, bias_full, wpa, wpb, wo, fg):
    b, n, d = x.shape
    tm = OUT_ROWS
    flat = pl.BlockSpec((1, tm, d), lambda i, j: (i, j, 0))
    const2 = lambda shape: pl.BlockSpec(shape, lambda i, j: (0,) * len(shape))
    once = lambda shape: pl.BlockSpec(shape, lambda i, j: (0,) * len(shape),
                                      pipeline_mode=pl.Buffered(1))
    return pl.pallas_call(
        _out_kernel,
        grid=(b, n // tm),
        in_specs=[
            flat,
            pl.BlockSpec((1, HEADS, tm, HEAD_DIM), lambda i, j: (i, 0, j, 0)),
            flat, flat, flat, flat,
            pl.BlockSpec((1, 1, d), lambda i, j: (i, 0, 2)),
            const2(ws_b.shape), const2(bias_full.shape),
            once(wpa.shape), once(wpb.shape), once(wo.shape),
            const2((1, d)),
        ],
        out_specs=flat,
        out_shape=jax.ShapeDtypeStruct((b, n, d), F32),
        scratch_shapes=[pltpu.VMEM((d // 2, d), jnp.uint32)] * 3,
        compiler_params=pltpu.CompilerParams(
            dimension_semantics=("arbitrary", "arbitrary"), vmem_limit_bytes=VMEM_LIMIT),
        name="out",
    )(x, ya, uz, vg, ga, gb, mod_x, ws_b, bias_full, wpa, wpb, wo, fg)


def kernel(x, c, ctx, c_ctx, w_mod, b_mod, norm_g, w_in, ret_decay_fwd, ret_decay_bwd, ret_norm_g,
           mlp_ln_g, mlp_ln_b, mlp_ws, mlp_bs, w_proj_a, w_proj_b, w_out, final_norm_g):
    b, n, d = x.shape
    assert d == D_MODEL and n % RET_CHUNK == 0 and n % OUT_ROWS == 0 and n % PROJ_ROWS == 0
    assert w_mod.shape[0] == 1, "single-layer block"

    mod_x, mod_c = _modulation(c, c_ctx[None, :], w_mod[0], b_mod[0][None, :])

    ng = norm_g[0][None, :]
    cos, sin = _rope_tables(n)
    w_in2 = w_in.reshape(w_in.shape[1:])

    q, kt, v, za, uz, vg, ga, gb, kc, vc = _proj(
        x, ctx, mod_x, mod_c, ng, w_in2, jnp.asarray(cos).astype(BF16), jnp.asarray(sin).astype(BF16),
        mlp_ln_g[0][None, :], mlp_ln_b[0][None, :], ret_norm_g[0][None, :])
    ya = _retention(ret_decay_fwd.astype(F32), ret_decay_bwd.astype(F32), q, kt, v, za, kc, vc)

    bias_full = jnp.repeat(mlp_bs[0].T, d // MLP_GROUPS, axis=1)
    return _output(x, ya, uz, vg, ga, gb, mod_x, mlp_ws[0].astype(BF16), bias_full,
                   w_proj_a.reshape(d, d), w_proj_b.reshape(d, d), w_out.reshape(d, d),
                   final_norm_g[None, :])
```

```python
import functools

import numpy as np
import jax
import jax.numpy as jnp
from jax import lax
from jax.experimental import pallas as pl
from jax.experimental.pallas import tpu as pltpu

F32 = jnp.float32
BF16 = jnp.bfloat16

D_MODEL = 1024
HEADS = 4
HEAD_DIM = 256
HALF = HEAD_DIM // 2
GRID_W = 64
MLP_GROUPS = 8
MLP_CHUNK = 128
ROPE_BASE = 10000.0
EPS = 1e-6
N_COL_GROUPS = 9

RET_CHUNK = 256
RET_HEADS_PER_STEP = 2
PROJ_ROWS = 512
PROJ_SUB_ROWS = 256
CTX_FIRST_STEP = 2
PROJ_PROLOGUE_STEPS = 10
MOD_COLS = 3072
OUT_ROWS = 1024
OUT_SUB_ROWS = 256
VMEM_LIMIT = 60 * 1024 * 1024


def _rope_tables(n):
    pos = np.arange(n)
    row = (pos // GRID_W).astype(np.float64)
    col = (pos % GRID_W).astype(np.float64)
    quarter = HALF // 2
    inv = np.power(ROPE_BASE, -np.arange(quarter, dtype=np.float64) / quarter)
    ang = np.concatenate([row[:, None] * inv, col[:, None] * inv], axis=-1)
    return np.cos(ang).astype(np.float32), np.sin(ang).astype(np.float32)


def _silu(x):
    return x * jax.nn.sigmoid(x)


def _pack_rows(w):
    return pltpu.bitcast(w.astype(BF16), jnp.uint32)


def _unpack_rows(w32):
    return pltpu.bitcast(w32, BF16)


def _mod_kernel(c_ref, cc_ref, w_ref, b_ref, ox_ref, oc_ref):
    w = w_ref[...].astype(BF16)
    mx = jnp.dot(_silu(c_ref[...]).astype(BF16), w, preferred_element_type=F32) + b_ref[...]
    for i in range(ox_ref.shape[0]):
        ox_ref[i] = mx[i:i + 1, :]
    oc_ref[...] = jnp.dot(_silu(cc_ref[...]).astype(BF16), w, preferred_element_type=F32) + b_ref[...]


def _modulation(c, c_ctx, w_mod, b_mod):
    rows, d = c.shape
    width = w_mod.shape[1]
    bn = MOD_COLS
    return pl.pallas_call(
        _mod_kernel,
        grid=(width // bn,),
        in_specs=[
            pl.BlockSpec((rows, d), lambda j: (0, 0)),
            pl.BlockSpec((1, d), lambda j: (0, 0)),
            pl.BlockSpec((d, bn), lambda j: (0, j)),
            pl.BlockSpec((1, bn), lambda j: (0, j)),
        ],
        out_specs=[pl.BlockSpec((rows, 1, bn), lambda j: (0, 0, j)),
                   pl.BlockSpec((1, bn), lambda j: (0, j))],
        out_shape=[jax.ShapeDtypeStruct((rows, 1, width), F32),
                   jax.ShapeDtypeStruct((1, width), F32)],
        name="mod",
    )(c, c_ctx, w_mod, b_mod)


def _modulated_norm(x, ng, sh, sc):
    ms = jnp.mean(x * x, axis=-1, keepdims=True)
    return x * lax.rsqrt(ms + EPS) * (ng * (1.0 + sc)) + sh


def _proj_kernel(xn_ref, shn_ref, scn_ref, ng_ref, w_ref, cos_ref, sin_ref,
                 lng_ref, lnb_ref, rg_ref, ctx_ref, shc_ref, scc_ref,
                 q_ref, k_ref, v_ref, za_ref, uz_ref, vg_ref, ga_ref, gb_ref, kc_ref, vc_ref,
                 hb_ref, wp_ref):
    step = pl.program_id(0)

    @pl.when(step < N_COL_GROUPS)
    def _():
        group = _weight_group(step)
        scale = jnp.where(group == 0, HEAD_DIM ** -0.5, 1.0)
        wp_ref[group] = _pack_rows(w_ref[...] * scale)

    @pl.when((step >= CTX_FIRST_STEP) & (step < PROJ_PROLOGUE_STEPS))
    def _():
        hc = _modulated_norm(ctx_ref[0], ng_ref[...], shc_ref[...], scc_ref[...]).astype(BF16)
        pk = jnp.dot(hc, _unpack_rows(wp_ref[1]), preferred_element_type=F32)
        pv = jnp.dot(hc, _unpack_rows(wp_ref[2]), preferred_element_type=F32)
        for h in range(HEADS):
            kc_ref[0, h] = pk[:, h * HEAD_DIM:(h + 1) * HEAD_DIM].astype(BF16)
            vc_ref[0, h] = pv[:, h * HEAD_DIM:(h + 1) * HEAD_DIM].astype(BF16)

    @pl.when(step == PROJ_PROLOGUE_STEPS - 1)
    def _():
        hb_ref[...] = _modulated_norm(xn_ref[...], ng_ref[...], shn_ref[0], scn_ref[0]).astype(BF16)

    @pl.when(step >= PROJ_PROLOGUE_STEPS)
    def _():
        _proj_rows(xn_ref, shn_ref, scn_ref, ng_ref, wp_ref, cos_ref, sin_ref, lng_ref, lnb_ref, rg_ref,
                   q_ref, k_ref, v_ref, za_ref, uz_ref, vg_ref, ga_ref, gb_ref, hb_ref)


def _weight_group(step):
    return jnp.where(step < 2, step + 1, jnp.where(step == 2, 0, jnp.minimum(step, N_COL_GROUPS - 1)))


def _proj_rows(xn_ref, shn_ref, scn_ref, ng_ref, wp_ref, cos_ref, sin_ref, lng_ref, lnb_ref, rg_ref,
               q_ref, k_ref, v_ref, za_ref, uz_ref, vg_ref, ga_ref, gb_ref, hb_ref):
    hb_next = _modulated_norm(xn_ref[...], ng_ref[...], shn_ref[0], scn_ref[0]).astype(BF16)

    for r in range(PROJ_ROWS // PROJ_SUB_ROWS):
        rows = slice(r * PROJ_SUB_ROWS, (r + 1) * PROJ_SUB_ROWS)
        cos = cos_ref[rows, :]
        sin = sin_ref[rows, :]

        def group(g):
            return jnp.dot(hb_ref[rows, :], _unpack_rows(wp_ref[g]), preferred_element_type=F32)

        def store_rope(p, o_ref, transposed=False):
            pb = p.astype(BF16)
            for h in range(HEADS):
                t1 = pb[:, h * HEAD_DIM:h * HEAD_DIM + HALF]
                t2 = pb[:, h * HEAD_DIM + HALF:(h + 1) * HEAD_DIM]
                if transposed:
                    o_ref[0, h, :HALF, rows] = (t1 * cos - t2 * sin).T
                    o_ref[0, h, HALF:, rows] = (t1 * sin + t2 * cos).T
                else:
                    o_ref[0, h, rows, :HALF] = t1 * cos - t2 * sin
                    o_ref[0, h, rows, HALF:] = t1 * sin + t2 * cos

        def store_heads(pb, o_ref):
            for h in range(HEADS):
                o_ref[0, h, rows, :] = pb[:, h * HEAD_DIM:(h + 1) * HEAD_DIM]

        store_rope(group(0), q_ref)
        store_heads(_silu(group(3)).astype(BF16) * rg_ref[...].astype(BF16), za_ref)
        uz_ref[0, rows, :] = jax.nn.gelu(group(4)).astype(BF16)
        ga_ref[0, rows, :] = jax.nn.sigmoid(group(7)).astype(BF16)
        t = jax.nn.gelu(group(5))
        mu = jnp.mean(t, axis=-1, keepdims=True)
        tc = t - mu
        tn = tc * lax.rsqrt(jnp.mean(tc * tc, axis=-1, keepdims=True) + EPS)
        vg_ref[0, rows, :] = (tn.astype(BF16) * lng_ref[...].astype(BF16)
                              + lnb_ref[...].astype(BF16))
        gb_ref[0, rows, :] = jax.nn.sigmoid(group(8)).astype(BF16)
        store_rope(group(1), k_ref, transposed=True)
        uz_ref[0, rows, :] = uz_ref[0, rows, :] * _silu(group(6)).astype(BF16)
        store_heads(group(2).astype(BF16), v_ref)

    hb_ref[...] = hb_next


def _proj(x, ctx, mod_x, mod_c, ng, w_in, cos, sin, ln_g, ln_b, ret_g):
    b, n, d = x.shape
    tm = PROJ_ROWS
    per_b = n // tm
    steps = b * per_b
    pre = PROJ_PROLOGUE_STEPS
    n_ctx = ctx.shape[1]
    ctx_b = lambda s: jnp.clip(s - CTX_FIRST_STEP, 0, b - 1)
    x2 = x.reshape(b * n, d)
    row = lambda s: jnp.maximum(s - pre, 0)
    nxt = lambda s: jnp.clip(s + 1 - pre, 0, steps - 1)
    head_out = jax.ShapeDtypeStruct((b, HEADS, n, HEAD_DIM), BF16)
    headt_out = jax.ShapeDtypeStruct((b, HEADS, HEAD_DIM, n), BF16)
    flat_out = jax.ShapeDtypeStruct((b, n, d), BF16)
    head_spec = pl.BlockSpec((1, HEADS, tm, HEAD_DIM), lambda s: (row(s) // per_b, 0, row(s) % per_b, 0))
    headt_spec = pl.BlockSpec((1, HEADS, HEAD_DIM, tm), lambda s: (row(s) // per_b, 0, 0, row(s) % per_b))
    flat_spec = pl.BlockSpec((1, tm, d), lambda s: (row(s) // per_b, row(s) % per_b, 0))
    vec = pl.BlockSpec((1, d), lambda s: (0, 0))
    ctx_spec = pl.BlockSpec((1, HEADS, n_ctx, HEAD_DIM), lambda s: (ctx_b(s), 0, 0, 0))
    ctx_out = jax.ShapeDtypeStruct((b, HEADS, n_ctx, HEAD_DIM), BF16)
    assert CTX_FIRST_STEP + b == pre
    return pl.pallas_call(
        _proj_kernel,
        grid=(pre + steps,),
        in_specs=[
            pl.BlockSpec((tm, d), lambda s: (nxt(s), 0)),
            pl.BlockSpec((1, 1, d), lambda s: (nxt(s) // per_b, 0, 0)),
            pl.BlockSpec((1, 1, d), lambda s: (nxt(s) // per_b, 0, 1)),
            vec,
            pl.BlockSpec((d, d), lambda s: (0, _weight_group(s))),
            pl.BlockSpec((tm, HALF), lambda s: (row(s) % per_b, 0)),
            pl.BlockSpec((tm, HALF), lambda s: (row(s) % per_b, 0)),
            vec, vec, vec,
            pl.BlockSpec((1, n_ctx, d), lambda s: (ctx_b(s), 0, 0)),
            pl.BlockSpec((1, d), lambda s: (0, 0)),
            pl.BlockSpec((1, d), lambda s: (0, 1)),
        ],
        out_specs=[head_spec, headt_spec, head_spec, head_spec] + [flat_spec] * 4 + [ctx_spec] * 2,
        out_shape=[head_out, headt_out, head_out, head_out] + [flat_out] * 4 + [ctx_out] * 2,
        scratch_shapes=[pltpu.VMEM((tm, d), BF16),
                        pltpu.VMEM((N_COL_GROUPS, d // 2, d), jnp.uint32)],
        compiler_params=pltpu.CompilerParams(
            dimension_semantics=("arbitrary",), vmem_limit_bytes=VMEM_LIMIT),
        name="proj",
    )(x2, mod_x, mod_x, ng, w_in, cos, sin, ln_g, ln_b, ret_g, ctx, mod_c, mod_c)


def _dot_tn(a, b):
    return lax.dot_general(a, b, (((0,), (0,)), ((), ())), preferred_element_type=F32)


def _ret_kernel(decf_ref, decb_ref, q_ref, kt_ref, v_ref, za_ref, kc_ref, vc_ref, o_ref,
                sf_ref, tf_ref, tb_ref, dmat_ref, kdf_ref, kdb_ref, qdf_ref, qdb_ref, cwf_ref, cwb_ref,
                *, n_chunks, n_ctx):
    C = RET_CHUNK
    heads = range(RET_HEADS_PER_STEP)

    def log_decay(dec_ref, hh):
        x = dec_ref[...]
        ls = jnp.minimum(x, 0.0) - jnp.log1p(jnp.exp(-jnp.abs(x)))
        lane = lax.broadcasted_iota(jnp.int32, x.shape, 1)
        pick = lane == pl.program_id(0) * RET_HEADS_PER_STEP + hh
        return jnp.sum(jnp.where(pick, ls, 0.0), axis=1, keepdims=True)

    lgf = [log_decay(decf_ref, hh) for hh in heads]
    lgb = [log_decay(decb_ref, hh) for hh in heads]

    @pl.when(pl.program_id(1) == 0)
    def _():
        ii = lax.broadcasted_iota(jnp.int32, (C, C), 0)
        jj = lax.broadcasted_iota(jnp.int32, (C, C), 1)
        diff = (ii - jj).astype(F32)
        row = lax.broadcasted_iota(jnp.int32, (C, HEAD_DIM), 0).astype(F32)
        col = lax.broadcasted_iota(jnp.int32, (HEAD_DIM, C), 1).astype(F32)
        crow = lax.broadcasted_iota(jnp.int32, (n_ctx, HEAD_DIM), 0).astype(F32)
        for hh in heads:
            dmat_ref[hh] = jnp.where(diff >= 0.0,
                                     jnp.exp(lgf[hh] * jnp.maximum(diff, 0.0)),
                                     jnp.exp(lgb[hh] * jnp.maximum(-diff, 0.0))).astype(BF16)
            kdf_ref[hh] = jnp.exp(lgf[hh] * (C - 1.0 - col)).astype(BF16)
            kdb_ref[hh] = jnp.exp(lgb[hh] * col).astype(BF16)
            qdf_ref[hh] = jnp.exp(lgf[hh] * (row + 1.0)).astype(BF16)
            qdb_ref[hh] = jnp.exp(lgb[hh] * (C - row)).astype(BF16)
            cwf_ref[hh] = jnp.exp(lgf[hh] * (n_ctx - 1.0 - crow)).astype(BF16)
            cwb_ref[hh] = jnp.exp(lgb[hh] * crow).astype(BF16)

    cdec_f = [jnp.exp(lgf[hh] * C) for hh in heads]
    cdec_b = [jnp.exp(lgb[hh] * C) for hh in heads]

    for hh in heads:
        kc = kc_ref[0, hh]
        vc = vc_ref[0, hh]
        sf_ref[hh] = _dot_tn(kc * cwf_ref[hh], vc)
        tf_ref[hh] = _dot_tn(kc * cwb_ref[hh], vc)

    for c in range(n_chunks - 1, -1, -1):
        sl = slice(c * C, (c + 1) * C)
        for hh in heads:
            tb_ref[hh, c] = tf_ref[hh].astype(BF16)
            if c > 0:
                tf_ref[hh] = cdec_b[hh] * tf_ref[hh] + jnp.dot(
                    kt_ref[0, hh, :, sl] * kdb_ref[hh], v_ref[0, hh, sl, :], preferred_element_type=F32)

    for c in range(n_chunks):
        sl = slice(c * C, (c + 1) * C)
        for hh in heads:
            qq = q_ref[0, hh, sl, :]
            kt = kt_ref[0, hh, :, sl]
            vv = v_ref[0, hh, sl, :]
            p = jnp.dot(qq, kt, preferred_element_type=F32).astype(BF16) * dmat_ref[hh]
            o = (jnp.dot(p, vv, preferred_element_type=F32)
                 + jnp.dot(qq * qdf_ref[hh], sf_ref[hh].astype(BF16), preferred_element_type=F32)
                 + jnp.dot(qq * qdb_ref[hh], tb_ref[hh, c], preferred_element_type=F32))
            if c + 1 < n_chunks:
                sf_ref[hh] = cdec_f[hh] * sf_ref[hh] + jnp.dot(kt * kdf_ref[hh], vv,
                                                               preferred_element_type=F32)
            mu = jnp.mean(o, axis=-1, keepdims=True)
            oc = o - mu
            rstd = lax.rsqrt(jnp.mean(oc * oc, axis=-1, keepdims=True) + EPS)
            o_ref[0, hh, sl, :] = (oc * rstd).astype(BF16) * za_ref[0, hh, sl, :]


def _retention(dec_f, dec_b, q, kt, v, za, kc, vc):
    b, hh, n, dh = q.shape
    n_ctx = kc.shape[2]
    n_chunks = n // RET_CHUNK
    hp = RET_HEADS_PER_STEP
    seq = pl.BlockSpec((1, hp, n, dh), lambda j, i: (i, j, 0, 0))
    seqt = pl.BlockSpec((1, hp, dh, n), lambda j, i: (i, j, 0, 0))
    cseq = pl.BlockSpec((1, hp, n_ctx, dh), lambda j, i: (i, j, 0, 0))
    table = pltpu.VMEM((hp, RET_CHUNK, dh), BF16)
    tablet = pltpu.VMEM((hp, dh, RET_CHUNK), BF16)
    ctable = pltpu.VMEM((hp, n_ctx, dh), BF16)
    return pl.pallas_call(
        functools.partial(_ret_kernel, n_chunks=n_chunks, n_ctx=n_ctx),
        grid=(hh // hp, b),
        in_specs=[
            pl.BlockSpec(dec_f.shape, lambda j, i: (0, 0)),
            pl.BlockSpec(dec_b.shape, lambda j, i: (0, 0)),
            seq, seqt, seq, seq, cseq, cseq,
        ],
        out_specs=seq,
        out_shape=jax.ShapeDtypeStruct((b, hh, n, dh), BF16),
        scratch_shapes=[
            pltpu.VMEM((hp, dh, dh), F32),
            pltpu.VMEM((hp, dh, dh), F32),
            pltpu.VMEM((hp, n_chunks, dh, dh), BF16),
            pltpu.VMEM((hp, RET_CHUNK, RET_CHUNK), BF16),
            tablet, tablet, table, table, ctable, ctable,
        ],
        compiler_params=pltpu.CompilerParams(
            dimension_semantics=("arbitrary", "arbitrary"), vmem_limit_bytes=VMEM_LIMIT),
        name="ret",
    )(dec_f, dec_b, q, kt, v, za, kc, vc)


def _out_rows(wpa_p, wpb_p, wo_p, x_ref, ya_ref, uz_ref, vg_ref, ga_ref, gb_ref, gt_ref, ws_ref, bias_ref,
              fg_ref, o_ref):
    tm = x_ref.shape[1]
    gw = D_MODEL // MLP_GROUPS
    n_sub = OUT_SUB_ROWS // MLP_CHUNK
    for r in range(tm // OUT_SUB_ROWS):
        base = r * OUT_SUB_ROWS
        rows = slice(base, base + OUT_SUB_ROWS)
        mixed = [[None] * MLP_GROUPS for _ in range(n_sub)]
        for g in range(MLP_GROUPS):
            cols = slice(g * gw, (g + 1) * gw)
            rhs = jnp.concatenate(
                [vg_ref[0, base + c * MLP_CHUNK:base + (c + 1) * MLP_CHUNK, cols]
                 for c in range(n_sub)], axis=1)
            m = jnp.dot(ws_ref[g], rhs, preferred_element_type=F32)
            for c in range(n_sub):
                mixed[c][g] = m[:, c * gw:(c + 1) * gw] + bias_ref[:, cols]
        mix = jnp.concatenate([jnp.concatenate(mixed[c], axis=1) for c in range(n_sub)], axis=0)
        yb = (uz_ref[0, rows, :].astype(F32) * mix).astype(BF16)
        ya = jnp.concatenate([ya_ref[0, h, rows, :] for h in range(HEADS)], axis=1)
        pa = jnp.dot(ya, _unpack_rows(wpa_p[...]), preferred_element_type=F32)
        pb = jnp.dot(yb, _unpack_rows(wpb_p[...]), preferred_element_type=F32)
        merged = (ga_ref[0, rows, :].astype(F32) * pa
                  + gb_ref[0, rows, :].astype(F32) * pb).astype(BF16)
        out = jnp.dot(merged, _unpack_rows(wo_p[...]), preferred_element_type=F32)
        xo = x_ref[0, rows, :] + gt_ref[0] * out
        ms = jnp.mean(xo * xo, axis=-1, keepdims=True)
        o_ref[0, rows, :] = xo * lax.rsqrt(ms + EPS) * fg_ref[...]


def _out_kernel(x_hbm, ya_hbm, uz_hbm, vg_hbm, ga_hbm, gb_hbm, gt_hbm, ws_hbm, bias_hbm,
                wpa_hbm, wpb_hbm, wo_hbm, fg_hbm, o_hbm, wpa_p, wpb_p, wo_p, stage_ref, *, b, n):
    for w_hbm, p_ref in ((wpa_hbm, wpa_p), (wpb_hbm, wpb_p), (wo_hbm, wo_p)):
        pltpu.sync_copy(w_hbm, stage_ref)
        p_ref[...] = _pack_rows(stage_ref[...])
    tm = OUT_ROWS
    d = D_MODEL
    flat = pl.BlockSpec((1, tm, d), lambda i, j: (i, j, 0))
    const2 = lambda shape: pl.BlockSpec(shape, lambda i, j: (0,) * len(shape))
    pipeline = pltpu.emit_pipeline(
        functools.partial(_out_rows, wpa_p, wpb_p, wo_p),
        grid=(b, n // tm),
        in_specs=[
            flat,
            pl.BlockSpec((1, HEADS, tm, HEAD_DIM), lambda i, j: (i, 0, j, 0)),
            flat, flat, flat, flat,
            pl.BlockSpec((1, 1, d), lambda i, j: (i, 0, 2)),
            const2(ws_hbm.shape), const2(bias_hbm.shape), const2((1, d)),
        ],
        out_specs=[flat],
    )
    pipeline(x_hbm, ya_hbm, uz_hbm, vg_hbm, ga_hbm, gb_hbm, gt_hbm, ws_hbm, bias_hbm, fg_hbm, o_hbm)


def _output(x, ya, uz, vg, ga, gb, mod_x, ws_b, bias_full, wpa, wpb, wo, fg):
    b, n, d = x.shape
    any_spec = pl.BlockSpec(memory_space=pl.ANY)
    return pl.pallas_call(
        functools.partial(_out_kernel, b=b, n=n),
        in_specs=[any_spec] * 13,
        out_specs=any_spec,
        out_shape=jax.ShapeDtypeStruct((b, n, d), F32),
        scratch_shapes=[pltpu.VMEM((d // 2, d), jnp.uint32)] * 3 + [pltpu.VMEM((d, d), F32)],
        compiler_params=pltpu.CompilerParams(vmem_limit_bytes=VMEM_LIMIT),
        name="out",
    )(x, ya, uz, vg, ga, gb, mod_x, ws_b, bias_full, wpa, wpb, wo, fg)


def kernel(x, c, ctx, c_ctx, w_mod, b_mod, norm_g, w_in, ret_decay_fwd, ret_decay_bwd, ret_norm_g,
           mlp_ln_g, mlp_ln_b, mlp_ws, mlp_bs, w_proj_a, w_proj_b, w_out, final_norm_g):
    b, n, d = x.shape
    assert d == D_MODEL and n % RET_CHUNK == 0 and n % OUT_ROWS == 0 and n % PROJ_ROWS == 0
    assert w_mod.shape[0] == 1, "single-layer block"

    mod_x, mod_c = _modulation(c, c_ctx[None, :], w_mod[0], b_mod[0][None, :])

    ng = norm_g[0][None, :]
    cos, sin = _rope_tables(n)
    w_in2 = w_in.reshape(w_in.shape[1:])

    q, kt, v, za, uz, vg, ga, gb, kc, vc = _proj(
        x, ctx, mod_x, mod_c, ng, w_in2, jnp.asarray(cos).astype(BF16), jnp.asarray(sin).astype(BF16),
        mlp_ln_g[0][None, :], mlp_ln_b[0][None, :], ret_norm_g[0][None, :])
    ya = _retention(ret_decay_fwd.astype(F32), ret_decay_bwd.astype(F32), q, kt, v, za, kc, vc)

    bias_full = jnp.repeat(mlp_bs[0].T, d // MLP_GROUPS, axis=1)
    return _output(x, ya, uz, vg, ga, gb, mod_x, mlp_ws[0].astype(BF16), bias_full,
                   w_proj_a.reshape(d, d), w_proj_b.reshape(d, d), w_out.reshape(d, d),
                   final_norm_g[None, :])
```

```python
import functools

import numpy as np
import jax
import jax.numpy as jnp
from jax import lax
from jax.experimental import pallas as pl
from jax.experimental.pallas import tpu as pltpu

F32 = jnp.float32
BF16 = jnp.bfloat16

D_MODEL = 1024
HEADS = 4
HEAD_DIM = 256
HALF = HEAD_DIM // 2
GRID_W = 64
MLP_GROUPS = 8
MLP_CHUNK = 128
ROPE_BASE = 10000.0
EPS = 1e-6
N_COL_GROUPS = 9

RET_CHUNK = 256
RET_HEADS_PER_STEP = 2
PROJ_ROWS = 512
PROJ_SUB_ROWS = 256
CTX_FIRST_STEP = 2
PROJ_PROLOGUE_STEPS = 10
MOD_COLS = 3072
OUT_ROWS = 1024
OUT_SUB_ROWS = 256
VMEM_LIMIT = 60 * 1024 * 1024


def _rope_tables(n):
    pos = np.arange(n)
    row = (pos // GRID_W).astype(np.float64)
    col = (pos % GRID_W).astype(np.float64)
    quarter = HALF // 2
    inv = np.power(ROPE_BASE, -np.arange(quarter, dtype=np.float64) / quarter)
    ang = np.concatenate([row[:, None] * inv, col[:, None] * inv], axis=-1)
    return np.cos(ang).astype(np.float32), np.sin(ang).astype(np.float32)


def _silu(x):
    return x * jax.nn.sigmoid(x)


def _pack_rows(w):
    return pltpu.bitcast(w.astype(BF16), jnp.uint32)


def _unpack_rows(w32):
    return pltpu.bitcast(w32, BF16)


def _mod_kernel(c_ref, cc_ref, w_ref, b_ref, ox_ref, oc_ref):
    w = w_ref[...].astype(BF16)
    mx = jnp.dot(_silu(c_ref[...]).astype(BF16), w, preferred_element_type=F32) + b_ref[...]
    for i in range(ox_ref.shape[0]):
        ox_ref[i] = mx[i:i + 1, :]
    oc_ref[...] = jnp.dot(_silu(cc_ref[...]).astype(BF16), w, preferred_element_type=F32) + b_ref[...]


def _modulation(c, c_ctx, w_mod, b_mod):
    rows, d = c.shape
    width = w_mod.shape[1]
    bn = MOD_COLS
    return pl.pallas_call(
        _mod_kernel,
        grid=(width // bn,),
        in_specs=[
            pl.BlockSpec((rows, d), lambda j: (0, 0)),
            pl.BlockSpec((1, d), lambda j: (0, 0)),
            pl.BlockSpec((d, bn), lambda j: (0, j)),
            pl.BlockSpec((1, bn), lambda j: (0, j)),
        ],
        out_specs=[pl.BlockSpec((rows, 1, bn), lambda j: (0, 0, j)),
                   pl.BlockSpec((1, bn), lambda j: (0, j))],
        out_shape=[jax.ShapeDtypeStruct((rows, 1, width), F32),
                   jax.ShapeDtypeStruct((1, width), F32)],
        name="mod",
    )(c, c_ctx, w_mod, b_mod)


def _modulated_norm(x, ng, sh, sc):
    ms = jnp.mean(x * x, axis=-1, keepdims=True)
    return x * lax.rsqrt(ms + EPS) * (ng * (1.0 + sc)) + sh


def _proj_kernel(xn_ref, shn_ref, scn_ref, ng_ref, w_ref, cos_ref, sin_ref,
                 lng_ref, lnb_ref, rg_ref, ctx_ref, shc_ref, scc_ref,
                 q_ref, k_ref, v_ref, za_ref, uz_ref, vg_ref, ga_ref, gb_ref, kc_ref, vc_ref,
                 hb_ref, wp_ref):
    step = pl.program_id(0)

    @pl.when(step < N_COL_GROUPS)
    def _():
        group = _weight_group(step)
        scale = jnp.where(group == 0, HEAD_DIM ** -0.5, 1.0)
        wp_ref[group] = _pack_rows(w_ref[...] * scale)

    @pl.when((step >= CTX_FIRST_STEP) & (step < PROJ_PROLOGUE_STEPS))
    def _():
        hc = _modulated_norm(ctx_ref[0], ng_ref[...], shc_ref[...], scc_ref[...]).astype(BF16)
        pk = jnp.dot(hc, _unpack_rows(wp_ref[1]), preferred_element_type=F32)
        pv = jnp.dot(hc, _unpack_rows(wp_ref[2]), preferred_element_type=F32)
        for h in range(HEADS):
            kc_ref[0, h] = pk[:, h * HEAD_DIM:(h + 1) * HEAD_DIM].astype(BF16)
            vc_ref[0, h] = pv[:, h * HEAD_DIM:(h + 1) * HEAD_DIM].astype(BF16)

    @pl.when(step == PROJ_PROLOGUE_STEPS - 1)
    def _():
        hb_ref[...] = _modulated_norm(xn_ref[...], ng_ref[...], shn_ref[0], scn_ref[0]).astype(BF16)

    @pl.when(step >= PROJ_PROLOGUE_STEPS)
    def _():
        _proj_rows(xn_ref, shn_ref, scn_ref, ng_ref, wp_ref, cos_ref, sin_ref, lng_ref, lnb_ref, rg_ref,
                   q_ref, k_ref, v_ref, za_ref, uz_ref, vg_ref, ga_ref, gb_ref, hb_ref)


def _weight_group(step):
    return jnp.where(step < 2, step + 1, jnp.where(step == 2, 0, jnp.minimum(step, N_COL_GROUPS - 1)))


def _proj_rows(xn_ref, shn_ref, scn_ref, ng_ref, wp_ref, cos_ref, sin_ref, lng_ref, lnb_ref, rg_ref,
               q_ref, k_ref, v_ref, za_ref, uz_ref, vg_ref, ga_ref, gb_ref, hb_ref):
    hb_next = _modulated_norm(xn_ref[...], ng_ref[...], shn_ref[0], scn_ref[0]).astype(BF16)

    for r in range(PROJ_ROWS // PROJ_SUB_ROWS):
        rows = slice(r * PROJ_SUB_ROWS, (r + 1) * PROJ_SUB_ROWS)
        cos = cos_ref[rows, :]
        sin = sin_ref[rows, :]

        def group(g):
            return jnp.dot(hb_ref[rows, :], _unpack_rows(wp_ref[g]), preferred_element_type=F32)

        def store_rope(p, o_ref, transposed=False):
            pb = p.astype(BF16)
            for h in range(HEADS):
                t1 = pb[:, h * HEAD_DIM:h * HEAD_DIM + HALF]
                t2 = pb[:, h * HEAD_DIM + HALF:(h + 1) * HEAD_DIM]
                if transposed:
                    o_ref[0, h, :HALF, rows] = (t1 * cos - t2 * sin).T
                    o_ref[0, h, HALF:, rows] = (t1 * sin + t2 * cos).T
                else:
                    o_ref[0, h, rows, :HALF] = t1 * cos - t2 * sin
                    o_ref[0, h, rows, HALF:] = t1 * sin + t2 * cos

        def store_heads(pb, o_ref):
            for h in range(HEADS):
                o_ref[0, h, rows, :] = pb[:, h * HEAD_DIM:(h + 1) * HEAD_DIM]

        store_rope(group(0), q_ref)
        store_heads(_silu(group(3)).astype(BF16) * rg_ref[...].astype(BF16), za_ref)
        uz_ref[0, rows, :] = jax.nn.gelu(group(4)).astype(BF16)
        ga_ref[0, rows, :] = jax.nn.sigmoid(group(7)).astype(BF16)
        t = jax.nn.gelu(group(5))
        mu = jnp.mean(t, axis=-1, keepdims=True)
        tc = t - mu
        tn = tc * lax.rsqrt(jnp.mean(tc * tc, axis=-1, keepdims=True) + EPS)
        vg_ref[0, rows, :] = (tn.astype(BF16) * lng_ref[...].astype(BF16)
                              + lnb_ref[...].astype(BF16))
        gb_ref[0, rows, :] = jax.nn.sigmoid(group(8)).astype(BF16)
        store_rope(group(1), k_ref, transposed=True)
        uz_ref[0, rows, :] = uz_ref[0, rows, :] * _silu(group(6)).astype(BF16)
        store_heads(group(2).astype(BF16), v_ref)

    hb_ref[...] = hb_next


def _proj(x, ctx, mod_x, mod_c, ng, w_in, cos, sin, ln_g, ln_b, ret_g):
    b, n, d = x.shape
    tm = PROJ_ROWS
    per_b = n // tm
    steps = b * per_b
    pre = PROJ_PROLOGUE_STEPS
    n_ctx = ctx.shape[1]
    ctx_b = lambda s: jnp.clip(s - CTX_FIRST_STEP, 0, b - 1)
    x2 = x.reshape(b * n, d)
    row = lambda s: jnp.maximum(s - pre, 0)
    nxt = lambda s: jnp.clip(s + 1 - pre, 0, steps - 1)
    head_out = jax.ShapeDtypeStruct((b, HEADS, n, HEAD_DIM), BF16)
    headt_out = jax.ShapeDtypeStruct((b, HEADS, HEAD_DIM, n), BF16)
    flat_out = jax.ShapeDtypeStruct((b, n, d), BF16)
    head_spec = pl.BlockSpec((1, HEADS, tm, HEAD_DIM), lambda s: (row(s) // per_b, 0, row(s) % per_b, 0))
    headt_spec = pl.BlockSpec((1, HEADS, HEAD_DIM, tm), lambda s: (row(s) // per_b, 0, 0, row(s) % per_b))
    flat_spec = pl.BlockSpec((1, tm, d), lambda s: (row(s) // per_b, row(s) % per_b, 0))
    vec = pl.BlockSpec((1, d), lambda s: (0, 0))
    ctx_spec = pl.BlockSpec((1, HEADS, n_ctx, HEAD_DIM), lambda s: (ctx_b(s), 0, 0, 0))
    ctx_out = jax.ShapeDtypeStruct((b, HEADS, n_ctx, HEAD_DIM), BF16)
    assert CTX_FIRST_STEP + b == pre
    return pl.pallas_call(
        _proj_kernel,
        grid=(pre + steps,),
        in_specs=[
            pl.BlockSpec((tm, d), lambda s: (nxt(s), 0)),
            pl.BlockSpec((1, 1, d), lambda s: (nxt(s) // per_b, 0, 0)),
            pl.BlockSpec((1, 1, d), lambda s: (nxt(s) // per_b, 0, 1)),
            vec,
            pl.BlockSpec((d, d), lambda s: (0, _weight_group(s))),
            pl.BlockSpec((tm, HALF), lambda s: (row(s) % per_b, 0)),
            pl.BlockSpec((tm, HALF), lambda s: (row(s) % per_b, 0)),
            vec, vec, vec,
            pl.BlockSpec((1, n_ctx, d), lambda s: (ctx_b(s), 0, 0)),
            pl.BlockSpec((1, d), lambda s: (0, 0)),
            pl.BlockSpec((1, d), lambda s: (0, 1)),
        ],
        out_specs=[head_spec, headt_spec, head_spec, head_spec] + [flat_spec] * 4 + [ctx_spec] * 2,
        out_shape=[head_out, headt_out, head_out, head_out] + [flat_out] * 4 + [ctx_out] * 2,
        scratch_shapes=[pltpu.VMEM((tm, d), BF16),
                        pltpu.VMEM((N_COL_GROUPS, d // 2, d), jnp.uint32)],
        compiler_params=pltpu.CompilerParams(
            dimension_semantics=("arbitrary",), vmem_limit_bytes=VMEM_LIMIT),
        name="proj",
    )(x2, mod_x, mod_x, ng, w_in, cos, sin, ln_g, ln_b, ret_g, ctx, mod_c, mod_c)


def _dot_tn(a, b):
    return lax.dot_general(a, b, (((0,), (0,)), ((), ())), preferred_element_type=F32)


def _ret_kernel(decf_ref, decb_ref, q_ref, kt_ref, v_ref, za_ref, kc_ref, vc_ref, o_ref,
                sf_ref, tf_ref, tb_ref, dmat_ref, kdf_ref, kdb_ref, qdf_ref, qdb_ref, cwf_ref, cwb_ref,
                *, n_chunks, n_ctx):
    C = RET_CHUNK
    heads = range(RET_HEADS_PER_STEP)

    def log_decay(dec_ref, hh):
        x = dec_ref[...]
        ls = jnp.minimum(x, 0.0) - jnp.log1p(jnp.exp(-jnp.abs(x)))
        lane = lax.broadcasted_iota(jnp.int32, x.shape, 1)
        pick = lane == pl.program_id(0) * RET_HEADS_PER_STEP + hh
        return jnp.sum(jnp.where(pick, ls, 0.0), axis=1, keepdims=True)

    lgf = [log_decay(decf_ref, hh) for hh in heads]
    lgb = [log_decay(decb_ref, hh) for hh in heads]

    @pl.when(pl.program_id(1) == 0)
    def _():
        ii = lax.broadcasted_iota(jnp.int32, (C, C), 0)
        jj = lax.broadcasted_iota(jnp.int32, (C, C), 1)
        diff = (ii - jj).astype(F32)
        row = lax.broadcasted_iota(jnp.int32, (C, HEAD_DIM), 0).astype(F32)
        col = lax.broadcasted_iota(jnp.int32, (HEAD_DIM, C), 1).astype(F32)
        crow = lax.broadcasted_iota(jnp.int32, (n_ctx, HEAD_DIM), 0).astype(F32)
        for hh in heads:
            dmat_ref[hh] = jnp.where(diff >= 0.0,
                                     jnp.exp(lgf[hh] * jnp.maximum(diff, 0.0)),
                                     jnp.exp(lgb[hh] * jnp.maximum(-diff, 0.0))).astype(BF16)
            kdf_ref[hh] = jnp.exp(lgf[hh] * (C - 1.0 - col)).astype(BF16)
            kdb_ref[hh] = jnp.exp(lgb[hh] * col).astype(BF16)
            qdf_ref[hh] = jnp.exp(lgf[hh] * (row + 1.0)).astype(BF16)
            qdb_ref[hh] = jnp.exp(lgb[hh] * (C - row)).astype(BF16)
            cwf_ref[hh] = jnp.exp(lgf[hh] * (n_ctx - 1.0 - crow)).astype(BF16)
            cwb_ref[hh] = jnp.exp(lgb[hh] * crow).astype(BF16)

    cdec_f = [jnp.exp(lgf[hh] * C) for hh in heads]
    cdec_b = [jnp.exp(lgb[hh] * C) for hh in heads]

    for hh in heads:
        kc = kc_ref[0, hh]
        vc = vc_ref[0, hh]
        sf_ref[hh] = _dot_tn(kc * cwf_ref[hh], vc)
        tf_ref[hh] = _dot_tn(kc * cwb_ref[hh], vc)

    for c in range(n_chunks - 1, -1, -1):
        sl = slice(c * C, (c + 1) * C)
        for hh in heads:
            tb_ref[hh, c] = tf_ref[hh].astype(BF16)
            if c > 0:
                tf_ref[hh] = cdec_b[hh] * tf_ref[hh] + jnp.dot(
                    kt_ref[0, hh, :, sl] * kdb_ref[hh], v_ref[0, hh, sl, :], preferred_element_type=F32)

    for c in range(n_chunks):
        sl = slice(c * C, (c + 1) * C)
        for hh in heads:
            qq = q_ref[0, hh, sl, :]
            kt = kt_ref[0, hh, :, sl]
            vv = v_ref[0, hh, sl, :]
            p = jnp.dot(qq, kt, preferred_element_type=F32).astype(BF16) * dmat_ref[hh]
            o = (jnp.dot(p, vv, preferred_element_type=F32)
                 + jnp.dot(qq * qdf_ref[hh], sf_ref[hh].astype(BF16), preferred_element_type=F32)
                 + jnp.dot(qq * qdb_ref[hh], tb_ref[hh, c], preferred_element_type=F32))
            if c + 1 < n_chunks:
                sf_ref[hh] = cdec_f[hh] * sf_ref[hh] + jnp.dot(kt * kdf_ref[hh], vv,
                                                               preferred_element_type=F32)
            mu = jnp.mean(o, axis=-1, keepdims=True)
            oc = o - mu
            rstd = lax.rsqrt(jnp.mean(oc * oc, axis=-1, keepdims=True) + EPS)
            o_ref[0, hh, sl, :] = (oc * rstd).astype(BF16) * za_ref[0, hh, sl, :]


def _retention(dec_f, dec_b, q, kt, v, za, kc, vc):
    b, hh, n, dh = q.shape
    n_ctx = kc.shape[2]
    n_chunks = n // RET_CHUNK
    hp = RET_HEADS_PER_STEP
    seq = pl.BlockSpec((1, hp, n, dh), lambda j, i: (i, j, 0, 0))
    seqt = pl.BlockSpec((1, hp, dh, n), lambda j, i: (i, j, 0, 0))
    cseq = pl.BlockSpec((1, hp, n_ctx, dh), lambda j, i: (i, j, 0, 0))
    table = pltpu.VMEM((hp, RET_CHUNK, dh), BF16)
    tablet = pltpu.VMEM((hp, dh, RET_CHUNK), BF16)
    ctable = pltpu.VMEM((hp, n_ctx, dh), BF16)
    in_specs = [
        pl.BlockSpec(dec_f.shape, lambda j, i: (0, 0)),
        pl.BlockSpec(dec_b.shape, lambda j, i: (0, 0)),
        seq, seqt, seq, seq, cseq, cseq,
    ]
    n_in = len(in_specs)

    def outer(*refs):
        pipeline = pltpu.emit_pipeline(
            functools.partial(_ret_kernel, n_chunks=n_chunks, n_ctx=n_ctx),
            grid=(hh // hp, b), in_specs=in_specs, out_specs=[seq])
        pipeline(*refs[:n_in + 1], scratches=refs[n_in + 1:])

    any_spec = pl.BlockSpec(memory_space=pl.ANY)
    return pl.pallas_call(
        outer,
        in_specs=[any_spec] * n_in,
        out_specs=any_spec,
        out_shape=jax.ShapeDtypeStruct((b, hh, n, dh), BF16),
        scratch_shapes=[
            pltpu.VMEM((hp, dh, dh), F32),
            pltpu.VMEM((hp, dh, dh), F32),
            pltpu.VMEM((hp, n_chunks, dh, dh), BF16),
            pltpu.VMEM((hp, RET_CHUNK, RET_CHUNK), BF16),
            tablet, tablet, table, table, ctable, ctable,
        ],
        compiler_params=pltpu.CompilerParams(vmem_limit_bytes=VMEM_LIMIT),
        name="ret",
    )(dec_f, dec_b, q, kt, v, za, kc, vc)


def _out_kernel(x_ref, ya_ref, uz_ref, vg_ref, ga_ref, gb_ref, gt_ref, ws_ref, bias_ref,
                wpa_ref, wpb_ref, wo_ref, fg_ref, o_ref, wpa_p, wpb_p, wo_p):
    @pl.when((pl.program_id(0) == 0) & (pl.program_id(1) == 0))
    def _():
        wpa_p[...] = _pack_rows(wpa_ref[...])
        wpb_p[...] = _pack_rows(wpb_ref[...])
        wo_p[...] = _pack_rows(wo_ref[...])

    tm = x_ref.shape[1]
    gw = D_MODEL // MLP_GROUPS
    n_sub = OUT_SUB_ROWS // MLP_CHUNK
    for r in range(tm // OUT_SUB_ROWS):
        base = r * OUT_SUB_ROWS
        rows = slice(base, base + OUT_SUB_ROWS)
        mixed = [[None] * MLP_GROUPS for _ in range(n_sub)]
        for g in range(MLP_GROUPS):
            cols = slice(g * gw, (g + 1) * gw)
            rhs = jnp.concatenate(
                [vg_ref[0, base + c * MLP_CHUNK:base + (c + 1) * MLP_CHUNK, cols]
                 for c in range(n_sub)], axis=1)
            m = jnp.dot(ws_ref[g], rhs, preferred_element_type=F32)
            for c in range(n_sub):
                mixed[c][g] = m[:, c * gw:(c + 1) * gw] + bias_ref[:, cols]
        mix = jnp.concatenate([jnp.concatenate(mixed[c], axis=1) for c in range(n_sub)], axis=0)
        yb = (uz_ref[0, rows, :].astype(F32) * mix).astype(BF16)
        ya = jnp.concatenate([ya_ref[0, h, rows, :] for h in range(HEADS)], axis=1)
        pa = jnp.dot(ya, _unpack_rows(wpa_p[...]), preferred_element_type=F32)
        pb = jnp.dot(yb, _unpack_rows(wpb_p[...]), preferred_element_type=F32)
        merged = (ga_ref[0, rows, :].astype(F32) * pa
                  + gb_ref[0, rows, :].astype(F32) * pb).astype(BF16)
        out = jnp.dot(merged, _unpack_rows(wo_p[...]), preferred_element_type=F32)
        xo = x_ref[0, rows, :] + gt_ref[0] * out
        ms = jnp.mean(xo * xo, axis=-1, keepdims=True)
        o_ref[0, rows, :] = xo * lax.rsqrt(ms + EPS) * fg_ref[...]


def _output(x, ya, uz, vg, ga, gb, mod_x, ws_b, bias_full, wpa, wpb, wo, fg):
    b, n, d = x.shape
    tm = OUT_ROWS
    flat = pl.BlockSpec((1, tm, d), lambda i, j: (i, j, 0))
    const2 = lambda shape: pl.BlockSpec(shape, lambda i, j: (0,) * len(shape))
    once = lambda shape: pl.BlockSpec(shape, lambda i, j: (0,) * len(shape),
                                      pipeline_mode=pl.Buffered(1))
    return pl.pallas_call(
        _out_kernel,
        grid=(b, n // tm),
        in_specs=[
            flat,
            pl.BlockSpec((1, HEADS, tm, HEAD_DIM), lambda i, j: (i, 0, j, 0)),
            flat, flat, flat, flat,
            pl.BlockSpec((1, 1, d), lambda i, j: (i, 0, 2)),
            const2(ws_b.shape), const2(bias_full.shape),
            once(wpa.shape), once(wpb.shape), once(wo.shape),
            const2((1, d)),
        ],
        out_specs=flat,
        out_shape=jax.ShapeDtypeStruct((b, n, d), F32),
        scratch_shapes=[pltpu.VMEM((d // 2, d), jnp.uint32)] * 3,
        compiler_params=pltpu.CompilerParams(
            dimension_semantics=("arbitrary", "arbitrary"), vmem_limit_bytes=VMEM_LIMIT),
        name="out",
    )(x, ya, uz, vg, ga, gb, mod_x, ws_b, bias_full, wpa, wpb, wo, fg)


def kernel(x, c, ctx, c_ctx, w_mod, b_mod, norm_g, w_in, ret_decay_fwd, ret_decay_bwd, ret_norm_g,
           mlp_ln_g, mlp_ln_b, mlp_ws, mlp_bs, w_proj_a, w_proj_b, w_out, final_norm_g):
    b, n, d = x.shape
    assert d == D_MODEL and n % RET_CHUNK == 0 and n % OUT_ROWS == 0 and n % PROJ_ROWS == 0
    assert w_mod.shape[0] == 1, "single-layer block"

    mod_x, mod_c = _modulation(c, c_ctx[None, :], w_mod[0], b_mod[0][None, :])

    ng = norm_g[0][None, :]
    cos, sin = _rope_tables(n)
    w_in2 = w_in.reshape(w_in.shape[1:])

    q, kt, v, za, uz, vg, ga, gb, kc, vc = _proj(
        x, ctx, mod_x, mod_c, ng, w_in2, jnp.asarray(cos).astype(BF16), jnp.asarray(sin).astype(BF16),
        mlp_ln_g[0][None, :], mlp_ln_b[0][None, :], ret_norm_g[0][None, :])
    ya = _retention(ret_decay_fwd.astype(F32), ret_decay_bwd.astype(F32), q, kt, v, za, kc, vc)

    bias_full = jnp.repeat(mlp_bs[0].T, d // MLP_GROUPS, axis=1)
    return _output(x, ya, uz, vg, ga, gb, mod_x, mlp_ws[0].astype(BF16), bias_full,
                   w_proj_a.reshape(d, d), w_proj_b.reshape(d, d), w_out.reshape(d, d),
                   final_norm_g[None, :])
```

```python
import functools

import numpy as np
import jax
import jax.numpy as jnp
from jax import lax
from jax.experimental import pallas as pl
from jax.experimental.pallas import tpu as pltpu

F32 = jnp.float32
BF16 = jnp.bfloat16

D_MODEL = 1024
HEADS = 4
HEAD_DIM = 256
HALF = HEAD_DIM // 2
GRID_W = 64
MLP_GROUPS = 8
MLP_CHUNK = 128
ROPE_BASE = 10000.0
EPS = 1e-6
N_COL_GROUPS = 9

RET_CHUNK = 256
RET_HEADS_PER_STEP = 2
PROJ_ROWS = 512
PROJ_SUB_ROWS = 256
CTX_FIRST_STEP = 2
PROJ_PROLOGUE_STEPS = 10
MOD_COLS = 3072
OUT_ROWS = 1024
OUT_SUB_ROWS = 256
VMEM_LIMIT = 60 * 1024 * 1024


def _rope_tables(n):
    pos = np.arange(n)
    row = (pos // GRID_W).astype(np.float64)
    col = (pos % GRID_W).astype(np.float64)
    quarter = HALF // 2
    inv = np.power(ROPE_BASE, -np.arange(quarter, dtype=np.float64) / quarter)
    ang = np.concatenate([row[:, None] * inv, col[:, None] * inv], axis=-1)
    return np.cos(ang).astype(np.float32), np.sin(ang).astype(np.float32)


def _silu(x):
    return x * jax.nn.sigmoid(x)


def _pack_rows(w):
    return pltpu.bitcast(w.astype(BF16), jnp.uint32)


def _unpack_rows(w32):
    return pltpu.bitcast(w32, BF16)


def _mod_kernel(c_ref, cc_ref, w_ref, b_ref, ox_ref, oc_ref):
    w = w_ref[...].astype(BF16)
    mx = jnp.dot(_silu(c_ref[...]).astype(BF16), w, preferred_element_type=F32) + b_ref[...]
    for i in range(ox_ref.shape[0]):
        ox_ref[i] = mx[i:i + 1, :]
    oc_ref[...] = jnp.dot(_silu(cc_ref[...]).astype(BF16), w, preferred_element_type=F32) + b_ref[...]


def _modulation(c, c_ctx, w_mod, b_mod):
    rows, d = c.shape
    width = w_mod.shape[1]
    bn = MOD_COLS
    return pl.pallas_call(
        _mod_kernel,
        grid=(width // bn,),
        in_specs=[
            pl.BlockSpec((rows, d), lambda j: (0, 0)),
            pl.BlockSpec((1, d), lambda j: (0, 0)),
            pl.BlockSpec((d, bn), lambda j: (0, j)),
            pl.BlockSpec((1, bn), lambda j: (0, j)),
        ],
        out_specs=[pl.BlockSpec((rows, 1, bn), lambda j: (0, 0, j)),
                   pl.BlockSpec((1, bn), lambda j: (0, j))],
        out_shape=[jax.ShapeDtypeStruct((rows, 1, width), F32),
                   jax.ShapeDtypeStruct((1, width), F32)],
        name="mod",
    )(c, c_ctx, w_mod, b_mod)


def _modulated_norm(x, ng, sh, sc):
    ms = jnp.mean(x * x, axis=-1, keepdims=True)
    return x * lax.rsqrt(ms + EPS) * (ng * (1.0 + sc)) + sh


def _proj_kernel(xn_ref, shn_ref, scn_ref, ng_ref, w_ref, cos_ref, sin_ref,
                 lng_ref, lnb_ref, rg_ref, ctx_ref, shc_ref, scc_ref,
                 q_ref, k_ref, v_ref, za_ref, uz_ref, vg_ref, ga_ref, gb_ref, kc_ref, vc_ref,
                 hb_ref, wp_ref):
    step = pl.program_id(0)

    @pl.when(step < N_COL_GROUPS)
    def _():
        group = _weight_group(step)
        scale = jnp.where(group == 0, HEAD_DIM ** -0.5, 1.0)
        wp_ref[group] = _pack_rows(w_ref[...] * scale)

    @pl.when((step >= CTX_FIRST_STEP) & (step < PROJ_PROLOGUE_STEPS))
    def _():
        hc = _modulated_norm(ctx_ref[0], ng_ref[...], shc_ref[...], scc_ref[...]).astype(BF16)
        pk = jnp.dot(hc, _unpack_rows(wp_ref[1]), preferred_element_type=F32)
        pv = jnp.dot(hc, _unpack_rows(wp_ref[2]), preferred_element_type=F32)
        for h in range(HEADS):
            kc_ref[0, h] = pk[:, h * HEAD_DIM:(h + 1) * HEAD_DIM].astype(BF16)
            vc_ref[0, h] = pv[:, h * HEAD_DIM:(h + 1) * HEAD_DIM].astype(BF16)

    @pl.when(step == PROJ_PROLOGUE_STEPS - 1)
    def _():
        hb_ref[...] = _modulated_norm(xn_ref[...], ng_ref[...], shn_ref[0], scn_ref[0]).astype(BF16)

    @pl.when(step >= PROJ_PROLOGUE_STEPS)
    def _():
        _proj_rows(xn_ref, shn_ref, scn_ref, ng_ref, wp_ref, cos_ref, sin_ref, lng_ref, lnb_ref, rg_ref,
                   q_ref, k_ref, v_ref, za_ref, uz_ref, vg_ref, ga_ref, gb_ref, hb_ref)


def _weight_group(step):
    return jnp.where(step < 2, step + 1, jnp.where(step == 2, 0, jnp.minimum(step, N_COL_GROUPS - 1)))


def _proj_rows(xn_ref, shn_ref, scn_ref, ng_ref, wp_ref, cos_ref, sin_ref, lng_ref, lnb_ref, rg_ref,
               q_ref, k_ref, v_ref, za_ref, uz_ref, vg_ref, ga_ref, gb_ref, hb_ref):
    hb_next = _modulated_norm(xn_ref[...], ng_ref[...], shn_ref[0], scn_ref[0]).astype(BF16)

    for r in range(PROJ_ROWS // PROJ_SUB_ROWS):
        rows = slice(r * PROJ_SUB_ROWS, (r + 1) * PROJ_SUB_ROWS)
        cos = cos_ref[rows, :]
        sin = sin_ref[rows, :]

        def group(g):
            return jnp.dot(hb_ref[rows, :], _unpack_rows(wp_ref[g]), preferred_element_type=F32)

        def store_rope(p, o_ref, transposed=False):
            pb = p.astype(BF16)
            for h in range(HEADS):
                t1 = pb[:, h * HEAD_DIM:h * HEAD_DIM + HALF]
                t2 = pb[:, h * HEAD_DIM + HALF:(h + 1) * HEAD_DIM]
                if transposed:
                    o_ref[0, h, :HALF, rows] = (t1 * cos - t2 * sin).T
                    o_ref[0, h, HALF:, rows] = (t1 * sin + t2 * cos).T
                else:
                    o_ref[0, h, rows, :HALF] = t1 * cos - t2 * sin
                    o_ref[0, h, rows, HALF:] = t1 * sin + t2 * cos

        def store_heads(pb, o_ref):
            for h in range(HEADS):
                o_ref[0, h, rows, :] = pb[:, h * HEAD_DIM:(h + 1) * HEAD_DIM]

        store_rope(group(0), q_ref)
        store_heads(_silu(group(3)).astype(BF16) * rg_ref[...].astype(BF16), za_ref)
        uz_ref[0, rows, :] = jax.nn.gelu(group(4)).astype(BF16)
        ga_ref[0, rows, :] = jax.nn.sigmoid(group(7)).astype(BF16)
        t = jax.nn.gelu(group(5))
        mu = jnp.mean(t, axis=-1, keepdims=True)
        tc = t - mu
        tn = tc * lax.rsqrt(jnp.mean(tc * tc, axis=-1, keepdims=True) + EPS)
        vg_ref[0, rows, :] = (tn.astype(BF16) * lng_ref[...].astype(BF16)
                              + lnb_ref[...].astype(BF16))
        gb_ref[0, rows, :] = jax.nn.sigmoid(group(8)).astype(BF16)
        store_rope(group(1), k_ref, transposed=True)
        uz_ref[0, rows, :] = uz_ref[0, rows, :] * _silu(group(6)).astype(BF16)
        store_heads(group(2).astype(BF16), v_ref)

    hb_ref[...] = hb_next


def _proj(x, ctx, mod_x, mod_c, ng, w_in, cos, sin, ln_g, ln_b, ret_g):
    b, n, d = x.shape
    tm = PROJ_ROWS
    per_b = n // tm
    steps = b * per_b
    pre = PROJ_PROLOGUE_STEPS
    n_ctx = ctx.shape[1]
    ctx_b = lambda s: jnp.clip(s - CTX_FIRST_STEP, 0, b - 1)
    x2 = x.reshape(b * n, d)
    row = lambda s: jnp.maximum(s - pre, 0)
    nxt = lambda s: jnp.clip(s + 1 - pre, 0, steps - 1)
    head_out = jax.ShapeDtypeStruct((b, HEADS, n, HEAD_DIM), BF16)
    headt_out = jax.ShapeDtypeStruct((b, HEADS, HEAD_DIM, n), BF16)
    flat_out = jax.ShapeDtypeStruct((b, n, d), BF16)
    head_spec = pl.BlockSpec((1, HEADS, tm, HEAD_DIM), lambda s: (row(s) // per_b, 0, row(s) % per_b, 0))
    headt_spec = pl.BlockSpec((1, HEADS, HEAD_DIM, tm), lambda s: (row(s) // per_b, 0, 0, row(s) % per_b))
    flat_spec = pl.BlockSpec((1, tm, d), lambda s: (row(s) // per_b, row(s) % per_b, 0))
    vec = pl.BlockSpec((1, d), lambda s: (0, 0))
    ctx_spec = pl.BlockSpec((1, HEADS, n_ctx, HEAD_DIM), lambda s: (ctx_b(s), 0, 0, 0))
    ctx_out = jax.ShapeDtypeStruct((b, HEADS, n_ctx, HEAD_DIM), BF16)
    assert CTX_FIRST_STEP + b == pre
    return pl.pallas_call(
        _proj_kernel,
        grid=(pre + steps,),
        in_specs=[
            pl.BlockSpec((tm, d), lambda s: (nxt(s), 0)),
            pl.BlockSpec((1, 1, d), lambda s: (nxt(s) // per_b, 0, 0)),
            pl.BlockSpec((1, 1, d), lambda s: (nxt(s) // per_b, 0, 1)),
            vec,
            pl.BlockSpec((d, d), lambda s: (0, _weight_group(s))),
            pl.BlockSpec((tm, HALF), lambda s: (row(s) % per_b, 0)),
            pl.BlockSpec((tm, HALF), lambda s: (row(s) % per_b, 0)),
            vec, vec, vec,
            pl.BlockSpec((1, n_ctx, d), lambda s: (ctx_b(s), 0, 0)),
            pl.BlockSpec((1, d), lambda s: (0, 0)),
            pl.BlockSpec((1, d), lambda s: (0, 1)),
        ],
        out_specs=[head_spec, headt_spec, head_spec, head_spec] + [flat_spec] * 4 + [ctx_spec] * 2,
        out_shape=[head_out, headt_out, head_out, head_out] + [flat_out] * 4 + [ctx_out] * 2,
        scratch_shapes=[pltpu.VMEM((tm, d), BF16),
                        pltpu.VMEM((N_COL_GROUPS, d // 2, d), jnp.uint32)],
        compiler_params=pltpu.CompilerParams(
            dimension_semantics=("arbitrary",), vmem_limit_bytes=VMEM_LIMIT),
        name="proj",
    )(x2, mod_x, mod_x, ng, w_in, cos, sin, ln_g, ln_b, ret_g, ctx, mod_c, mod_c)


def _dot_tn(a, b):
    return lax.dot_general(a, b, (((0,), (0,)), ((), ())), preferred_element_type=F32)


def _ret_kernel(decf_ref, decb_ref, q_ref, kt_ref, v_ref, za_ref, kc_ref, vc_ref, o_ref,
                sf_ref, tf_ref, tb_ref, dmat_ref, kdf_ref, kdb_ref, qdf_ref, qdb_ref, cwf_ref, cwb_ref,
                *, n_chunks, n_ctx):
    C = RET_CHUNK
    heads = range(RET_HEADS_PER_STEP)

    def log_decay(dec_ref, hh):
        x = dec_ref[...]
        ls = jnp.minimum(x, 0.0) - jnp.log1p(jnp.exp(-jnp.abs(x)))
        lane = lax.broadcasted_iota(jnp.int32, x.shape, 1)
        pick = lane == pl.program_id(0) * RET_HEADS_PER_STEP + hh
        return jnp.sum(jnp.where(pick, ls, 0.0), axis=1, keepdims=True)

    lgf = [log_decay(decf_ref, hh) for hh in heads]
    lgb = [log_decay(decb_ref, hh) for hh in heads]

    @pl.when(pl.program_id(1) == 0)
    def _():
        ii = lax.broadcasted_iota(jnp.int32, (C, C), 0)
        jj = lax.broadcasted_iota(jnp.int32, (C, C), 1)
        diff = (ii - jj).astype(F32)
        row = lax.broadcasted_iota(jnp.int32, (C, HEAD_DIM), 0).astype(F32)
        col = lax.broadcasted_iota(jnp.int32, (HEAD_DIM, C), 1).astype(F32)
        crow = lax.broadcasted_iota(jnp.int32, (n_ctx, HEAD_DIM), 0).astype(F32)
        for hh in heads:
            dmat_ref[hh] = jnp.where(diff >= 0.0,
                                     jnp.exp(lgf[hh] * jnp.maximum(diff, 0.0)),
                                     jnp.exp(lgb[hh] * jnp.maximum(-diff, 0.0))).astype(BF16)
            kdf_ref[hh] = jnp.exp(lgf[hh] * (C - 1.0 - col)).astype(BF16)
            kdb_ref[hh] = jnp.exp(lgb[hh] * col).astype(BF16)
            qdf_ref[hh] = jnp.exp(lgf[hh] * (row + 1.0)).astype(BF16)
            qdb_ref[hh] = jnp.exp(lgb[hh] * (C - row)).astype(BF16)
            cwf_ref[hh] = jnp.exp(lgf[hh] * (n_ctx - 1.0 - crow)).astype(BF16)
            cwb_ref[hh] = jnp.exp(lgb[hh] * crow).astype(BF16)

    cdec_f = [jnp.exp(lgf[hh] * C) for hh in heads]
    cdec_b = [jnp.exp(lgb[hh] * C) for hh in heads]

    for hh in heads:
        kc = kc_ref[0, hh]
        vc = vc_ref[0, hh]
        sf_ref[hh] = _dot_tn(kc * cwf_ref[hh], vc)
        tf_ref[hh] = _dot_tn(kc * cwb_ref[hh], vc)

    for c in range(n_chunks - 1, -1, -1):
        sl = slice(c * C, (c + 1) * C)
        for hh in heads:
            tb_ref[hh, c] = tf_ref[hh].astype(BF16)
            if c > 0:
                tf_ref[hh] = cdec_b[hh] * tf_ref[hh] + jnp.dot(
                    kt_ref[0, hh, :, sl] * kdb_ref[hh], v_ref[0, hh, sl, :], preferred_element_type=F32)

    for c in range(n_chunks):
        sl = slice(c * C, (c + 1) * C)
        for hh in heads:
            qq = q_ref[0, hh, sl, :]
            kt = kt_ref[0, hh, :, sl]
            vv = v_ref[0, hh, sl, :]
            p = jnp.dot(qq, kt, preferred_element_type=F32).astype(BF16) * dmat_ref[hh]
            o = (jnp.dot(p, vv, preferred_element_type=F32)
                 + jnp.dot(qq * qdf_ref[hh], sf_ref[hh].astype(BF16), preferred_element_type=F32)
                 + jnp.dot(qq * qdb_ref[hh], tb_ref[hh, c], preferred_element_type=F32))
            if c + 1 < n_chunks:
                sf_ref[hh] = cdec_f[hh] * sf_ref[hh] + jnp.dot(kt * kdf_ref[hh], vv,
                                                               preferred_element_type=F32)
            mu = jnp.mean(o, axis=-1, keepdims=True)
            oc = o - mu
            rstd = lax.rsqrt(jnp.mean(oc * oc, axis=-1, keepdims=True) + EPS)
            o_ref[0, hh, sl, :] = (oc * rstd).astype(BF16) * za_ref[0, hh, sl, :]


def _retention(dec_f, dec_b, q, kt, v, za, kc, vc):
    b, hh, n, dh = q.shape
    n_ctx = kc.shape[2]
    n_chunks = n // RET_CHUNK
    hp = RET_HEADS_PER_STEP
    seq = pl.BlockSpec((1, hp, n, dh), lambda j, i: (i, j, 0, 0))
    seqt = pl.BlockSpec((1, hp, dh, n), lambda j, i: (i, j, 0, 0))
    cseq = pl.BlockSpec((1, hp, n_ctx, dh), lambda j, i: (i, j, 0, 0))
    table = pltpu.VMEM((hp, RET_CHUNK, dh), BF16)
    tablet = pltpu.VMEM((hp, dh, RET_CHUNK), BF16)
    ctable = pltpu.VMEM((hp, n_ctx, dh), BF16)
    in_specs = [
        pl.BlockSpec(dec_f.shape, lambda j, i: (0, 0)),
        pl.BlockSpec(dec_b.shape, lambda j, i: (0, 0)),
        seq, seqt, seq, seq, cseq, cseq,
    ]
    n_in = len(in_specs)

    def outer(*refs):
        pipeline = pltpu.emit_pipeline(
            functools.partial(_ret_kernel, n_chunks=n_chunks, n_ctx=n_ctx),
            grid=(hh // hp, b), in_specs=in_specs, out_specs=[seq])
        pipeline(*refs[:n_in + 1], scratches=refs[n_in + 1:])

    any_spec = pl.BlockSpec(memory_space=pl.ANY)
    return pl.pallas_call(
        outer,
        in_specs=[any_spec] * n_in,
        out_specs=any_spec,
        out_shape=jax.ShapeDtypeStruct((b, hh, n, dh), BF16),
        scratch_shapes=[
            pltpu.VMEM((hp, dh, dh), F32),
            pltpu.VMEM((hp, dh, dh), F32),
            pltpu.VMEM((hp, n_chunks, dh, dh), BF16),
            pltpu.VMEM((hp, RET_CHUNK, RET_CHUNK), BF16),
            tablet, tablet, table, table, ctable, ctable,
        ],
        compiler_params=pltpu.CompilerParams(vmem_limit_bytes=VMEM_LIMIT),
        name="ret",
    )(dec_f, dec_b, q, kt, v, za, kc, vc)


def _out_weight_copies(w_hbms, stage_ref, sem):
    return [pltpu.make_async_copy(w, stage_ref.at[k], sem.at[k]) for k, w in enumerate(w_hbms)]


def _out_rows(w_hbms, packed, stage_ref, sem, x_ref, ya_ref, uz_ref, vg_ref, ga_ref, gb_ref, gt_ref,
              ws_ref, bias_ref, fg_ref, o_ref):
    @pl.when((pl.program_id(0) == 0) & (pl.program_id(1) == 0))
    def _():
        for k, copy in enumerate(_out_weight_copies(w_hbms, stage_ref, sem)):
            copy.wait()
            packed[k][...] = _pack_rows(stage_ref[k])

    wpa_p, wpb_p, wo_p = packed
    tm = x_ref.shape[1]
    gw = D_MODEL // MLP_GROUPS
    n_sub = OUT_SUB_ROWS // MLP_CHUNK
    for r in range(tm // OUT_SUB_ROWS):
        base = r * OUT_SUB_ROWS
        rows = slice(base, base + OUT_SUB_ROWS)
        mixed = [[None] * MLP_GROUPS for _ in range(n_sub)]
        for g in range(MLP_GROUPS):
            cols = slice(g * gw, (g + 1) * gw)
            rhs = jnp.concatenate(
                [vg_ref[0, base + c * MLP_CHUNK:base + (c + 1) * MLP_CHUNK, cols]
                 for c in range(n_sub)], axis=1)
            m = jnp.dot(ws_ref[g], rhs, preferred_element_type=F32)
            for c in range(n_sub):
                mixed[c][g] = m[:, c * gw:(c + 1) * gw] + bias_ref[:, cols]
        mix = jnp.concatenate([jnp.concatenate(mixed[c], axis=1) for c in range(n_sub)], axis=0)
        yb = (uz_ref[0, rows, :].astype(F32) * mix).astype(BF16)
        ya = jnp.concatenate([ya_ref[0, h, rows, :] for h in range(HEADS)], axis=1)
        pa = jnp.dot(ya, _unpack_rows(wpa_p[...]), preferred_element_type=F32)
        pb = jnp.dot(yb, _unpack_rows(wpb_p[...]), preferred_element_type=F32)
        merged = (ga_ref[0, rows, :].astype(F32) * pa
                  + gb_ref[0, rows, :].astype(F32) * pb).astype(BF16)
        out = jnp.dot(merged, _unpack_rows(wo_p[...]), preferred_element_type=F32)
        xo = x_ref[0, rows, :] + gt_ref[0] * out
        ms = jnp.mean(xo * xo, axis=-1, keepdims=True)
        o_ref[0, rows, :] = xo * lax.rsqrt(ms + EPS) * fg_ref[...]


def _out_kernel(x_hbm, ya_hbm, uz_hbm, vg_hbm, ga_hbm, gb_hbm, gt_hbm, ws_hbm, bias_hbm,
                wpa_hbm, wpb_hbm, wo_hbm, fg_hbm, o_hbm, wpa_p, wpb_p, wo_p, stage_ref, sem, *, b, n):
    w_hbms = (wpa_hbm, wpb_hbm, wo_hbm)
    for copy in _out_weight_copies(w_hbms, stage_ref, sem):
        copy.start()
    tm = OUT_ROWS
    d = D_MODEL
    flat = pl.BlockSpec((1, tm, d), lambda i, j: (i, j, 0))
    const2 = lambda shape: pl.BlockSpec(shape, lambda i, j: (0,) * len(shape))
    pipeline = pltpu.emit_pipeline(
        functools.partial(_out_rows, w_hbms, (wpa_p, wpb_p, wo_p), stage_ref, sem),
        grid=(b, n // tm),
        in_specs=[
            flat,
            pl.BlockSpec((1, HEADS, tm, HEAD_DIM), lambda i, j: (i, 0, j, 0)),
            flat, flat, flat, flat,
            pl.BlockSpec((1, 1, d), lambda i, j: (i, 0, 2)),
            const2(ws_hbm.shape), const2(bias_hbm.shape), const2((1, d)),
        ],
        out_specs=[flat],
    )
    pipeline(x_hbm, ya_hbm, uz_hbm, vg_hbm, ga_hbm, gb_hbm, gt_hbm, ws_hbm, bias_hbm, fg_hbm, o_hbm)


def _output(x, ya, uz, vg, ga, gb, mod_x, ws_b, bias_full, wpa, wpb, wo, fg):
    b, n, d = x.shape
    any_spec = pl.BlockSpec(memory_space=pl.ANY)
    return pl.pallas_call(
        functools.partial(_out_kernel, b=b, n=n),
        in_specs=[any_spec] * 13,
        out_specs=any_spec,
        out_shape=jax.ShapeDtypeStruct((b, n, d), F32),
        scratch_shapes=[pltpu.VMEM((d // 2, d), jnp.uint32)] * 3
        + [pltpu.VMEM((3, d, d), F32), pltpu.SemaphoreType.DMA((3,))],
        compiler_params=pltpu.CompilerParams(vmem_limit_bytes=VMEM_LIMIT),
        name="out",
    )(x, ya, uz, vg, ga, gb, mod_x, ws_b, bias_full, wpa, wpb, wo, fg)


def kernel(x, c, ctx, c_ctx, w_mod, b_mod, norm_g, w_in, ret_decay_fwd, ret_decay_bwd, ret_norm_g,
           mlp_ln_g, mlp_ln_b, mlp_ws, mlp_bs, w_proj_a, w_proj_b, w_out, final_norm_g):
    b, n, d = x.shape
    assert d == D_MODEL and n % RET_CHUNK == 0 and n % OUT_ROWS == 0 and n % PROJ_ROWS == 0
    assert w_mod.shape[0] == 1, "single-layer block"

    mod_x, mod_c = _modulation(c, c_ctx[None, :], w_mod[0], b_mod[0][None, :])

    ng = norm_g[0][None, :]
    cos, sin = _rope_tables(n)
    w_in2 = w_in.reshape(w_in.shape[1:])

    q, kt, v, za, uz, vg, ga, gb, kc, vc = _proj(
        x, ctx, mod_x, mod_c, ng, w_in2, jnp.asarray(cos).astype(BF16), jnp.asarray(sin).astype(BF16),
        mlp_ln_g[0][None, :], mlp_ln_b[0][None, :], ret_norm_g[0][None, :])
    ya = _retention(ret_decay_fwd.astype(F32), ret_decay_bwd.astype(F32), q, kt, v, za, kc, vc)

    bias_full = jnp.repeat(mlp_bs[0].T, d // MLP_GROUPS, axis=1)
    return _output(x, ya, uz, vg, ga, gb, mod_x, mlp_ws[0].astype(BF16), bias_full,
                   w_proj_a.reshape(d, d), w_proj_b.reshape(d, d), w_out.reshape(d, d),
                   final_norm_g[None, :])
```

```python
import functools

import numpy as np
import jax
import jax.numpy as jnp
from jax import lax
from jax.experimental import pallas as pl
from jax.experimental.pallas import tpu as pltpu

F32 = jnp.float32
BF16 = jnp.bfloat16

D_MODEL = 1024
HEADS = 4
HEAD_DIM = 256
HALF = HEAD_DIM // 2
GRID_W = 64
MLP_GROUPS = 8
MLP_CHUNK = 128
ROPE_BASE = 10000.0
EPS = 1e-6
N_COL_GROUPS = 9

RET_CHUNK = 256
RET_HEADS_PER_STEP = 2
PROJ_ROWS = 512
PROJ_SUB_ROWS = 256
CTX_FIRST_STEP = 2
PROJ_PROLOGUE_STEPS = 10
MOD_COLS = 3072
OUT_ROWS = 1024
OUT_SUB_ROWS = 256
VMEM_LIMIT = 60 * 1024 * 1024


def _rope_tables(n):
    pos = np.arange(n)
    row = (pos // GRID_W).astype(np.float64)
    col = (pos % GRID_W).astype(np.float64)
    quarter = HALF // 2
    inv = np.power(ROPE_BASE, -np.arange(quarter, dtype=np.float64) / quarter)
    ang = np.concatenate([row[:, None] * inv, col[:, None] * inv], axis=-1)
    return np.cos(ang).astype(np.float32), np.sin(ang).astype(np.float32)


def _silu(x):
    return x * jax.nn.sigmoid(x)


def _pack_rows(w):
    return pltpu.bitcast(w.astype(BF16), jnp.uint32)


def _unpack_rows(w32):
    return pltpu.bitcast(w32, BF16)


def _mod_kernel(c_ref, cc_ref, w_ref, b_ref, ox_ref, oc_ref):
    w = w_ref[...].astype(BF16)
    mx = jnp.dot(_silu(c_ref[...]).astype(BF16), w, preferred_element_type=F32) + b_ref[...]
    for i in range(ox_ref.shape[0]):
        ox_ref[i] = mx[i:i + 1, :]
    oc_ref[...] = jnp.dot(_silu(cc_ref[...]).astype(BF16), w, preferred_element_type=F32) + b_ref[...]


def _modulation(c, c_ctx, w_mod, b_mod):
    rows, d = c.shape
    width = w_mod.shape[1]
    bn = MOD_COLS
    return pl.pallas_call(
        _mod_kernel,
        grid=(width // bn,),
        in_specs=[
            pl.BlockSpec((rows, d), lambda j: (0, 0)),
            pl.BlockSpec((1, d), lambda j: (0, 0)),
            pl.BlockSpec((d, bn), lambda j: (0, j)),
            pl.BlockSpec((1, bn), lambda j: (0, j)),
        ],
        out_specs=[pl.BlockSpec((rows, 1, bn), lambda j: (0, 0, j)),
                   pl.BlockSpec((1, bn), lambda j: (0, j))],
        out_shape=[jax.ShapeDtypeStruct((rows, 1, width), F32),
                   jax.ShapeDtypeStruct((1, width), F32)],
        name="mod",
    )(c, c_ctx, w_mod, b_mod)


def _modulated_norm(x, ng, sh, sc):
    ms = jnp.mean(x * x, axis=-1, keepdims=True)
    return x * lax.rsqrt(ms + EPS) * (ng * (1.0 + sc)) + sh


def _proj_kernel(xn_ref, shn_ref, scn_ref, ng_ref, w_ref, cos_ref, sin_ref,
                 lng_ref, lnb_ref, rg_ref, ctx_ref, shc_ref, scc_ref,
                 q_ref, k_ref, v_ref, za_ref, uz_ref, vg_ref, ga_ref, gb_ref, kc_ref, vc_ref,
                 hb_ref, wp_ref):
    step = pl.program_id(0)

    @pl.when(step < N_COL_GROUPS)
    def _():
        group = _weight_group(step)
        scale = jnp.where(group == 0, HEAD_DIM ** -0.5, 1.0)
        wp_ref[group] = _pack_rows(w_ref[...] * scale)

    @pl.when((step >= CTX_FIRST_STEP) & (step < PROJ_PROLOGUE_STEPS))
    def _():
        hc = _modulated_norm(ctx_ref[0], ng_ref[...], shc_ref[...], scc_ref[...]).astype(BF16)
        pk = jnp.dot(hc, _unpack_rows(wp_ref[1]), preferred_element_type=F32)
        pv = jnp.dot(hc, _unpack_rows(wp_ref[2]), preferred_element_type=F32)
        for h in range(HEADS):
            kc_ref[0, h] = pk[:, h * HEAD_DIM:(h + 1) * HEAD_DIM].astype(BF16)
            vc_ref[0, h] = pv[:, h * HEAD_DIM:(h + 1) * HEAD_DIM].astype(BF16)

    @pl.when(step == PROJ_PROLOGUE_STEPS - 1)
    def _():
        hb_ref[...] = _modulated_norm(xn_ref[...], ng_ref[...], shn_ref[0], scn_ref[0]).astype(BF16)

    @pl.when(step >= PROJ_PROLOGUE_STEPS)
    def _():
        _proj_rows(xn_ref, shn_ref, scn_ref, ng_ref, wp_ref, cos_ref, sin_ref, lng_ref, lnb_ref, rg_ref,
                   q_ref, k_ref, v_ref, za_ref, uz_ref, vg_ref, ga_ref, gb_ref, hb_ref)


def _weight_group(step):
    return jnp.where(step < 2, step + 1, jnp.where(step == 2, 0, jnp.minimum(step, N_COL_GROUPS - 1)))


def _proj_rows(xn_ref, shn_ref, scn_ref, ng_ref, wp_ref, cos_ref, sin_ref, lng_ref, lnb_ref, rg_ref,
               q_ref, k_ref, v_ref, za_ref, uz_ref, vg_ref, ga_ref, gb_ref, hb_ref):
    hb_next = _modulated_norm(xn_ref[...], ng_ref[...], shn_ref[0], scn_ref[0]).astype(BF16)

    for r in range(PROJ_ROWS // PROJ_SUB_ROWS):
        rows = slice(r * PROJ_SUB_ROWS, (r + 1) * PROJ_SUB_ROWS)
        cos = cos_ref[rows, :]
        sin = sin_ref[rows, :]

        def group(g):
            return jnp.dot(hb_ref[rows, :], _unpack_rows(wp_ref[g]), preferred_element_type=F32)

        def store_rope(p, o_ref, transposed=False):
            pb = p.astype(BF16)
            for h in range(HEADS):
                t1 = pb[:, h * HEAD_DIM:h * HEAD_DIM + HALF]
                t2 = pb[:, h * HEAD_DIM + HALF:(h + 1) * HEAD_DIM]
                if transposed:
                    o_ref[0, h, :HALF, rows] = (t1 * cos - t2 * sin).T
                    o_ref[0, h, HALF:, rows] = (t1 * sin + t2 * cos).T
                else:
                    o_ref[0, h, rows, :HALF] = t1 * cos - t2 * sin
                    o_ref[0, h, rows, HALF:] = t1 * sin + t2 * cos

        def store_heads(pb, o_ref):
            for h in range(HEADS):
                o_ref[0, h, rows, :] = pb[:, h * HEAD_DIM:(h + 1) * HEAD_DIM]

        store_rope(group(0), q_ref)
        store_heads(_silu(group(3)).astype(BF16) * rg_ref[...].astype(BF16), za_ref)
        uz_ref[0, rows, :] = jax.nn.gelu(group(4)).astype(BF16)
        ga_ref[0, rows, :] = jax.nn.sigmoid(group(7)).astype(BF16)
        t = jax.nn.gelu(group(5))
        mu = jnp.mean(t, axis=-1, keepdims=True)
        tc = t - mu
        tn = tc * lax.rsqrt(jnp.mean(tc * tc, axis=-1, keepdims=True) + EPS)
        vg_ref[0, rows, :] = (tn.astype(BF16) * lng_ref[...].astype(BF16)
                              + lnb_ref[...].astype(BF16))
        gb_ref[0, rows, :] = jax.nn.sigmoid(group(8)).astype(BF16)
        store_rope(group(1), k_ref, transposed=True)
        uz_ref[0, rows, :] = uz_ref[0, rows, :] * _silu(group(6)).astype(BF16)
        store_heads(group(2).astype(BF16), v_ref)

    hb_ref[...] = hb_next


def _proj(x, ctx, mod_x, mod_c, ng, w_in, cos, sin, ln_g, ln_b, ret_g):
    b, n, d = x.shape
    tm = PROJ_ROWS
    per_b = n // tm
    steps = b * per_b
    pre = PROJ_PROLOGUE_STEPS
    n_ctx = ctx.shape[1]
    ctx_b = lambda s: jnp.clip(s - CTX_FIRST_STEP, 0, b - 1)
    x2 = x.reshape(b * n, d)
    row = lambda s: jnp.maximum(s - pre, 0)
    nxt = lambda s: jnp.clip(s + 1 - pre, 0, steps - 1)
    head_out = jax.ShapeDtypeStruct((b, HEADS, n, HEAD_DIM), BF16)
    headt_out = jax.ShapeDtypeStruct((b, HEADS, HEAD_DIM, n), BF16)
    flat_out = jax.ShapeDtypeStruct((b, n, d), BF16)
    head_spec = pl.BlockSpec((1, HEADS, tm, HEAD_DIM), lambda s: (row(s) // per_b, 0, row(s) % per_b, 0))
    headt_spec = pl.BlockSpec((1, HEADS, HEAD_DIM, tm), lambda s: (row(s) // per_b, 0, 0, row(s) % per_b))
    flat_spec = pl.BlockSpec((1, tm, d), lambda s: (row(s) // per_b, row(s) % per_b, 0))
    vec = pl.BlockSpec((1, d), lambda s: (0, 0))
    ctx_spec = pl.BlockSpec((1, HEADS, n_ctx, HEAD_DIM), lambda s: (ctx_b(s), 0, 0, 0))
    ctx_out = jax.ShapeDtypeStruct((b, HEADS, n_ctx, HEAD_DIM), BF16)
    assert CTX_FIRST_STEP + b == pre
    return pl.pallas_call(
        _proj_kernel,
        grid=(pre + steps,),
        in_specs=[
            pl.BlockSpec((tm, d), lambda s: (nxt(s), 0)),
            pl.BlockSpec((1, 1, d), lambda s: (nxt(s) // per_b, 0, 0)),
            pl.BlockSpec((1, 1, d), lambda s: (nxt(s) // per_b, 0, 1)),
            vec,
            pl.BlockSpec((d, d), lambda s: (0, _weight_group(s))),
            pl.BlockSpec((tm, HALF), lambda s: (row(s) % per_b, 0)),
            pl.BlockSpec((tm, HALF), lambda s: (row(s) % per_b, 0)),
            vec, vec, vec,
            pl.BlockSpec((1, n_ctx, d), lambda s: (ctx_b(s), 0, 0)),
            pl.BlockSpec((1, d), lambda s: (0, 0)),
            pl.BlockSpec((1, d), lambda s: (0, 1)),
        ],
        out_specs=[head_spec, headt_spec, head_spec, head_spec] + [flat_spec] * 4 + [ctx_spec] * 2,
        out_shape=[head_out, headt_out, head_out, head_out] + [flat_out] * 4 + [ctx_out] * 2,
        scratch_shapes=[pltpu.VMEM((tm, d), BF16),
                        pltpu.VMEM((N_COL_GROUPS, d // 2, d), jnp.uint32)],
        compiler_params=pltpu.CompilerParams(
            dimension_semantics=("arbitrary",), vmem_limit_bytes=VMEM_LIMIT),
        name="proj",
    )(x2, mod_x, mod_x, ng, w_in, cos, sin, ln_g, ln_b, ret_g, ctx, mod_c, mod_c)


def _dot_tn(a, b):
    return lax.dot_general(a, b, (((0,), (0,)), ((), ())), preferred_element_type=F32)


def _ret_kernel(decf_ref, decb_ref, q_ref, kt_ref, v_ref, za_ref, kc_ref, vc_ref, o_ref,
                sf_ref, tf_ref, tb_ref, dmat_ref, kdf_ref, kdb_ref, qdf_ref, qdb_ref, cwf_ref, cwb_ref,
                *, n_chunks, n_ctx):
    C = RET_CHUNK
    heads = range(RET_HEADS_PER_STEP)

    def log_decay(dec_ref, hh):
        x = dec_ref[...]
        ls = jnp.minimum(x, 0.0) - jnp.log1p(jnp.exp(-jnp.abs(x)))
        lane = lax.broadcasted_iota(jnp.int32, x.shape, 1)
        pick = lane == pl.program_id(0) * RET_HEADS_PER_STEP + hh
        return jnp.sum(jnp.where(pick, ls, 0.0), axis=1, keepdims=True)

    lgf = [log_decay(decf_ref, hh) for hh in heads]
    lgb = [log_decay(decb_ref, hh) for hh in heads]

    @pl.when(pl.program_id(1) == 0)
    def _():
        ii = lax.broadcasted_iota(jnp.int32, (C, C), 0)
        jj = lax.broadcasted_iota(jnp.int32, (C, C), 1)
        diff = (ii - jj).astype(F32)
        row = lax.broadcasted_iota(jnp.int32, (C, HEAD_DIM), 0).astype(F32)
        col = lax.broadcasted_iota(jnp.int32, (HEAD_DIM, C), 1).astype(F32)
        crow = lax.broadcasted_iota(jnp.int32, (n_ctx, HEAD_DIM), 0).astype(F32)
        for hh in heads:
            dmat_ref[hh] = jnp.where(diff >= 0.0,
                                     jnp.exp(lgf[hh] * jnp.maximum(diff, 0.0)),
                                     jnp.exp(lgb[hh] * jnp.maximum(-diff, 0.0))).astype(BF16)
            kdf_ref[hh] = jnp.exp(lgf[hh] * (C - 1.0 - col)).astype(BF16)
            kdb_ref[hh] = jnp.exp(lgb[hh] * col).astype(BF16)
            qdf_ref[hh] = jnp.exp(lgf[hh] * (row + 1.0)).astype(BF16)
            qdb_ref[hh] = jnp.exp(lgb[hh] * (C - row)).astype(BF16)
            cwf_ref[hh] = jnp.exp(lgf[hh] * (n_ctx - 1.0 - crow)).astype(BF16)
            cwb_ref[hh] = jnp.exp(lgb[hh] * crow).astype(BF16)

    cdec_f = [jnp.exp(lgf[hh] * C) for hh in heads]
    cdec_b = [jnp.exp(lgb[hh] * C) for hh in heads]

    for hh in heads:
        kc = kc_ref[0, hh]
        vc = vc_ref[0, hh]
        sf_ref[hh] = _dot_tn(kc * cwf_ref[hh], vc)
        tf_ref[hh] = _dot_tn(kc * cwb_ref[hh], vc)

    for c in range(n_chunks - 1, -1, -1):
        sl = slice(c * C, (c + 1) * C)
        for hh in heads:
            tb_ref[hh, c] = tf_ref[hh].astype(BF16)
            if c > 0:
                tf_ref[hh] = cdec_b[hh] * tf_ref[hh] + jnp.dot(
                    kt_ref[0, hh, :, sl] * kdb_ref[hh], v_ref[0, hh, sl, :], preferred_element_type=F32)

    for c in range(n_chunks):
        sl = slice(c * C, (c + 1) * C)
        for hh in heads:
            qq = q_ref[0, hh, sl, :]
            kt = kt_ref[0, hh, :, sl]
            vv = v_ref[0, hh, sl, :]
            p = jnp.dot(qq, kt, preferred_element_type=F32).astype(BF16) * dmat_ref[hh]
            o = (jnp.dot(p, vv, preferred_element_type=F32)
                 + jnp.dot(qq * qdf_ref[hh], sf_ref[hh].astype(BF16), preferred_element_type=F32)
                 + jnp.dot(qq * qdb_ref[hh], tb_ref[hh, c], preferred_element_type=F32))
            if c + 1 < n_chunks:
                sf_ref[hh] = cdec_f[hh] * sf_ref[hh] + jnp.dot(kt * kdf_ref[hh], vv,
                                                               preferred_element_type=F32)
            mu = jnp.mean(o, axis=-1, keepdims=True)
            oc = o - mu
            rstd = lax.rsqrt(jnp.mean(oc * oc, axis=-1, keepdims=True) + EPS)
            o_ref[0, hh, sl, :] = (oc * rstd).astype(BF16) * za_ref[0, hh, sl, :]


def _ret_specs(q, kc):
    b, hh, n, dh = q.shape
    n_ctx = kc.shape[2]
    hp = RET_HEADS_PER_STEP
    seq = pl.BlockSpec((1, hp, n, dh), lambda j, i: (i, j, 0, 0))
    seqt = pl.BlockSpec((1, hp, dh, n), lambda j, i: (i, j, 0, 0))
    cseq = pl.BlockSpec((1, hp, n_ctx, dh), lambda j, i: (i, j, 0, 0))
    vecs = pl.BlockSpec((1, hh), lambda j, i: (0, 0))
    in_specs = [vecs, vecs, seq, seqt, seq, seq, cseq, cseq]
    table = pltpu.VMEM((hp, RET_CHUNK, dh), BF16)
    tablet = pltpu.VMEM((hp, dh, RET_CHUNK), BF16)
    ctable = pltpu.VMEM((hp, n_ctx, dh), BF16)
    scratch = [
        pltpu.VMEM((hp, dh, dh), F32),
        pltpu.VMEM((hp, dh, dh), F32),
        pltpu.VMEM((hp, n // RET_CHUNK, dh, dh), BF16),
        pltpu.VMEM((hp, RET_CHUNK, RET_CHUNK), BF16),
        tablet, tablet, table, table, ctable, ctable,
    ]
    return in_specs, seq, scratch


def _out_weight_copies(w_hbms, stage_ref, sem):
    return [pltpu.make_async_copy(w, stage_ref.at[k], sem.at[k]) for k, w in enumerate(w_hbms)]


def _out_rows(w_hbms, packed, stage_ref, sem, x_ref, ya_ref, uz_ref, vg_ref, ga_ref, gb_ref, gt_ref,
              ws_ref, bias_ref, fg_ref, o_ref):
    @pl.when((pl.program_id(0) == 0) & (pl.program_id(1) == 0))
    def _():
        for k, copy in enumerate(_out_weight_copies(w_hbms, stage_ref, sem)):
            copy.wait()
            packed[k][...] = _pack_rows(stage_ref[k])

    wpa_p, wpb_p, wo_p = packed
    tm = x_ref.shape[1]
    gw = D_MODEL // MLP_GROUPS
    n_sub = OUT_SUB_ROWS // MLP_CHUNK
    for r in range(tm // OUT_SUB_ROWS):
        base = r * OUT_SUB_ROWS
        rows = slice(base, base + OUT_SUB_ROWS)
        mixed = [[None] * MLP_GROUPS for _ in range(n_sub)]
        for g in range(MLP_GROUPS):
            cols = slice(g * gw, (g + 1) * gw)
            rhs = jnp.concatenate(
                [vg_ref[0, base + c * MLP_CHUNK:base + (c + 1) * MLP_CHUNK, cols]
                 for c in range(n_sub)], axis=1)
            m = jnp.dot(ws_ref[g], rhs, preferred_element_type=F32)
            for c in range(n_sub):
                mixed[c][g] = m[:, c * gw:(c + 1) * gw] + bias_ref[:, cols]
        mix = jnp.concatenate([jnp.concatenate(mixed[c], axis=1) for c in range(n_sub)], axis=0)
        yb = (uz_ref[0, rows, :].astype(F32) * mix).astype(BF16)
        ya = jnp.concatenate([ya_ref[0, h, rows, :] for h in range(HEADS)], axis=1)
        pa = jnp.dot(ya, _unpack_rows(wpa_p[...]), preferred_element_type=F32)
        pb = jnp.dot(yb, _unpack_rows(wpb_p[...]), preferred_element_type=F32)
        merged = (ga_ref[0, rows, :].astype(F32) * pa
                  + gb_ref[0, rows, :].astype(F32) * pb).astype(BF16)
        out = jnp.dot(merged, _unpack_rows(wo_p[...]), preferred_element_type=F32)
        xo = x_ref[0, rows, :] + gt_ref[0] * out
        ms = jnp.mean(xo * xo, axis=-1, keepdims=True)
        o_ref[0, rows, :] = xo * lax.rsqrt(ms + EPS) * fg_ref[...]


def _ret_out_kernel(*refs, b, n, n_ctx, n_ret_scratch):
    (decf, decb, q, kt, v, za, kc, vc,
     x, uz, vg, ga, gb, gt, ws, bias, wpa, wpb, wo, fg,
     ya, y) = refs[:22]
    ret_scratch = refs[22:22 + n_ret_scratch]
    wpa_p, wpb_p, wo_p, stage_ref, sem = refs[22 + n_ret_scratch:]
    w_hbms = (wpa, wpb, wo)
    for copy in _out_weight_copies(w_hbms, stage_ref, sem):
        copy.start()

    ret_in_specs, seq, _ = _ret_specs(q, kc)
    pltpu.emit_pipeline(
        functools.partial(_ret_kernel, n_chunks=n // RET_CHUNK, n_ctx=n_ctx),
        grid=(HEADS // RET_HEADS_PER_STEP, b), in_specs=ret_in_specs, out_specs=[seq],
    )(decf, decb, q, kt, v, za, kc, vc, ya, scratches=ret_scratch)

    tm = OUT_ROWS
    d = D_MODEL
    flat = pl.BlockSpec((1, tm, d), lambda i, j: (i, j, 0))
    const2 = lambda shape: pl.BlockSpec(shape, lambda i, j: (0,) * len(shape))
    pltpu.emit_pipeline(
        functools.partial(_out_rows, w_hbms, (wpa_p, wpb_p, wo_p), stage_ref, sem),
        grid=(b, n // tm),
        in_specs=[
            flat,
            pl.BlockSpec((1, HEADS, tm, HEAD_DIM), lambda i, j: (i, 0, j, 0)),
            flat, flat, flat, flat,
            pl.BlockSpec((1, 1, d), lambda i, j: (i, 0, 2)),
            const2(ws.shape), const2(bias.shape), const2((1, d)),
        ],
        out_specs=[flat],
    )(x, ya, uz, vg, ga, gb, gt, ws, bias, fg, y)


def _ret_out(dec_f, dec_b, q, kt, v, za, kc, vc, x, uz, vg, ga, gb, mod_x, ws_b, bias_full, wpa, wpb, wo, fg):
    b, n, d = x.shape
    _, _, ret_scratch = _ret_specs(q, kc)
    any_spec = pl.BlockSpec(memory_space=pl.ANY)
    ya, y = pl.pallas_call(
        functools.partial(_ret_out_kernel, b=b, n=n, n_ctx=kc.shape[2], n_ret_scratch=len(ret_scratch)),
        in_specs=[any_spec] * 20,
        out_specs=[any_spec, any_spec],
        out_shape=[jax.ShapeDtypeStruct(q.shape, BF16), jax.ShapeDtypeStruct((b, n, d), F32)],
        scratch_shapes=ret_scratch + [pltpu.VMEM((d // 2, d), jnp.uint32)] * 3
        + [pltpu.VMEM((3, d, d), F32), pltpu.SemaphoreType.DMA((3,))],
        compiler_params=pltpu.CompilerParams(vmem_limit_bytes=VMEM_LIMIT),
        name="ret_out",
    )(dec_f, dec_b, q, kt, v, za, kc, vc, x, uz, vg, ga, gb, mod_x, ws_b, bias_full, wpa, wpb, wo, fg)
    return y


def kernel(x, c, ctx, c_ctx, w_mod, b_mod, norm_g, w_in, ret_decay_fwd, ret_decay_bwd, ret_norm_g,
           mlp_ln_g, mlp_ln_b, mlp_ws, mlp_bs, w_proj_a, w_proj_b, w_out, final_norm_g):
    b, n, d = x.shape
    assert d == D_MODEL and n % RET_CHUNK == 0 and n % OUT_ROWS == 0 and n % PROJ_ROWS == 0
    assert w_mod.shape[0] == 1, "single-layer block"

    mod_x, mod_c = _modulation(c, c_ctx[None, :], w_mod[0], b_mod[0][None, :])

    ng = norm_g[0][None, :]
    cos, sin = _rope_tables(n)
    w_in2 = w_in.reshape(w_in.shape[1:])

    q, kt, v, za, uz, vg, ga, gb, kc, vc = _proj(
        x, ctx, mod_x, mod_c, ng, w_in2, jnp.asarray(cos).astype(BF16), jnp.asarray(sin).astype(BF16),
        mlp_ln_g[0][None, :], mlp_ln_b[0][None, :], ret_norm_g[0][None, :])
    bias_full = jnp.repeat(mlp_bs[0].T, d // MLP_GROUPS, axis=1)
    return _ret_out(ret_decay_fwd.astype(F32), ret_decay_bwd.astype(F32), q, kt, v, za, kc, vc,
                    x, uz, vg, ga, gb, mod_x, mlp_ws[0].astype(BF16), bias_full,
                    w_proj_a.reshape(d, d), w_proj_b.reshape(d, d), w_out.reshape(d, d),
                    final_norm_g[None, :])
```

```python
import functools

import numpy as np
import jax
import jax.numpy as jnp
from jax import lax
from jax.experimental import pallas as pl
from jax.experimental.pallas import tpu as pltpu

F32 = jnp.float32
BF16 = jnp.bfloat16

D_MODEL = 1024
HEADS = 4
HEAD_DIM = 256
HALF = HEAD_DIM // 2
GRID_W = 64
MLP_GROUPS = 8
MLP_CHUNK = 128
ROPE_BASE = 10000.0
EPS = 1e-6
N_COL_GROUPS = 9

RET_CHUNK = 256
RET_HEADS_PER_STEP = 2
PROJ_ROWS = 512
PROJ_SUB_ROWS = 256
CTX_FIRST_STEP = 2
PROJ_PROLOGUE_STEPS = 10
MOD_COLS = 3072
OUT_ROWS = 1024
OUT_SUB_ROWS = 256
VMEM_LIMIT = 60 * 1024 * 1024


def _rope_tables(n):
    pos = np.arange(n)
    row = (pos // GRID_W).astype(np.float64)
    col = (pos % GRID_W).astype(np.float64)
    quarter = HALF // 2
    inv = np.power(ROPE_BASE, -np.arange(quarter, dtype=np.float64) / quarter)
    ang = np.concatenate([row[:, None] * inv, col[:, None] * inv], axis=-1)
    return np.cos(ang).astype(np.float32), np.sin(ang).astype(np.float32)


def _silu(x):
    return x * jax.nn.sigmoid(x)


def _pack_rows(w):
    return pltpu.bitcast(w.astype(BF16), jnp.uint32)


def _unpack_rows(w32):
    return pltpu.bitcast(w32, BF16)


def _mod_kernel(c_ref, cc_ref, w_ref, b_ref, ox_ref, oc_ref):
    w = w_ref[...].astype(BF16)
    mx = jnp.dot(_silu(c_ref[...]).astype(BF16), w, preferred_element_type=F32) + b_ref[...]
    for i in range(ox_ref.shape[0]):
        ox_ref[i] = mx[i:i + 1, :]
    oc_ref[...] = jnp.dot(_silu(cc_ref[...]).astype(BF16), w, preferred_element_type=F32) + b_ref[...]


def _modulation(c, c_ctx, w_mod, b_mod):
    rows, d = c.shape
    width = w_mod.shape[1]
    bn = MOD_COLS
    return pl.pallas_call(
        _mod_kernel,
        grid=(width // bn,),
        in_specs=[
            pl.BlockSpec((rows, d), lambda j: (0, 0)),
            pl.BlockSpec((1, d), lambda j: (0, 0)),
            pl.BlockSpec((d, bn), lambda j: (0, j)),
            pl.BlockSpec((1, bn), lambda j: (0, j)),
        ],
        out_specs=[pl.BlockSpec((rows, 1, bn), lambda j: (0, 0, j)),
                   pl.BlockSpec((1, bn), lambda j: (0, j))],
        out_shape=[jax.ShapeDtypeStruct((rows, 1, width), F32),
                   jax.ShapeDtypeStruct((1, width), F32)],
        name="mod",
    )(c, c_ctx, w_mod, b_mod)


def _modulated_norm(x, ng, sh, sc):
    ms = jnp.mean(x * x, axis=-1, keepdims=True)
    return x * lax.rsqrt(ms + EPS) * (ng * (1.0 + sc)) + sh


def _proj_kernel(xn_ref, shn_ref, scn_ref, ng_ref, w_ref, cos_ref, sin_ref,
                 lng_ref, lnb_ref, rg_ref, ctx_ref, shc_ref, scc_ref,
                 q_ref, k_ref, v_ref, za_ref, uz_ref, vg_ref, ga_ref, gb_ref, kc_ref, vc_ref,
                 hb_ref, wp_ref):
    step = pl.program_id(0)

    @pl.when(step < N_COL_GROUPS)
    def _():
        group = _weight_group(step)
        scale = jnp.where(group == 0, HEAD_DIM ** -0.5, 1.0)
        wp_ref[group] = _pack_rows(w_ref[...] * scale)

    @pl.when((step >= CTX_FIRST_STEP) & (step < PROJ_PROLOGUE_STEPS))
    def _():
        hc = _modulated_norm(ctx_ref[0], ng_ref[...], shc_ref[...], scc_ref[...]).astype(BF16)
        pk = jnp.dot(hc, _unpack_rows(wp_ref[1]), preferred_element_type=F32)
        pv = jnp.dot(hc, _unpack_rows(wp_ref[2]), preferred_element_type=F32)
        for h in range(HEADS):
            kc_ref[0, h] = pk[:, h * HEAD_DIM:(h + 1) * HEAD_DIM].astype(BF16)
            vc_ref[0, h] = pv[:, h * HEAD_DIM:(h + 1) * HEAD_DIM].astype(BF16)

    @pl.when(step == PROJ_PROLOGUE_STEPS - 1)
    def _():
        hb_ref[...] = _modulated_norm(xn_ref[...], ng_ref[...], shn_ref[0], scn_ref[0]).astype(BF16)

    @pl.when(step >= PROJ_PROLOGUE_STEPS)
    def _():
        _proj_rows(xn_ref, shn_ref, scn_ref, ng_ref, wp_ref, cos_ref, sin_ref, lng_ref, lnb_ref, rg_ref,
                   q_ref, k_ref, v_ref, za_ref, uz_ref, vg_ref, ga_ref, gb_ref, hb_ref)


def _weight_group(step):
    return jnp.where(step < 2, step + 1, jnp.where(step == 2, 0, jnp.minimum(step, N_COL_GROUPS - 1)))


def _proj_rows(xn_ref, shn_ref, scn_ref, ng_ref, wp_ref, cos_ref, sin_ref, lng_ref, lnb_ref, rg_ref,
               q_ref, k_ref, v_ref, za_ref, uz_ref, vg_ref, ga_ref, gb_ref, hb_ref):
    hb_next = _modulated_norm(xn_ref[...], ng_ref[...], shn_ref[0], scn_ref[0]).astype(BF16)

    for r in range(PROJ_ROWS // PROJ_SUB_ROWS):
        rows = slice(r * PROJ_SUB_ROWS, (r + 1) * PROJ_SUB_ROWS)
        cos = cos_ref[rows, :]
        sin = sin_ref[rows, :]

        def group(g):
            return jnp.dot(hb_ref[rows, :], _unpack_rows(wp_ref[g]), preferred_element_type=F32)

        def store_rope(p, o_ref, transposed=False):
            pb = p.astype(BF16)
            for h in range(HEADS):
                t1 = pb[:, h * HEAD_DIM:h * HEAD_DIM + HALF]
                t2 = pb[:, h * HEAD_DIM + HALF:(h + 1) * HEAD_DIM]
                if transposed:
                    o_ref[0, h, :HALF, rows] = (t1 * cos - t2 * sin).T
                    o_ref[0, h, HALF:, rows] = (t1 * sin + t2 * cos).T
                else:
                    o_ref[0, h, rows, :HALF] = t1 * cos - t2 * sin
                    o_ref[0, h, rows, HALF:] = t1 * sin + t2 * cos

        def store_heads(pb, o_ref):
            for h in range(HEADS):
                o_ref[0, h, rows, :] = pb[:, h * HEAD_DIM:(h + 1) * HEAD_DIM]

        store_rope(group(0), q_ref)
        store_heads(_silu(group(3)).astype(BF16) * rg_ref[...].astype(BF16), za_ref)
        uz_ref[0, rows, :] = jax.nn.gelu(group(4)).astype(BF16)
        ga_ref[0, rows, :] = jax.nn.sigmoid(group(7)).astype(BF16)
        t = jax.nn.gelu(group(5))
        mu = jnp.mean(t, axis=-1, keepdims=True)
        tc = t - mu
        tn = tc * lax.rsqrt(jnp.mean(tc * tc, axis=-1, keepdims=True) + EPS)
        vg_ref[0, rows, :] = (tn.astype(BF16) * lng_ref[...].astype(BF16)
                              + lnb_ref[...].astype(BF16))
        gb_ref[0, rows, :] = jax.nn.sigmoid(group(8)).astype(BF16)
        store_rope(group(1), k_ref, transposed=True)
        uz_ref[0, rows, :] = uz_ref[0, rows, :] * _silu(group(6)).astype(BF16)
        store_heads(group(2).astype(BF16), v_ref)

    hb_ref[...] = hb_next


def _proj(x, ctx, mod_x, mod_c, ng, w_in, cos, sin, ln_g, ln_b, ret_g):
    b, n, d = x.shape
    tm = PROJ_ROWS
    per_b = n // tm
    steps = b * per_b
    pre = PROJ_PROLOGUE_STEPS
    n_ctx = ctx.shape[1]
    ctx_b = lambda s: jnp.clip(s - CTX_FIRST_STEP, 0, b - 1)
    x2 = x.reshape(b * n, d)
    row = lambda s: jnp.maximum(s - pre, 0)
    nxt = lambda s: jnp.clip(s + 1 - pre, 0, steps - 1)
    head_out = jax.ShapeDtypeStruct((b, HEADS, n, HEAD_DIM), BF16)
    headt_out = jax.ShapeDtypeStruct((b, HEADS, HEAD_DIM, n), BF16)
    flat_out = jax.ShapeDtypeStruct((b, n, d), BF16)
    head_spec = pl.BlockSpec((1, HEADS, tm, HEAD_DIM), lambda s: (row(s) // per_b, 0, row(s) % per_b, 0))
    headt_spec = pl.BlockSpec((1, HEADS, HEAD_DIM, tm), lambda s: (row(s) // per_b, 0, 0, row(s) % per_b))
    flat_spec = pl.BlockSpec((1, tm, d), lambda s: (row(s) // per_b, row(s) % per_b, 0))
    vec = pl.BlockSpec((1, d), lambda s: (0, 0))
    ctx_spec = pl.BlockSpec((1, HEADS, n_ctx, HEAD_DIM), lambda s: (ctx_b(s), 0, 0, 0))
    ctx_out = jax.ShapeDtypeStruct((b, HEADS, n_ctx, HEAD_DIM), BF16)
    assert CTX_FIRST_STEP + b == pre
    return pl.pallas_call(
        _proj_kernel,
        grid=(pre + steps,),
        in_specs=[
            pl.BlockSpec((tm, d), lambda s: (nxt(s), 0)),
            pl.BlockSpec((1, 1, d), lambda s: (nxt(s) // per_b, 0, 0)),
            pl.BlockSpec((1, 1, d), lambda s: (nxt(s) // per_b, 0, 1)),
            vec,
            pl.BlockSpec((d, d), lambda s: (0, _weight_group(s))),
            pl.BlockSpec((tm, HALF), lambda s: (row(s) % per_b, 0)),
            pl.BlockSpec((tm, HALF), lambda s: (row(s) % per_b, 0)),
            vec, vec, vec,
            pl.BlockSpec((1, n_ctx, d), lambda s: (ctx_b(s), 0, 0)),
            pl.BlockSpec((1, d), lambda s: (0, 0)),
            pl.BlockSpec((1, d), lambda s: (0, 1)),
        ],
        out_specs=[head_spec, headt_spec, head_spec, head_spec] + [flat_spec] * 4 + [ctx_spec] * 2,
        out_shape=[head_out, headt_out, head_out, head_out] + [flat_out] * 4 + [ctx_out] * 2,
        scratch_shapes=[pltpu.VMEM((tm, d), BF16),
                        pltpu.VMEM((N_COL_GROUPS, d // 2, d), jnp.uint32)],
        compiler_params=pltpu.CompilerParams(
            dimension_semantics=("arbitrary",), vmem_limit_bytes=VMEM_LIMIT),
        name="proj",
    )(x2, mod_x, mod_x, ng, w_in, cos, sin, ln_g, ln_b, ret_g, ctx, mod_c, mod_c)


def _dot_tn(a, b):
    return lax.dot_general(a, b, (((0,), (0,)), ((), ())), preferred_element_type=F32)


def _ret_kernel(decf_ref, decb_ref, q_ref, kt_ref, v_ref, za_ref, kc_ref, vc_ref, o_ref,
                sf_ref, tf_ref, tb_ref, dmat_ref, kdf_ref, kdb_ref, qdf_ref, qdb_ref, cwf_ref, cwb_ref,
                *, n_chunks, n_ctx):
    C = RET_CHUNK
    heads = range(RET_HEADS_PER_STEP)

    def log_decay(dec_ref, hh):
        x = dec_ref[...]
        ls = jnp.minimum(x, 0.0) - jnp.log1p(jnp.exp(-jnp.abs(x)))
        lane = lax.broadcasted_iota(jnp.int32, x.shape, 1)
        pick = lane == pl.program_id(0) * RET_HEADS_PER_STEP + hh
        return jnp.sum(jnp.where(pick, ls, 0.0), axis=1, keepdims=True)

    lgf = [log_decay(decf_ref, hh) for hh in heads]
    lgb = [log_decay(decb_ref, hh) for hh in heads]

    @pl.when(pl.program_id(1) == 0)
    def _():
        ii = lax.broadcasted_iota(jnp.int32, (C, C), 0)
        jj = lax.broadcasted_iota(jnp.int32, (C, C), 1)
        diff = (ii - jj).astype(F32)
        row = lax.broadcasted_iota(jnp.int32, (C, HEAD_DIM), 0).astype(F32)
        col = lax.broadcasted_iota(jnp.int32, (HEAD_DIM, C), 1).astype(F32)
        crow = lax.broadcasted_iota(jnp.int32, (n_ctx, HEAD_DIM), 0).astype(F32)
        for hh in heads:
            dmat_ref[hh] = jnp.where(diff >= 0.0,
                                     jnp.exp(lgf[hh] * jnp.maximum(diff, 0.0)),
                                     jnp.exp(lgb[hh] * jnp.maximum(-diff, 0.0))).astype(BF16)
            kdf_ref[hh] = jnp.exp(lgf[hh] * (C - 1.0 - col)).astype(BF16)
            kdb_ref[hh] = jnp.exp(lgb[hh] * col).astype(BF16)
            qdf_ref[hh] = jnp.exp(lgf[hh] * (row + 1.0)).astype(BF16)
            qdb_ref[hh] = jnp.exp(lgb[hh] * (C - row)).astype(BF16)
            cwf_ref[hh] = jnp.exp(lgf[hh] * (n_ctx - 1.0 - crow)).astype(BF16)
            cwb_ref[hh] = jnp.exp(lgb[hh] * crow).astype(BF16)

    cdec_f = [jnp.exp(lgf[hh] * C) for hh in heads]
    cdec_b = [jnp.exp(lgb[hh] * C) for hh in heads]

    for hh in heads:
        kc = kc_ref[0, hh]
        vc = vc_ref[0, hh]
        sf_ref[hh] = _dot_tn(kc * cwf_ref[hh], vc)
        tf_ref[hh] = _dot_tn(kc * cwb_ref[hh], vc)

    for c in range(n_chunks - 1, -1, -1):
        sl = slice(c * C, (c + 1) * C)
        for hh in heads:
            tb_ref[hh, c] = tf_ref[hh].astype(BF16)
            if c > 0:
                tf_ref[hh] = cdec_b[hh] * tf_ref[hh] + jnp.dot(
                    kt_ref[0, hh, :, sl] * kdb_ref[hh], v_ref[0, hh, sl, :], preferred_element_type=F32)

    for c in range(n_chunks):
        sl = slice(c * C, (c + 1) * C)
        for hh in heads:
            qq = q_ref[0, hh, sl, :]
            kt = kt_ref[0, hh, :, sl]
            vv = v_ref[0, hh, sl, :]
            p = jnp.dot(qq, kt, preferred_element_type=F32).astype(BF16) * dmat_ref[hh]
            o = (jnp.dot(p, vv, preferred_element_type=F32)
                 + jnp.dot(qq * qdf_ref[hh], sf_ref[hh].astype(BF16), preferred_element_type=F32)
                 + jnp.dot(qq * qdb_ref[hh], tb_ref[hh, c], preferred_element_type=F32))
            if c + 1 < n_chunks:
                sf_ref[hh] = cdec_f[hh] * sf_ref[hh] + jnp.dot(kt * kdf_ref[hh], vv,
                                                               preferred_element_type=F32)
            mu = jnp.mean(o, axis=-1, keepdims=True)
            oc = o - mu
            rstd = lax.rsqrt(jnp.mean(oc * oc, axis=-1, keepdims=True) + EPS)
            o_ref[0, hh, sl, :] = (oc * rstd).astype(BF16) * za_ref[0, hh, sl, :]


def _retention(dec_f, dec_b, q, kt, v, za, kc, vc):
    b, hh, n, dh = q.shape
    n_ctx = kc.shape[2]
    n_chunks = n // RET_CHUNK
    hp = RET_HEADS_PER_STEP
    seq = pl.BlockSpec((1, hp, n, dh), lambda j, i: (i, j, 0, 0))
    seq3 = pl.BlockSpec((1, hp, n, dh), lambda j, i: (i, j, 0, 0), pipeline_mode=pl.Buffered(3))
    seqt = pl.BlockSpec((1, hp, dh, n), lambda j, i: (i, j, 0, 0), pipeline_mode=pl.Buffered(3))
    cseq = pl.BlockSpec((1, hp, n_ctx, dh), lambda j, i: (i, j, 0, 0))
    table = pltpu.VMEM((hp, RET_CHUNK, dh), BF16)
    tablet = pltpu.VMEM((hp, dh, RET_CHUNK), BF16)
    ctable = pltpu.VMEM((hp, n_ctx, dh), BF16)
    in_specs = [
        pl.BlockSpec(dec_f.shape, lambda j, i: (0, 0)),
        pl.BlockSpec(dec_b.shape, lambda j, i: (0, 0)),
        seq3, seqt, seq3, seq3, cseq, cseq,
    ]
    n_in = len(in_specs)

    def outer(*refs):
        pipeline = pltpu.emit_pipeline(
            functools.partial(_ret_kernel, n_chunks=n_chunks, n_ctx=n_ctx),
            grid=(hh // hp, b), in_specs=in_specs, out_specs=[seq])
        pipeline(*refs[:n_in + 1], scratches=refs[n_in + 1:])

    any_spec = pl.BlockSpec(memory_space=pl.ANY)
    return pl.pallas_call(
        outer,
        in_specs=[any_spec] * n_in,
        out_specs=any_spec,
        out_shape=jax.ShapeDtypeStruct((b, hh, n, dh), BF16),
        scratch_shapes=[
            pltpu.VMEM((hp, dh, dh), F32),
            pltpu.VMEM((hp, dh, dh), F32),
            pltpu.VMEM((hp, n_chunks, dh, dh), BF16),
            pltpu.VMEM((hp, RET_CHUNK, RET_CHUNK), BF16),
            tablet, tablet, table, table, ctable, ctable,
        ],
        compiler_params=pltpu.CompilerParams(vmem_limit_bytes=VMEM_LIMIT),
        name="ret",
    )(dec_f, dec_b, q, kt, v, za, kc, vc)


def _out_kernel(x_ref, ya_ref, uz_ref, vg_ref, ga_ref, gb_ref, gt_ref, ws_ref, bias_ref,
                wpa_ref, wpb_ref, wo_ref, fg_ref, o_ref, wpa_p, wpb_p, wo_p):
    @pl.when((pl.program_id(0) == 0) & (pl.program_id(1) == 0))
    def _():
        wpa_p[...] = _pack_rows(wpa_ref[...])
        wpb_p[...] = _pack_rows(wpb_ref[...])
        wo_p[...] = _pack_rows(wo_ref[...])

    tm = x_ref.shape[1]
    gw = D_MODEL // MLP_GROUPS
    n_sub = OUT_SUB_ROWS // MLP_CHUNK
    for r in range(tm // OUT_SUB_ROWS):
        base = r * OUT_SUB_ROWS
        rows = slice(base, base + OUT_SUB_ROWS)
        mixed = [[None] * MLP_GROUPS for _ in range(n_sub)]
        for g in range(MLP_GROUPS):
            cols = slice(g * gw, (g + 1) * gw)
            rhs = jnp.concatenate(
                [vg_ref[0, base + c * MLP_CHUNK:base + (c + 1) * MLP_CHUNK, cols]
                 for c in range(n_sub)], axis=1)
            m = jnp.dot(ws_ref[g], rhs, preferred_element_type=F32)
            for c in range(n_sub):
                mixed[c][g] = m[:, c * gw:(c + 1) * gw] + bias_ref[:, cols]
        mix = jnp.concatenate([jnp.concatenate(mixed[c], axis=1) for c in range(n_sub)], axis=0)
        yb = (uz_ref[0, rows, :].astype(F32) * mix).astype(BF16)
        ya = jnp.concatenate([ya_ref[0, h, rows, :] for h in range(HEADS)], axis=1)
        pa = jnp.dot(ya, _unpack_rows(wpa_p[...]), preferred_element_type=F32)
        pb = jnp.dot(yb, _unpack_rows(wpb_p[...]), preferred_element_type=F32)
        merged = (ga_ref[0, rows, :].astype(F32) * pa
                  + gb_ref[0, rows, :].astype(F32) * pb).astype(BF16)
        out = jnp.dot(merged, _unpack_rows(wo_p[...]), preferred_element_type=F32)
        xo = x_ref[0, rows, :] + gt_ref[0] * out
        ms = jnp.mean(xo * xo, axis=-1, keepdims=True)
        o_ref[0, rows, :] = xo * lax.rsqrt(ms + EPS) * fg_ref[...]


def _output(x, ya, uz, vg, ga, gb, mod_x, ws_b, bias_full, wpa, wpb, wo, fg):
    b, n, d = x.shape
    tm = OUT_ROWS
    flat = pl.BlockSpec((1, tm, d), lambda i, j: (i, j, 0))
    const2 = lambda shape: pl.BlockSpec(shape, lambda i, j: (0,) * len(shape))
    once = lambda shape: pl.BlockSpec(shape, lambda i, j: (0,) * len(shape),
                                      pipeline_mode=pl.Buffered(1))
    return pl.pallas_call(
        _out_kernel,
        grid=(b, n // tm),
        in_specs=[
            flat,
            pl.BlockSpec((1, HEADS, tm, HEAD_DIM), lambda i, j: (i, 0, j, 0)),
            flat, flat, flat, flat,
            pl.BlockSpec((1, 1, d), lambda i, j: (i, 0, 2)),
            const2(ws_b.shape), const2(bias_full.shape),
            once(wpa.shape), once(wpb.shape), once(wo.shape),
            const2((1, d)),
        ],
        out_specs=flat,
        out_shape=jax.ShapeDtypeStruct((b, n, d), F32),
        scratch_shapes=[pltpu.VMEM((d // 2, d), jnp.uint32)] * 3,
        compiler_params=pltpu.CompilerParams(
            dimension_semantics=("arbitrary", "arbitrary"), vmem_limit_bytes=VMEM_LIMIT),
        name="out",
    )(x, ya, uz, vg, ga, gb, mod_x, ws_b, bias_full, wpa, wpb, wo, fg)


def kernel(x, c, ctx, c_ctx, w_mod, b_mod, norm_g, w_in, ret_decay_fwd, ret_decay_bwd, ret_norm_g,
           mlp_ln_g, mlp_ln_b, mlp_ws, mlp_bs, w_proj_a, w_proj_b, w_out, final_norm_g):
    b, n, d = x.shape
    assert d == D_MODEL and n % RET_CHUNK == 0 and n % OUT_ROWS == 0 and n % PROJ_ROWS == 0
    assert w_mod.shape[0] == 1, "single-layer block"

    mod_x, mod_c = _modulation(c, c_ctx[None, :], w_mod[0], b_mod[0][None, :])

    ng = norm_g[0][None, :]
    cos, sin = _rope_tables(n)
    w_in2 = w_in.reshape(w_in.shape[1:])

    q, kt, v, za, uz, vg, ga, gb, kc, vc = _proj(
        x, ctx, mod_x, mod_c, ng, w_in2, jnp.asarray(cos).astype(BF16), jnp.asarray(sin).astype(BF16),
        mlp_ln_g[0][None, :], mlp_ln_b[0][None, :], ret_norm_g[0][None, :])
    ya = _retention(ret_decay_fwd.astype(F32), ret_decay_bwd.astype(F32), q, kt, v, za, kc, vc)

    bias_full = jnp.repeat(mlp_bs[0].T, d // MLP_GROUPS, axis=1)
    return _output(x, ya, uz, vg, ga, gb, mod_x, mlp_ws[0].astype(BF16), bias_full,
                   w_proj_a.reshape(d, d), w_proj_b.reshape(d, d), w_out.reshape(d, d),
                   final_norm_g[None, :])
```
